```python
import math
import jax, jax.numpy as jnp
from jax import lax
import numpy as np

D_MODEL = 1024
BATCH = 8
SEQ = 2048
DEPTH = 4
DEC_BATCH = 128
DEC_SEQ = 1
PAST_LEN = 16384
PAGE_SIZE = 128

N_MIXERS = 2
N_ML = (DEPTH + 1) // 2
N_GD = DEPTH // 2
N_MEM = 256
D_FF = 2816
FFN_RES = 0.5
EPS = 1e-6
ML_HEADS = 4
ML_DV = D_MODEL // ML_HEADS
ML_DQK = ML_DV // 2
ML_CHUNK = 64
ML_PROJ = 2 * ML_HEADS * ML_DQK + 2 * D_MODEL + 2 * ML_HEADS
GD_HEADS = 8
GD_DK = 128
GD_DV = D_MODEL // GD_HEADS
GD_CONV = 4
GD_CHUNK = 64
GD_QKV = 2 * GD_HEADS * GD_DK + GD_HEADS * GD_DV
GD_PROJ = GD_QKV + GD_HEADS * GD_DV + 2 * GD_HEADS
XA_HEADS = 4
XA_DH = D_MODEL // XA_HEADS

kernel_name = 'hybrid_mlstm_gdn_macaron_memory_step'


def rms_norm(x, g):
    xf = x.astype(jnp.float32)
    y = xf * lax.rsqrt(jnp.mean(xf * xf, axis=-1, keepdims=True) + EPS)
    return (y * g.astype(jnp.float32)).astype(x.dtype)


def head_rms(x, g):
    y = x * lax.rsqrt(jnp.mean(x * x, axis=-1, keepdims=True) + EPS)
    return y * g.astype(jnp.float32)


def l2_normalize(x):
    return x * lax.rsqrt(jnp.sum(x * x, axis=-1, keepdims=True) + EPS)


def swiglu(h, w_gate, w_up, w_down):
    return (jax.nn.silu(h @ w_gate) * (h @ w_up)) @ w_down


def pad_time(a, pad, value):
    if pad == 0:
        return a
    widths = [(0, 0)] * a.ndim
    widths[1] = (0, pad)
    return jnp.pad(a, widths, constant_values=value)


def chunk_layout(length, chunk):
    c = min(chunk, length)
    n_chunks = -(-length // c)
    return c, n_chunks, n_chunks * c - length


def to_chunks(a, n_chunks, c):
    b = a.shape[0]
    return jnp.moveaxis(a.reshape((b, n_chunks, c) + a.shape[2:]), 1, 0)


def from_chunks(a, length):
    n_chunks, b, c = a.shape[:3]
    return jnp.moveaxis(a, 0, 1).reshape((b, n_chunks * c) + a.shape[3:])[:, :length]


def mlstm_chunked(q, k, v, i_pre, log_f, C0, n0, m0):
    length = q.shape[1]
    c, n_chunks, pad = chunk_layout(length, ML_CHUNK)
    xs = (pad_time(q, pad, 0.0), pad_time(k, pad, 0.0), pad_time(v, pad, 0.0),
          pad_time(i_pre, pad, -jnp.inf), pad_time(log_f, pad, 0.0))
    xs = tuple(to_chunks(a, n_chunks, c) for a in xs)
    incl = jnp.tril(jnp.ones((c, c), dtype=bool))

    def step(carry, inp):
        C, nv, m = carry
        qc, kc, vc, ic, fc = [jnp.moveaxis(a, 2, 1) for a in inp]
        b = jnp.cumsum(fc, axis=-1)
        a_inter = b + m[..., None]
        dmat = jnp.where(incl, b[..., :, None] - b[..., None, :] + ic[..., None, :], -jnp.inf)
        m_t = jnp.maximum(a_inter, jnp.max(dmat, axis=-1))
        w_inter = jnp.exp(a_inter - m_t)
        s_qk = jnp.einsum('bhtd,bhsd->bhts', qc, kc) * jnp.exp(dmat - m_t[..., None])
        num = w_inter[..., None] * jnp.einsum('bhtd,bhde->bhte', qc, C) + jnp.einsum('bhts,bhse->bhte', s_qk, vc)
        den = w_inter * jnp.einsum('bhtd,bhd->bht', qc, nv) + jnp.sum(s_qk, axis=-1)
        h = num / jnp.maximum(jnp.abs(den), jnp.exp(-m_t))[..., None]
        m_new = m_t[..., -1]
        w_k = jnp.exp(b[..., -1:] - b + ic - m_new[..., None])
        carry_decay = jnp.exp(b[..., -1] + m - m_new)
        C_new = carry_decay[..., None, None] * C + jnp.einsum('bhs,bhsd,bhse->bhde', w_k, kc, vc)
        n_new = carry_decay[..., None] * nv + jnp.einsum('bhs,bhsd->bhd', w_k, kc)
        return (C_new, n_new, m_new), jnp.moveaxis(h, 1, 2)

    (C, nv, m), hs = lax.scan(step, (C0, n0, m0), xs)
    return from_chunks(hs, length), C, nv, m


def gdn_chunked(q, k, v, beta, log_g, S0):
    length = q.shape[1]
    c, n_chunks, pad = chunk_layout(length, GD_CHUNK)
    xs = tuple(to_chunks(pad_time(a, pad, 0.0), n_chunks, c) for a in (q, k, v, beta, log_g))
    incl = jnp.tril(jnp.ones((c, c), dtype=bool))
    strict = jnp.tril(jnp.ones((c, c), dtype=bool), -1)
    eye = jnp.eye(c, dtype=jnp.float32)

    def step(S, inp):
        qc, kc, vc, bc, gc = [jnp.moveaxis(a, 2, 1) for a in inp]
        gam = jnp.cumsum(gc, axis=-1)
        decay = jnp.exp(jnp.where(incl, gam[..., :, None] - gam[..., None, :], -jnp.inf))
        kk = jnp.einsum('bhtd,bhsd->bhts', kc, kc)
        a_mat = jnp.where(strict, bc[..., :, None] * kk * decay, 0.0) + eye
        u = lax.linalg.triangular_solve(a_mat, bc[..., None] * vc, left_side=True, lower=True, unit_diagonal=True)
        w = lax.linalg.triangular_solve(a_mat, (bc * jnp.exp(gam))[..., None] * kc,
                                        left_side=True, lower=True, unit_diagonal=True)
        u = u - jnp.einsum('bhtd,bhde->bhte', w, S)
        qk = jnp.einsum('bhtd,bhsd->bhts', qc, kc) * decay
        o = jnp.einsum('bhtd,bhde->bhte', qc * jnp.exp(gam)[..., None], S) + jnp.einsum('bhts,bhse->bhte', qk, u)
        g_last = gam[..., -1]
        k_dec = kc * jnp.exp(g_last[..., None] - gam)[..., None]
        S_new = jnp.exp(g_last)[..., None, None] * S + jnp.einsum('bhsd,bhse->bhde', k_dec, u)
        return S_new, jnp.moveaxis(o, 1, 2)

    S, os_ = lax.scan(step, S0, xs)
    return from_chunks(os_, length), S


def mlstm_mixer(h, w_in, b_i, b_f, g_head, w_out, C0, n0, m0):
    bsz, length, _ = h.shape
    f32 = jnp.float32
    hq = ML_HEADS * ML_DQK
    p = h @ w_in
    q, k, v, o, ig, fg = jnp.split(
        p, [hq, 2 * hq, 2 * hq + D_MODEL, 2 * hq + 2 * D_MODEL, 2 * hq + 2 * D_MODEL + ML_HEADS], axis=-1)
    q = q.astype(f32).reshape(bsz, length, ML_HEADS, ML_DQK)
    k = k.astype(f32).reshape(bsz, length, ML_HEADS, ML_DQK) * ML_DQK ** -0.5
    v = v.astype(f32).reshape(bsz, length, ML_HEADS, ML_DV)
    i_pre = ig.astype(f32) + b_i.astype(f32)
    log_f = jax.nn.log_sigmoid(fg.astype(f32) + b_f.astype(f32))
    hh, C, n, m = mlstm_chunked(q, k, v, i_pre, log_f, C0.astype(f32), n0.astype(f32), m0.astype(f32))
    hh = head_rms(hh, g_head.reshape(ML_HEADS, ML_DV)).reshape(bsz, length, D_MODEL)
    out = (hh * jax.nn.sigmoid(o.astype(f32))).astype(h.dtype)
    return out @ w_out, C, n, m


def gdn_mixer(h, conv_prev, w_in, conv_w, a_log, dt_bias, g_out, w_out, S0):
    bsz, length, _ = h.shape
    f32 = jnp.float32
    p = h @ w_in
    qkv, z, b_pre, a_pre = jnp.split(p, [GD_QKV, GD_QKV + D_MODEL, GD_QKV + D_MODEL + GD_HEADS], axis=-1)
    ext = jnp.concatenate([conv_prev.astype(qkv.dtype), qkv], axis=1)
    conv = ext[:, 0:length] * conv_w[0]
    for j in range(1, GD_CONV):
        conv = conv + ext[:, j:j + length] * conv_w[j]
    conv = jax.nn.silu(conv).astype(f32)
    new_conv = ext[:, length:]
    q, k, v = jnp.split(conv, [GD_HEADS * GD_DK, 2 * GD_HEADS * GD_DK], axis=-1)
    q = l2_normalize(q.reshape(bsz, length, GD_HEADS, GD_DK)) * GD_DK ** -0.5
    k = l2_normalize(k.reshape(bsz, length, GD_HEADS, GD_DK))
    v = v.reshape(bsz, length, GD_HEADS, GD_DV)
    beta = jax.nn.sigmoid(b_pre.astype(f32))
    log_g = -jnp.exp(a_log.astype(f32)) * jax.nn.softplus(a_pre.astype(f32) + dt_bias.astype(f32))
    o, S = gdn_chunked(q, k, v, beta, log_g, S0.astype(f32))
    o = head_rms(o, g_out) * jax.nn.silu(z.astype(f32).reshape(bsz, length, GD_HEADS, GD_DV))
    return o.reshape(bsz, length, D_MODEL).astype(h.dtype) @ w_out, S, new_conv


def mem_kv(mem, g_mem, w_k, w_v):
    bsz = mem.shape[0]
    m = rms_norm(mem, g_mem)
    k = (m @ w_k).reshape(bsz, N_MEM, XA_HEADS, XA_DH)
    v = (m @ w_v).reshape(bsz, N_MEM, XA_HEADS, XA_DH)
    return k, v


def cross_attend(h, k, v, w_q, w_o):
    bsz, length, _ = h.shape
    q = (h @ w_q).reshape(bsz, length, XA_HEADS, XA_DH)
    s = jnp.einsum('blhd,bmhd->bhlm', q, k.astype(q.dtype)).astype(jnp.float32) * XA_DH ** -0.5
    p = jax.nn.softmax(s, axis=-1).astype(h.dtype)
    o = jnp.einsum('bhlm,bmhd->blhd', p, v.astype(h.dtype)).reshape(bsz, length, D_MODEL)
    return o @ w_o


def _trunk(x, mem_k, mem_v, ml_C, ml_n, ml_m, gd_S, gd_conv, P):
    new_C, new_n, new_m, new_S, new_conv = [], [], [], [], []
    for layer in range(DEPTH):
        j = layer // N_MIXERS
        x = x + FFN_RES * swiglu(rms_norm(x, P['g_ffn1'][layer]), P['ffn1_w_gate'][layer],
                                 P['ffn1_w_up'][layer], P['ffn1_w_down'][layer])
        h = rms_norm(x, P['g_mix'][layer])
        if layer % N_MIXERS == 0:
            y, C, n, m = mlstm_mixer(h, P['ml_w_in'][j], P['ml_b_i'][j], P['ml_b_f'][j], P['ml_g_head'][j],
                                     P['ml_w_out'][j], ml_C[j], ml_n[j], ml_m[j])
            new_C.append(C)
            new_n.append(n)
            new_m.append(m)
        else:
            y, S, cv = gdn_mixer(h, gd_conv[j], P['gd_w_in'][j], P['gd_conv_w'][j], P['gd_a_log'][j],
                                 P['gd_dt_bias'][j], P['gd_g_out'][j], P['gd_w_out'][j], gd_S[j])
            new_S.append(S)
            new_conv.append(cv)
        x = x + y
        x = x + cross_attend(rms_norm(x, P['g_xattn'][layer]), mem_k[layer], mem_v[layer],
                             P['xa_w_q'][layer], P['xa_w_o'][layer])
        x = x + FFN_RES * swiglu(rms_norm(x, P['g_ffn2'][layer]), P['ffn2_w_gate'][layer],
                                 P['ffn2_w_up'][layer], P['ffn2_w_down'][layer])
    y = rms_norm(x, P['g_final'])
    return y, jnp.stack(new_C), jnp.stack(new_n), jnp.stack(new_m), jnp.stack(new_S), jnp.stack(new_conv)


def setup_inputs(seed: int = 0) -> dict:
    key = jax.random.key(seed)
    ks = iter(jax.random.split(key, 48))

    def nrm(shape, scale):
        return jax.random.normal(next(ks), shape, jnp.float32) * scale

    def unif(shape, lo, hi):
        return jax.random.uniform(next(ks), shape, jnp.float32, lo, hi)

    def gain(shape):
        return 1.0 + nrm(shape, 0.02)

    D = D_MODEL
    s_d = D ** -0.5
    s_ff = D_FF ** -0.5
    dt = jnp.exp(unif((N_GD, GD_HEADS), math.log(1e-3), math.log(1e-1)))
    return {
        'x_prompt': nrm((BATCH, SEQ, D), 1.0),
        'x_sample': nrm((DEC_BATCH, DEC_SEQ, D), 1.0),
        'mem_prompt': nrm((BATCH, N_MEM, D), 1.0),
        'cache_mem_k': nrm((DEPTH, DEC_BATCH, N_MEM, XA_HEADS, XA_DH), 1.0),
        'cache_mem_v': nrm((DEPTH, DEC_BATCH, N_MEM, XA_HEADS, XA_DH), 1.0),
        'state_mlstm_C': nrm((N_ML, DEC_BATCH, ML_HEADS, ML_DQK, ML_DV), 0.1),
        'state_mlstm_n': nrm((N_ML, DEC_BATCH, ML_HEADS, ML_DQK), 0.1),
        'state_mlstm_m': nrm((N_ML, DEC_BATCH, ML_HEADS), 1.0),
        'state_gdn_S': nrm((N_GD, DEC_BATCH, GD_HEADS, GD_DK, GD_DV), 0.1),
        'state_gdn_conv': nrm((N_GD, DEC_BATCH, GD_CONV - 1, GD_QKV), 1.0),
        'g_ffn1': gain((DEPTH, D)),
        'ffn1_w_gate': nrm((DEPTH, D, D_FF), s_d),
        'ffn1_w_up': nrm((DEPTH, D, D_FF), s_d),
        'ffn1_w_down': nrm((DEPTH, D_FF, D), s_ff),
        'g_mix': gain((DEPTH, D)),
        'ml_w_in': nrm((N_ML, D, ML_PROJ), s_d),
        'ml_b_i': -2.0 + nrm((N_ML, ML_HEADS), 0.1),
        'ml_b_f': unif((N_ML, ML_HEADS), 3.0, 6.0),
        'ml_g_head': gain((N_ML, ML_HEADS * ML_DV)),
        'ml_w_out': nrm((N_ML, D, D), s_d),
        'gd_w_in': nrm((N_GD, D, GD_PROJ), s_d),
        'gd_conv_w': nrm((N_GD, GD_CONV, GD_QKV), GD_CONV ** -0.5),
        'gd_a_log': jnp.log(unif((N_GD, GD_HEADS), 1.0, 16.0)),
        'gd_dt_bias': dt + jnp.log(-jnp.expm1(-dt)),
        'gd_g_out': gain((N_GD, GD_DV)),
        'gd_w_out': nrm((N_GD, D, D), s_d),
        'g_xattn': gain((DEPTH, D)),
        'g_mem': gain((DEPTH, D)),
        'xa_w_q': nrm((DEPTH, D, D), s_d),
        'xa_w_k': nrm((DEPTH, D, D), s_d),
        'xa_w_v': nrm((DEPTH, D, D), s_d),
        'xa_w_o': nrm((DEPTH, D, D), s_d),
        'g_ffn2': gain((DEPTH, D)),
        'ffn2_w_gate': nrm((DEPTH, D, D_FF), s_d),
        'ffn2_w_up': nrm((DEPTH, D, D_FF), s_d),
        'ffn2_w_down': nrm((DEPTH, D_FF, D), s_ff),
        'g_final': gain((D,)),
    }


def reference(x_prompt, x_sample, mem_prompt, cache_mem_k, cache_mem_v, state_mlstm_C, state_mlstm_n,
              state_mlstm_m, state_gdn_S, state_gdn_conv, g_ffn1, ffn1_w_gate, ffn1_w_up, ffn1_w_down, g_mix,
              ml_w_in, ml_b_i, ml_b_f, ml_g_head, ml_w_out, gd_w_in, gd_conv_w, gd_a_log, gd_dt_bias, gd_g_out,
              gd_w_out, g_xattn, g_mem, xa_w_q, xa_w_k, xa_w_v, xa_w_o, g_ffn2, ffn2_w_gate, ffn2_w_up,
              ffn2_w_down, g_final):
    P = {
        'g_ffn1': g_ffn1, 'ffn1_w_gate': ffn1_w_gate, 'ffn1_w_up': ffn1_w_up, 'ffn1_w_down': ffn1_w_down,
        'g_mix': g_mix,
        'ml_w_in': ml_w_in, 'ml_b_i': ml_b_i, 'ml_b_f': ml_b_f, 'ml_g_head': ml_g_head, 'ml_w_out': ml_w_out,
        'gd_w_in': gd_w_in, 'gd_conv_w': gd_conv_w, 'gd_a_log': gd_a_log, 'gd_dt_bias': gd_dt_bias,
        'gd_g_out': gd_g_out, 'gd_w_out': gd_w_out,
        'g_xattn': g_xattn, 'xa_w_q': xa_w_q, 'xa_w_o': xa_w_o,
        'g_ffn2': g_ffn2, 'ffn2_w_gate': ffn2_w_gate, 'ffn2_w_up': ffn2_w_up, 'ffn2_w_down': ffn2_w_down,
        'g_final': g_final,
    }
    f32 = jnp.float32
    mks, mvs = [], []
    for layer in range(DEPTH):
        k, v = mem_kv(mem_prompt, g_mem[layer], xa_w_k[layer], xa_w_v[layer])
        mks.append(k)
        mvs.append(v)
    prompt_mem_k = jnp.stack(mks)
    prompt_mem_v = jnp.stack(mvs)
    zC = jnp.zeros((N_ML, BATCH, ML_HEADS, ML_DQK, ML_DV), f32)
    zn = jnp.zeros((N_ML, BATCH, ML_HEADS, ML_DQK), f32)
    zm = jnp.zeros((N_ML, BATCH, ML_HEADS), f32)
    zS = jnp.zeros((N_GD, BATCH, GD_HEADS, GD_DK, GD_DV), f32)
    zconv = jnp.zeros((N_GD, BATCH, GD_CONV - 1, GD_QKV), x_prompt.dtype)
    y_prompt, p_C, p_n, p_m, p_S, p_conv = _trunk(x_prompt, prompt_mem_k, prompt_mem_v, zC, zn, zm, zS, zconv, P)
    y_sample, s_C, s_n, s_m, s_S, s_conv = _trunk(x_sample, cache_mem_k, cache_mem_v, state_mlstm_C, state_mlstm_n,
                                                  state_mlstm_m, state_gdn_S, state_gdn_conv, P)
    return (y_prompt, y_sample, prompt_mem_k, prompt_mem_v, p_C, p_n, p_m, p_S, p_conv, s_C, s_n, s_m, s_S, s_conv)
```

```python
import functools

import jax
import jax.numpy as jnp
from jax import lax
from jax.experimental import pallas as pl
from jax.experimental.pallas import tpu as pltpu

F32 = jnp.float32
BF16 = jnp.bfloat16

D_MODEL = 1024
DEPTH = 4
N_MEM = 256
D_FF = 2816
FFN_RES = 0.5
EPS = 1e-6
ML_HEADS = 4
ML_DV = 256
ML_DQK = 128
GD_HEADS = 8
GD_DK = 128
GD_DV = 128
GD_CONV = 4
GD_QKV = 3072
XA_HEADS = 4
XA_DH = 256
CHUNK = 64
LANES = 128
SUBLANES = 8
ML_PROJ_PAD = 2 * ML_HEADS * ML_DQK + 2 * D_MODEL + 2 * LANES
GD_PROJ_PAD = GD_QKV + D_MODEL + 2 * LANES
VMEM_LIMIT_BYTES = 56 * 1024 * 1024
HI = lax.Precision.HIGHEST


def _params():
    return pltpu.CompilerParams(vmem_limit_bytes=VMEM_LIMIT_BYTES)


def _rms(x, g):
    return x * lax.rsqrt(jnp.mean(x * x, axis=-1, keepdims=True) + EPS) * g


def _silu(x):
    return x * jax.nn.sigmoid(x)


def _softplus(x):
    return jnp.maximum(x, 0.0) + jnp.log1p(jnp.exp(-jnp.abs(x)))


def _dot(a, b):
    return jnp.dot(a, b, preferred_element_type=F32)


def _dot_nt(a, b):
    return lax.dot_general(a, b, (((1,), (1,)), ((), ())), preferred_element_type=F32)


def _dot_tn(a, b):
    return lax.dot_general(a, b, (((0,), (0,)), ((), ())), preferred_element_type=F32)


def _resident(shape):
    nd = len(shape)
    return pl.BlockSpec(shape, lambda *_: (0,) * nd, pipeline_mode=pl.Buffered(1))


def _ffn_kernel(x_ref, g_ref, wg_ref, wu_ref, wd_ref, *rest, fchunk, final):
    if final:
        gf_ref, o_ref, y_ref, h_ref, a_ref = rest
    else:
        o_ref, h_ref, a_ref = rest
    h_ref[...] = _rms(x_ref[...], g_ref[...]).astype(BF16)
    for j in range(D_FF // fchunk):
        sl = slice(j * fchunk, (j + 1) * fchunk)
        h = h_ref[...]
        gate = _dot(h, wg_ref[:, sl])
        up = _dot(h, wu_ref[:, sl])
        a_ref[:, sl] = (_silu(gate) * up).astype(BF16)
    out = x_ref[...] + FFN_RES * _dot(a_ref[...], wd_ref[...])
    o_ref[...] = out
    if final:
        y_ref[...] = _rms(out, gf_ref[...])


def _ffn(x, g, wg, wu, wd, g_final=None):
    m = x.shape[0]
    tm = min(m, 512)
    final = g_final is not None
    row = pl.BlockSpec((tm, D_MODEL), lambda i: (i, 0))
    in_specs = [row, _resident((1, D_MODEL)), _resident((D_MODEL, D_FF)), _resident((D_MODEL, D_FF)),
                _resident((D_FF, D_MODEL))]
    args = [x, g.reshape(1, D_MODEL), wg, wu, wd]
    out_shape = jax.ShapeDtypeStruct((m, D_MODEL), F32)
    out_specs = row
    if final:
        in_specs.append(_resident((1, D_MODEL)))
        args.append(g_final.reshape(1, D_MODEL))
        out_shape = (out_shape, out_shape)
        out_specs = (row, row)
    return pl.pallas_call(
        functools.partial(_ffn_kernel, fchunk=256, final=final),
        out_shape=out_shape, grid=(m // tm,), in_specs=in_specs, out_specs=out_specs,
        scratch_shapes=[pltpu.VMEM((tm, D_MODEL), BF16), pltpu.VMEM((tm, D_FF), BF16)],
        compiler_params=_params(), name="ffn_final" if final else "ffn",
    )(*args)


def _proj_kernel(*refs, norm, res, nchunk):
    refs = list(refs)
    x_ref = refs.pop(0)
    g_ref = refs.pop(0) if norm else None
    w_ref = refs.pop(0)
    r_ref = refs.pop(0) if res else None
    o_ref = refs.pop(0)
    if norm:
        h = _rms(x_ref[...], g_ref[...]).astype(BF16)
    else:
        h = x_ref[...].astype(BF16)
    n = w_ref.shape[1]
    for n0 in range(0, n, nchunk):
        n1 = min(n, n0 + nchunk)
        y = _dot(h, w_ref[:, n0:n1])
        if res:
            y = y + r_ref[:, n0:n1]
        o_ref[:, n0:n1] = y.astype(o_ref.dtype)


def _proj(x, w, g=None, res=None, out_dtype=F32):
    m, k = x.shape
    n = w.shape[1]
    tm = min(m, 512)
    in_specs = [pl.BlockSpec((tm, k), lambda i: (i, 0))]
    args = [x]
    if g is not None:
        in_specs.append(_resident((1, k)))
        args.append(g.reshape(1, k))
    in_specs.append(_resident((k, n)))
    args.append(w)
    if res is not None:
        in_specs.append(pl.BlockSpec((tm, n), lambda i: (i, 0)))
        args.append(res)
    return pl.pallas_call(
        functools.partial(_proj_kernel, norm=g is not None, res=res is not None, nchunk=512),
        out_shape=jax.ShapeDtypeStruct((m, n), out_dtype), grid=(m // tm,), in_specs=in_specs,
        out_specs=pl.BlockSpec((tm, n), lambda i: (i, 0)),
        compiler_params=_params(), name="proj",
    )(*args)


def _memkv_kernel(x_ref, g_ref, wk_ref, wv_ref, k_ref, v_ref):
    h = _rms(x_ref[...], g_ref[0]).astype(BF16)
    k_ref[0] = _dot(h, wk_ref[0])
    v_ref[0] = _dot(h, wv_ref[0])


def _memkv(mem, g_mem, wk, wv):
    m = mem.shape[0]
    tm = 512
    w_spec = pl.BlockSpec((1, D_MODEL, D_MODEL), lambda l, i: (l, 0, 0))
    o_spec = pl.BlockSpec((1, tm, D_MODEL), lambda l, i: (l, i, 0))
    o_shape = jax.ShapeDtypeStruct((DEPTH, m, D_MODEL), F32)
    return pl.pallas_call(
        _memkv_kernel, out_shape=(o_shape, o_shape), grid=(DEPTH, m // tm),
        in_specs=[pl.BlockSpec((tm, D_MODEL), lambda l, i: (i, 0)),
                  pl.BlockSpec((1, 1, D_MODEL), lambda l, i: (l, 0, 0)), w_spec, w_spec],
        out_specs=(o_spec, o_spec), compiler_params=_params(), name="memkv",
    )(mem, g_mem.reshape(DEPTH, 1, D_MODEL), wk, wv)


def _attn_kernel(q_ref, k_ref, v_ref, o_ref):
    for h in range(XA_HEADS):
        sl = slice(h * XA_DH, (h + 1) * XA_DH)
        q = q_ref[0, :, sl].astype(BF16)
        k = k_ref[0, 0, :, sl].astype(BF16)
        v = v_ref[0, 0, :, sl].astype(BF16)
        s = _dot_nt(q, k) * XA_DH ** -0.5
        e = jnp.exp(s - jnp.max(s, axis=-1, keepdims=True))
        p = e / jnp.sum(e, axis=-1, keepdims=True)
        o_ref[0, :, sl] = _dot(p.astype(BF16), v).astype(o_ref.dtype)


def _attn(q, mem_k, mem_v, layer):
    b, l, _ = q.shape
    tq = min(l, 512)
    kv_spec = pl.BlockSpec((1, 1, N_MEM, D_MODEL), lambda i, j: (layer, i, 0, 0))
    q_spec = pl.BlockSpec((1, tq, D_MODEL), lambda i, j: (i, j, 0))
    return pl.pallas_call(
        _attn_kernel, out_shape=jax.ShapeDtypeStruct((b, l, D_MODEL), BF16), grid=(b, l // tq),
        in_specs=[q_spec, kv_spec, kv_spec], out_specs=q_spec,
        compiler_params=_params(), name="attn",
    )(q, mem_k, mem_v)


def _log_sigmoid(x):
    return jnp.minimum(x, 0.0) - jnp.log1p(jnp.exp(-jnp.abs(x)))


def _lane_select(h, value, into):
    lane = lax.broadcasted_iota(jnp.int32, into.shape, 1)
    return jnp.where(lane == h, value, into)


def _mlstm_chunk_kernel(p_ref, c0_ref, n0_ref, m0_ref, bi_ref, bf_ref, gh_ref,
                        o_ref, c_ref, n_ref, m_ref, *, c):
    @pl.when(pl.program_id(1) == 0)
    def _():
        c_ref[0] = c0_ref[0, 0]
        n_ref[0] = n0_ref[0, 0]
        m_ref[0] = m0_ref[0, 0]

    hq = ML_HEADS * ML_DQK
    i_pre = p_ref[0, :, 2 * hq + 2 * D_MODEL:2 * hq + 2 * D_MODEL + LANES] + bi_ref[...]
    log_f = _log_sigmoid(p_ref[0, :, 2 * hq + 2 * D_MODEL + LANES:ML_PROJ_PAD] + bf_ref[...])
    row = lax.broadcasted_iota(jnp.int32, (c, c), 0)
    col = lax.broadcasted_iota(jnp.int32, (c, c), 1)
    incl = row >= col
    b = jnp.dot(incl.astype(F32), log_f, precision=HI, preferred_element_type=F32)
    m_prev = m_ref[0]
    a_inter = b + m_prev
    xt = (i_pre - b).T
    b_last = b[c - 1:c, :]
    m_row = m_prev
    for h in range(ML_HEADS):
        hs = slice(h, h + 1)
        bcol = b[:, hs]
        dm = jnp.where(incl, bcol + xt[hs, :], -jnp.inf)
        mt = jnp.maximum(a_inter[:, hs], jnp.max(dm, axis=-1, keepdims=True))
        w_inter = jnp.exp(a_inter[:, hs] - mt)
        q = p_ref[0, :, h * ML_DQK:(h + 1) * ML_DQK]
        k = p_ref[0, :, hq + h * ML_DQK:hq + (h + 1) * ML_DQK] * ML_DQK ** -0.5
        vb = p_ref[0, :, 2 * hq + h * ML_DV:2 * hq + (h + 1) * ML_DV].astype(BF16)
        qb = q.astype(BF16)
        s = _dot_nt(qb, k.astype(BF16)) * jnp.exp(dm - mt)
        c_h = c_ref[0, h]
        n_h = n_ref[0, hs, :]
        num = w_inter * _dot(qb, c_h.astype(BF16)) + _dot(s.astype(BF16), vb)
        den = w_inter * jnp.sum(q * n_h, axis=-1, keepdims=True) + jnp.sum(s, axis=-1, keepdims=True)
        hh = num / jnp.maximum(jnp.abs(den), jnp.exp(-mt))
        m_new = mt[c - 1:c, :]
        w_k = jnp.exp(b_last[:, hs] - bcol + i_pre[:, hs] - m_new)
        decay = jnp.exp(b_last[:, hs] + m_prev[:, hs] - m_new)
        kw = k * w_k
        c_ref[0, h] = decay * c_h + _dot_tn(kw.astype(BF16), vb)
        n_ref[0, hs, :] = decay * n_h + jnp.sum(kw, axis=0, keepdims=True)
        m_row = _lane_select(h, m_new, m_row)
        vs = slice(h * ML_DV, (h + 1) * ML_DV)
        hn = hh * lax.rsqrt(jnp.mean(hh * hh, axis=-1, keepdims=True) + EPS) * gh_ref[:, vs]
        og = p_ref[0, :, 2 * hq + D_MODEL + h * ML_DV:2 * hq + D_MODEL + (h + 1) * ML_DV]
        o_ref[0, :, vs] = (hn * jax.nn.sigmoid(og)).astype(o_ref.dtype)
    m_ref[0] = m_row


def _mlstm_state_specs(bsz, layer_j, idx):
    c_in = pl.BlockSpec((1, 1, ML_HEADS, ML_DQK, ML_DV), lambda *g: (layer_j, idx(*g), 0, 0, 0))
    n_in = pl.BlockSpec((1, 1, ML_HEADS, ML_DQK), lambda *g: (layer_j, idx(*g), 0, 0))
    m_in = pl.BlockSpec((1, 1, 1, LANES), lambda *g: (layer_j, idx(*g), 0, 0))
    c_out = pl.BlockSpec((1, ML_HEADS, ML_DQK, ML_DV), lambda *g: (idx(*g), 0, 0, 0))
    n_out = pl.BlockSpec((1, ML_HEADS, ML_DQK), lambda *g: (idx(*g), 0, 0))
    m_out = pl.BlockSpec((1, 1, LANES), lambda *g: (idx(*g), 0, 0))
    shapes = (jax.ShapeDtypeStruct((bsz, ML_HEADS, ML_DQK, ML_DV), F32),
              jax.ShapeDtypeStruct((bsz, ML_HEADS, ML_DQK), F32),
              jax.ShapeDtypeStruct((bsz, 1, LANES), F32))
    return (c_in, n_in, m_in), (c_out, n_out, m_out), shapes


def _mlstm_chunked(p, c0, n0, m0, layer_j, b_i, b_f, g_head):
    bsz, length, _ = p.shape
    s_in, s_out, s_shapes = _mlstm_state_specs(bsz, layer_j, lambda i, j: i)
    return pl.pallas_call(
        functools.partial(_mlstm_chunk_kernel, c=CHUNK),
        out_shape=(jax.ShapeDtypeStruct((bsz, length, D_MODEL), BF16),) + s_shapes,
        grid=(bsz, length // CHUNK),
        in_specs=[pl.BlockSpec((1, CHUNK, ML_PROJ_PAD), lambda i, j: (i, j, 0)), *s_in,
                  _resident((1, LANES)), _resident((1, LANES)), _resident((1, D_MODEL))],
        out_specs=(pl.BlockSpec((1, CHUNK, D_MODEL), lambda i, j: (i, j, 0)),) + s_out,
        compiler_params=_params(), name="mlstm_chunk",
    )(p, c0, n0, m0, b_i, b_f, g_head)


def _row0(x, rows=SUBLANES):
    r = lax.broadcasted_iota(jnp.int32, (rows, x.shape[1]), 0)
    return jnp.where(r == 0, x, 0.0)


def _mlstm_step_kernel(p_ref, c0_ref, n0_ref, m0_ref, bi_ref, bf_ref, gh_ref,
                       o_ref, c_ref, n_ref, m_ref):
    hq = ML_HEADS * ML_DQK
    i_pre = p_ref[0, :, 2 * hq + 2 * D_MODEL:2 * hq + 2 * D_MODEL + LANES] + bi_ref[...]
    log_f = _log_sigmoid(p_ref[0, :, 2 * hq + 2 * D_MODEL + LANES:ML_PROJ_PAD] + bf_ref[...])
    m_prev = m0_ref[0, 0]
    mt = jnp.maximum(log_f + m_prev, i_pre)
    w_inter = jnp.exp(log_f + m_prev - mt)
    w_k = jnp.exp(i_pre - mt)
    e_inv = jnp.exp(-mt)
    m_ref[0] = mt
    for h in range(ML_HEADS):
        hs = slice(h, h + 1)
        q = p_ref[0, :, h * ML_DQK:(h + 1) * ML_DQK]
        k = p_ref[0, :, hq + h * ML_DQK:hq + (h + 1) * ML_DQK] * ML_DQK ** -0.5
        v = p_ref[0, :, 2 * hq + h * ML_DV:2 * hq + (h + 1) * ML_DV]
        c_h = c0_ref[0, 0, h]
        n_h = n0_ref[0, 0, hs, :]
        wi = w_inter[:, hs]
        s = jnp.sum(q * k, axis=-1, keepdims=True) * w_k[:, hs]
        q_c = _dot(_row0(q).astype(BF16), c_h.astype(BF16))[0:1]
        num = wi * q_c + s * v
        den = wi * jnp.sum(q * n_h, axis=-1, keepdims=True) + s
        hh = num / jnp.maximum(jnp.abs(den), e_inv[:, hs])
        kw = k * w_k[:, hs]
        v8 = jnp.broadcast_to(v, (SUBLANES, ML_DV)).astype(BF16)
        c_ref[0, h] = wi * c_h + _dot_tn(_row0(kw).astype(BF16), v8)
        n_ref[0, hs, :] = wi * n_h + kw
        vs = slice(h * ML_DV, (h + 1) * ML_DV)
        hn = hh * lax.rsqrt(jnp.mean(hh * hh, axis=-1, keepdims=True) + EPS) * gh_ref[:, vs]
        og = p_ref[0, :, 2 * hq + D_MODEL + h * ML_DV:2 * hq + D_MODEL + (h + 1) * ML_DV]
        o_ref[0, :, vs] = (hn * jax.nn.sigmoid(og)).astype(o_ref.dtype)


def _mlstm_step(p, c0, n0, m0, layer_j, b_i, b_f, g_head):
    bsz = p.shape[0]
    s_in, s_out, s_shapes = _mlstm_state_specs(bsz, layer_j, lambda i: i)
    return pl.pallas_call(
        _mlstm_step_kernel,
        out_shape=(jax.ShapeDtypeStruct((bsz, 1, D_MODEL), BF16),) + s_shapes,
        grid=(bsz,),
        in_specs=[pl.BlockSpec((1, 1, ML_PROJ_PAD), lambda i: (i, 0, 0)), *s_in,
                  _resident((1, LANES)), _resident((1, LANES)), _resident((1, D_MODEL))],
        out_specs=(pl.BlockSpec((1, 1, D_MODEL), lambda i: (i, 0, 0)),) + s_out,
        compiler_params=_params(), name="mlstm_step",
    )(p, c0, n0, m0, b_i, b_f, g_head)


def _l2n(x):
    return x * lax.rsqrt(jnp.sum(x * x, axis=-1, keepdims=True) + EPS)


def _unit_lower_inverse(a, c):
    row = lax.broadcasted_iota(jnp.int32, (c, c), 0)
    col = lax.broadcasted_iota(jnp.int32, (c, c), 1)
    n = -a
    t = jnp.where(row == col, 1.0, 0.0) + n
    power = 2
    while power < c:
        n = jnp.dot(n, n, precision=HI, preferred_element_type=F32)
        t = t + jnp.dot(t, n, precision=HI, preferred_element_type=F32)
        power *= 2
    return t


def _gdn_gates(p_ref, alog_ref, dtb_ref):
    beta = jax.nn.sigmoid(p_ref[0, :, GD_QKV + D_MODEL:GD_QKV + D_MODEL + LANES])
    a_pre = p_ref[0, :, GD_QKV + D_MODEL + LANES:GD_PROJ_PAD]
    log_g = -jnp.exp(alog_ref[...]) * _softplus(a_pre + dtb_ref[...])
    return beta, log_g


def _gdn_out(o, z, gout):
    return (o * lax.rsqrt(jnp.mean(o * o, axis=-1, keepdims=True) + EPS) * gout * _silu(z))


def _gdn_chunk_kernel(p_ref, cp_ref, cw_ref, alog_ref, dtb_ref, gout_ref, s0_ref,
                      o_ref, s_ref, nc_ref, e_ref, *, c):
    pad = SUBLANES
    keep = GD_CONV - 1

    @pl.when(pl.program_id(1) == 0)
    def _():
        s_ref[0] = s0_ref[0, 0]
        e_ref[pl.ds(pad - keep, keep), :] = cp_ref[0, 0]

    e_ref[pl.ds(pad, c), :] = p_ref[0, :, 0:GD_QKV]
    conv = e_ref[pl.ds(pad - keep, c), :] * cw_ref[0:1, :]
    for j in range(1, GD_CONV):
        conv = conv + e_ref[pl.ds(pad - keep + j, c), :] * cw_ref[j:j + 1, :]
    conv = _silu(conv)
    nc_ref[0] = e_ref[pl.ds(pad + c - keep, keep), :]
    e_ref[pl.ds(0, pad), :] = e_ref[pl.ds(c, pad), :]

    beta, log_g = _gdn_gates(p_ref, alog_ref, dtb_ref)
    row = lax.broadcasted_iota(jnp.int32, (c, c), 0)
    col = lax.broadcasted_iota(jnp.int32, (c, c), 1)
    incl = row >= col
    strict = row > col
    gam = jnp.dot(incl.astype(F32), log_g, precision=HI, preferred_element_type=F32)
    gam_t = gam.T
    e_gam = jnp.exp(gam)
    g_last = gam[c - 1:c, :]
    e_last = jnp.exp(g_last)
    hk = GD_HEADS * GD_DK
    for h in range(GD_HEADS):
        hs = slice(h, h + 1)
        q = _l2n(conv[:, h * GD_DK:(h + 1) * GD_DK]) * GD_DK ** -0.5
        k = _l2n(conv[:, hk + h * GD_DK:hk + (h + 1) * GD_DK])
        v = conv[:, 2 * hk + h * GD_DV:2 * hk + (h + 1) * GD_DV]
        gcol = gam[:, hs]
        dec = jnp.exp(jnp.where(incl, gcol - gam_t[hs, :], -jnp.inf))
        kb = k.astype(BF16)
        bcol = beta[:, hs]
        a = jnp.where(strict, bcol * _dot_nt(kb, kb) * dec, 0.0)
        t = _unit_lower_inverse(a, c)
        u = jnp.dot(t, bcol * v, precision=HI, preferred_element_type=F32)
        w = jnp.dot(t, (bcol * e_gam[:, hs]) * k, precision=HI, preferred_element_type=F32)
        s_h = s_ref[0, h]
        s_b = s_h.astype(BF16)
        u = u - _dot(w.astype(BF16), s_b)
        ub = u.astype(BF16)
        qk = _dot_nt(q.astype(BF16), kb) * dec
        o = _dot((q * e_gam[:, hs]).astype(BF16), s_b) + _dot(qk.astype(BF16), ub)
        k_dec = k * jnp.exp(g_last[:, hs] - gcol)
        s_ref[0, h] = e_last[:, hs] * s_h + _dot_tn(k_dec.astype(BF16), ub)
        z = p_ref[0, :, GD_QKV + h * GD_DV:GD_QKV + (h + 1) * GD_DV]
        o_ref[0, :, h * GD_DV:(h + 1) * GD_DV] = _gdn_out(o, z, gout_ref[...]).astype(o_ref.dtype)


def _gdn_specs(bsz, layer_j, idx):
    cp_in = pl.BlockSpec((1, 1, GD_CONV - 1, GD_QKV), lambda *g: (layer_j, idx(*g), 0, 0))
    s_in = pl.BlockSpec((1, 1, GD_HEADS, GD_DK, GD_DV), lambda *g: (layer_j, idx(*g), 0, 0, 0))
    s_out = pl.BlockSpec((1, GD_HEADS, GD_DK, GD_DV), lambda *g: (idx(*g), 0, 0, 0))
    nc_out = pl.BlockSpec((1, GD_CONV - 1, GD_QKV), lambda *g: (idx(*g), 0, 0))
    shapes = (jax.ShapeDtypeStruct((bsz, GD_HEADS, GD_DK, GD_DV), F32),
              jax.ShapeDtypeStruct((bsz, GD_CONV - 1, GD_QKV), F32))
    return cp_in, s_in, s_out, nc_out, shapes


def _gdn_chunked(p, conv_prev, s0, layer_j, conv_w, a_log, dt_bias, g_out):
    bsz, length, _ = p.shape
    cp_in, s_in, s_out, nc_out, shapes = _gdn_specs(bsz, layer_j, lambda i, j: i)
    return pl.pallas_call(
        functools.partial(_gdn_chunk_kernel, c=CHUNK),
        out_shape=(jax.ShapeDtypeStruct((bsz, length, D_MODEL), BF16),) + shapes,
        grid=(bsz, length // CHUNK),
        in_specs=[pl.BlockSpec((1, CHUNK, GD_PROJ_PAD), lambda i, j: (i, j, 0)), cp_in,
                  _resident((GD_CONV, GD_QKV)), _resident((1, LANES)), _resident((1, LANES)),
                  _resident((1, GD_DV)), s_in],
        out_specs=(pl.BlockSpec((1, CHUNK, D_MODEL), lambda i, j: (i, j, 0)), s_out, nc_out),
        scratch_shapes=[pltpu.VMEM((CHUNK + SUBLANES, GD_QKV), F32)],
        compiler_params=_params(), name="gdn_chunk",
    )(p, conv_prev, conv_w, a_log, dt_bias, g_out, s0)


def _gdn_step_kernel(p_ref, cp_ref, cw_ref, alog_ref, dtb_ref, gout_ref, s0_ref,
                     o_ref, s_ref, nc_ref):
    qkv = p_ref[0, :, 0:GD_QKV]
    prev = cp_ref[0, 0]
    conv = qkv * cw_ref[GD_CONV - 1:GD_CONV, :]
    for j in range(GD_CONV - 1):
        conv = conv + prev[j:j + 1, :] * cw_ref[j:j + 1, :]
    conv = _silu(conv)
    nc_ref[0] = jnp.concatenate([prev[1:GD_CONV - 1], qkv], axis=0)
    beta, log_g = _gdn_gates(p_ref, alog_ref, dtb_ref)
    g = jnp.exp(log_g)
    hk = GD_HEADS * GD_DK
    rows = lax.broadcasted_iota(jnp.int32, (SUBLANES, GD_DK), 0)
    for h in range(GD_HEADS):
        hs = slice(h, h + 1)
        q = _l2n(conv[:, h * GD_DK:(h + 1) * GD_DK]) * GD_DK ** -0.5
        k = _l2n(conv[:, hk + h * GD_DK:hk + (h + 1) * GD_DK])
        v = conv[:, 2 * hk + h * GD_DV:2 * hk + (h + 1) * GD_DV]
        s_h = s0_ref[0, 0, h]
        kq = jnp.where(rows == 0, k, jnp.where(rows == 1, q, 0.0))
        kq_s = _dot(kq.astype(BF16), s_h.astype(BF16))
        g_h = g[:, hs]
        u = beta[:, hs] * (v - g_h * kq_s[0:1])
        o = g_h * kq_s[1:2] + jnp.sum(q * k, axis=-1, keepdims=True) * u
        u8 = jnp.broadcast_to(u, (SUBLANES, GD_DV)).astype(BF16)
        s_ref[0, h] = g_h * s_h + _dot_tn(_row0(k).astype(BF16), u8)
        z = p_ref[0, :, GD_QKV + h * GD_DV:GD_QKV + (h + 1) * GD_DV]
        o_ref[0, :, h * GD_DV:(h + 1) * GD_DV] = _gdn_out(o, z, gout_ref[...]).astype(o_ref.dtype)


def _gdn_step(p, conv_prev, s0, layer_j, conv_w, a_log, dt_bias, g_out):
    bsz = p.shape[0]
    cp_in, s_in, s_out, nc_out, shapes = _gdn_specs(bsz, layer_j, lambda i: i)
    return pl.pallas_call(
        _gdn_step_kernel,
        out_shape=(jax.ShapeDtypeStruct((bsz, 1, D_MODEL), BF16),) + shapes,
        grid=(bsz,),
        in_specs=[pl.BlockSpec((1, 1, GD_PROJ_PAD), lambda i: (i, 0, 0)), cp_in,
                  _resident((GD_CONV, GD_QKV)), _resident((1, LANES)), _resident((1, LANES)),
                  _resident((1, GD_DV)), s_in],
        out_specs=(pl.BlockSpec((1, 1, D_MODEL), lambda i: (i, 0, 0)), s_out, nc_out),
        compiler_params=_params(), name="gdn_step",
    )(p, conv_prev, conv_w, a_log, dt_bias, g_out, s0)


def _lane_pad(x):
    return jnp.pad(x, [(0, 0)] * (x.ndim - 1) + [(0, LANES - x.shape[-1])])


def _prep_weights(P):
    W = {}
    for name in ('ffn1_w_gate', 'ffn1_w_up', 'ffn1_w_down', 'ffn2_w_gate', 'ffn2_w_up', 'ffn2_w_down',
                 'ml_w_out', 'gd_w_out', 'xa_w_q', 'xa_w_k', 'xa_w_v', 'xa_w_o'):
        W[name] = P[name].astype(BF16)
    ml = P['ml_w_in']
    main = 2 * ML_HEADS * ML_DQK + 2 * D_MODEL
    W['ml_w_in'] = jnp.concatenate(
        [ml[..., :main], _lane_pad(ml[..., main:main + ML_HEADS]), _lane_pad(ml[..., main + ML_HEADS:])],
        axis=-1).astype(BF16)
    gd = P['gd_w_in']
    main = GD_QKV + D_MODEL
    W['gd_w_in'] = jnp.concatenate(
        [gd[..., :main], _lane_pad(gd[..., main:main + GD_HEADS]), _lane_pad(gd[..., main + GD_HEADS:])],
        axis=-1).astype(BF16)
    return W


def _trunk(x, mem_k, mem_v, ml_c, ml_n, ml_m, gd_s, gd_conv, P, W):
    bsz, length, _ = x.shape
    single = length == 1
    x = x.reshape(bsz * length, D_MODEL)
    ml_m = _lane_pad(ml_m)[:, :, None, :]
    new_c, new_n, new_m, new_s, new_conv = [], [], [], [], []
    y = None
    for layer in range(DEPTH):
        j = layer // 2
        x = _ffn(x, P['g_ffn1'][layer], W['ffn1_w_gate'][layer], W['ffn1_w_up'][layer], W['ffn1_w_down'][layer])
        if layer % 2 == 0:
            p = _proj(x, W['ml_w_in'][j], g=P['g_mix'][layer]).reshape(bsz, length, ML_PROJ_PAD)
            fn = _mlstm_step if single else _mlstm_chunked
            a, c, n, m = fn(p, ml_c, ml_n, ml_m, j, _lane_pad(P['ml_b_i'][j][None]), _lane_pad(P['ml_b_f'][j][None]),
                            P['ml_g_head'][j][None])
            new_c.append(c)
            new_n.append(n)
            new_m.append(m[:, 0, :ML_HEADS])
            w_out = W['ml_w_out'][j]
        else:
            p = _proj(x, W['gd_w_in'][j], g=P['g_mix'][layer]).reshape(bsz, length, GD_PROJ_PAD)
            fn = _gdn_step if single else _gdn_chunked
            a, s, cv = fn(p, gd_conv, gd_s, j, P['gd_conv_w'][j], _lane_pad(P['gd_a_log'][j][None]),
                          _lane_pad(P['gd_dt_bias'][j][None]), P['gd_g_out'][j][None])
            new_s.append(s)
            new_conv.append(cv)
            w_out = W['gd_w_out'][j]
        x = _proj(a.reshape(bsz * length, D_MODEL), w_out, res=x)
        q = _proj(x, W['xa_w_q'][layer], g=P['g_xattn'][layer]).reshape(bsz, length, D_MODEL)
        if single:
            q = jnp.pad(q, ((0, 0), (0, SUBLANES - 1), (0, 0)))
        o = _attn(q, mem_k, mem_v, layer)
        if single:
            o = o[:, :1]
        x = _proj(o.reshape(bsz * length, D_MODEL), W['xa_w_o'][layer], res=x)
        if layer == DEPTH - 1:
            x, y = _ffn(x, P['g_ffn2'][layer], W['ffn2_w_gate'][layer], W['ffn2_w_up'][layer],
                        W['ffn2_w_down'][layer], g_final=P['g_final'])
        else:
            x = _ffn(x, P['g_ffn2'][layer], W['ffn2_w_gate'][layer], W['ffn2_w_up'][layer], W['ffn2_w_down'][layer])
    return (y.reshape(bsz, length, D_MODEL), jnp.stack(new_c), jnp.stack(new_n), jnp.stack(new_m),
            jnp.stack(new_s), jnp.stack(new_conv))


def kernel(x_prompt, x_sample, mem_prompt, cache_mem_k, cache_mem_v, state_mlstm_C, state_mlstm_n, state_mlstm_m, state_gdn_S, state_gdn_conv, g_ffn1, ffn1_w_gate, ffn1_w_up, ffn1_w_down, g_mix, ml_w_in, ml_b_i, ml_b_f, ml_g_head, ml_w_out, gd_w_in, gd_conv_w, gd_a_log, gd_dt_bias, gd_g_out, gd_w_out, g_xattn, g_mem, xa_w_q, xa_w_k, xa_w_v, xa_w_o, g_ffn2, ffn2_w_gate, ffn2_w_up, ffn2_w_down, g_final):
    P = dict(g_ffn1=g_ffn1, ffn1_w_gate=ffn1_w_gate, ffn1_w_up=ffn1_w_up, ffn1_w_down=ffn1_w_down, g_mix=g_mix,
             ml_w_in=ml_w_in, ml_b_i=ml_b_i, ml_b_f=ml_b_f, ml_g_head=ml_g_head, ml_w_out=ml_w_out,
             gd_w_in=gd_w_in, gd_conv_w=gd_conv_w, gd_a_log=gd_a_log, gd_dt_bias=gd_dt_bias, gd_g_out=gd_g_out,
             gd_w_out=gd_w_out, g_xattn=g_xattn, xa_w_q=xa_w_q, xa_w_k=xa_w_k, xa_w_v=xa_w_v, xa_w_o=xa_w_o,
             g_ffn2=g_ffn2, ffn2_w_gate=ffn2_w_gate, ffn2_w_up=ffn2_w_up, ffn2_w_down=ffn2_w_down, g_final=g_final)
    W = _prep_weights(P)
    batch, n_mem, _ = mem_prompt.shape
    dec_batch = x_sample.shape[0]
    n_ml, n_gd = state_mlstm_C.shape[0], state_gdn_S.shape[0]

    pk, pv = _memkv(mem_prompt.reshape(batch * n_mem, D_MODEL), g_mem, W['xa_w_k'], W['xa_w_v'])
    pk = pk.reshape(DEPTH, batch, n_mem, D_MODEL)
    pv = pv.reshape(DEPTH, batch, n_mem, D_MODEL)
    z_c = jnp.zeros((n_ml, batch, ML_HEADS, ML_DQK, ML_DV), F32)
    z_n = jnp.zeros((n_ml, batch, ML_HEADS, ML_DQK), F32)
    z_m = jnp.zeros((n_ml, batch, ML_HEADS), F32)
    z_s = jnp.zeros((n_gd, batch, GD_HEADS, GD_DK, GD_DV), F32)
    z_conv = jnp.zeros((n_gd, batch, GD_CONV - 1, GD_QKV), F32)
    y_p, p_c, p_n, p_m, p_s, p_conv = _trunk(x_prompt, pk, pv, z_c, z_n, z_m, z_s, z_conv, P, W)

    y_s, s_c, s_n, s_m, s_s, s_conv = _trunk(
        x_sample, cache_mem_k.reshape(DEPTH, dec_batch, N_MEM, D_MODEL),
        cache_mem_v.reshape(DEPTH, dec_batch, N_MEM, D_MODEL),
        state_mlstm_C, state_mlstm_n, state_mlstm_m, state_gdn_S, state_gdn_conv, P, W)

    kv_shape = (DEPTH, batch, n_mem, XA_HEADS, XA_DH)
    return (y_p, y_s, pk.reshape(kv_shape), pv.reshape(kv_shape), p_c, p_n, p_m, p_s, p_conv,
            s_c, s_n, s_m, s_s, s_conv)
```

```python
import functools

import jax
import jax.numpy as jnp
from jax import lax
from jax.experimental import pallas as pl
from jax.experimental.pallas import tpu as pltpu

F32 = jnp.float32
BF16 = jnp.bfloat16

D_MODEL = 1024
DEPTH = 4
N_MEM = 256
D_FF = 2816
FFN_RES = 0.5
EPS = 1e-6
ML_HEADS = 4
ML_DV = 256
ML_DQK = 128
GD_HEADS = 8
GD_DK = 128
GD_DV = 128
GD_CONV = 4
GD_QKV = 3072
XA_HEADS = 4
XA_DH = 256
CHUNK = 64
LANES = 128
SUBLANES = 8
ML_PROJ_PAD = 2 * ML_HEADS * ML_DQK + 2 * D_MODEL + 2 * LANES
GD_PROJ_PAD = GD_QKV + D_MODEL + 2 * LANES
VMEM_LIMIT_BYTES = 56 * 1024 * 1024
HI = lax.Precision.HIGHEST


def _params():
    return pltpu.CompilerParams(vmem_limit_bytes=VMEM_LIMIT_BYTES)


def _rms(x, g):
    return x * lax.rsqrt(jnp.mean(x * x, axis=-1, keepdims=True) + EPS) * g


def _silu(x):
    return x * jax.nn.sigmoid(x)


def _softplus(x):
    return jnp.maximum(x, 0.0) + jnp.log1p(jnp.exp(-jnp.abs(x)))


def _dot(a, b):
    return jnp.dot(a, b, preferred_element_type=F32)


def _dot_nt(a, b):
    return lax.dot_general(a, b, (((1,), (1,)), ((), ())), preferred_element_type=F32)


def _dot_tn(a, b):
    return lax.dot_general(a, b, (((0,), (0,)), ((), ())), preferred_element_type=F32)


def _resident(shape):
    nd = len(shape)
    return pl.BlockSpec(shape, lambda *_: (0,) * nd, pipeline_mode=pl.Buffered(1))


def _ffn_kernel(x_ref, g_ref, wg_ref, wu_ref, wd_ref, *rest, fchunk, final):
    if final:
        gf_ref, o_ref, y_ref, h_ref, a_ref = rest
    else:
        o_ref, h_ref, a_ref = rest
    h_ref[...] = _rms(x_ref[...], g_ref[...]).astype(BF16)
    for j in range(D_FF // fchunk):
        sl = slice(j * fchunk, (j + 1) * fchunk)
        h = h_ref[...]
        gate = _dot(h, wg_ref[:, sl])
        up = _dot(h, wu_ref[:, sl])
        a_ref[:, sl] = (_silu(gate) * up).astype(BF16)
    out = x_ref[...] + FFN_RES * _dot(a_ref[...], wd_ref[...])
    o_ref[...] = out
    if final:
        y_ref[...] = _rms(out, gf_ref[...])


def _ffn(x, g, wg, wu, wd, g_final=None):
    m = x.shape[0]
    tm = min(m, 512)
    final = g_final is not None
    row = pl.BlockSpec((tm, D_MODEL), lambda i: (i, 0))
    in_specs = [row, _resident((1, D_MODEL)), _resident((D_MODEL, D_FF)), _resident((D_MODEL, D_FF)),
                _resident((D_FF, D_MODEL))]
    args = [x, g.reshape(1, D_MODEL), wg, wu, wd]
    out_shape = jax.ShapeDtypeStruct((m, D_MODEL), F32)
    out_specs = row
    if final:
        in_specs.append(_resident((1, D_MODEL)))
        args.append(g_final.reshape(1, D_MODEL))
        out_shape = (out_shape, out_shape)
        out_specs = (row, row)
    return pl.pallas_call(
        functools.partial(_ffn_kernel, fchunk=256, final=final),
        out_shape=out_shape, grid=(m // tm,), in_specs=in_specs, out_specs=out_specs,
        scratch_shapes=[pltpu.VMEM((tm, D_MODEL), BF16), pltpu.VMEM((tm, D_FF), BF16)],
        compiler_params=_params(), name="ffn_final" if final else "ffn",
    )(*args)


def _proj_kernel(*refs, norm, res, nchunk):
    refs = list(refs)
    x_ref = refs.pop(0)
    g_ref = refs.pop(0) if norm else None
    w_ref = refs.pop(0)
    r_ref = refs.pop(0) if res else None
    o_ref = refs.pop(0)
    if norm:
        h = _rms(x_ref[...], g_ref[...]).astype(BF16)
    else:
        h = x_ref[...].astype(BF16)
    n = w_ref.shape[1]
    for n0 in range(0, n, nchunk):
        n1 = min(n, n0 + nchunk)
        y = _dot(h, w_ref[:, n0:n1])
        if res:
            y = y + r_ref[:, n0:n1]
        o_ref[:, n0:n1] = y.astype(o_ref.dtype)


def _proj(x, w, g=None, res=None, out_dtype=F32):
    m, k = x.shape
    n = w.shape[1]
    tm = min(m, 512)
    in_specs = [pl.BlockSpec((tm, k), lambda i: (i, 0))]
    args = [x]
    if g is not None:
        in_specs.append(_resident((1, k)))
        args.append(g.reshape(1, k))
    in_specs.append(_resident((k, n)))
    args.append(w)
    if res is not None:
        in_specs.append(pl.BlockSpec((tm, n), lambda i: (i, 0)))
        args.append(res)
    return pl.pallas_call(
        functools.partial(_proj_kernel, norm=g is not None, res=res is not None, nchunk=512),
        out_shape=jax.ShapeDtypeStruct((m, n), out_dtype), grid=(m // tm,), in_specs=in_specs,
        out_specs=pl.BlockSpec((tm, n), lambda i: (i, 0)),
        compiler_params=_params(), name="proj",
    )(*args)


def _memkv_kernel(x_ref, g_ref, wk_ref, wv_ref, k_ref, v_ref):
    h = _rms(x_ref[...], g_ref[0]).astype(BF16)
    k_ref[0] = _dot(h, wk_ref[0])
    v_ref[0] = _dot(h, wv_ref[0])


def _memkv(mem, g_mem, wk, wv):
    m = mem.shape[0]
    tm = 512
    w_spec = pl.BlockSpec((1, D_MODEL, D_MODEL), lambda l, i: (l, 0, 0))
    o_spec = pl.BlockSpec((1, tm, D_MODEL), lambda l, i: (l, i, 0))
    o_shape = jax.ShapeDtypeStruct((DEPTH, m, D_MODEL), F32)
    return pl.pallas_call(
        _memkv_kernel, out_shape=(o_shape, o_shape), grid=(DEPTH, m // tm),
        in_specs=[pl.BlockSpec((tm, D_MODEL), lambda l, i: (i, 0)),
                  pl.BlockSpec((1, 1, D_MODEL), lambda l, i: (l, 0, 0)), w_spec, w_spec],
        out_specs=(o_spec, o_spec), compiler_params=_params(), name="memkv",
    )(mem, g_mem.reshape(DEPTH, 1, D_MODEL), wk, wv)


def _attn_kernel(q_ref, k_ref, v_ref, o_ref):
    for h in range(XA_HEADS):
        sl = slice(h * XA_DH, (h + 1) * XA_DH)
        q = q_ref[0, :, sl].astype(BF16)
        k = k_ref[0, 0, :, sl].astype(BF16)
        v = v_ref[0, 0, :, sl].astype(BF16)
        s = _dot_nt(q, k) * XA_DH ** -0.5
        e = jnp.exp(s - jnp.max(s, axis=-1, keepdims=True))
        p = e / jnp.sum(e, axis=-1, keepdims=True)
        o_ref[0, :, sl] = _dot(p.astype(BF16), v).astype(o_ref.dtype)


def _attn(q, mem_k, mem_v, layer):
    b, l, _ = q.shape
    tq = min(l, 512)
    kv_spec = pl.BlockSpec((1, 1, N_MEM, D_MODEL), lambda i, j: (layer, i, 0, 0))
    q_spec = pl.BlockSpec((1, tq, D_MODEL), lambda i, j: (i, j, 0))
    return pl.pallas_call(
        _attn_kernel, out_shape=jax.ShapeDtypeStruct((b, l, D_MODEL), BF16), grid=(b, l // tq),
        in_specs=[q_spec, kv_spec, kv_spec], out_specs=q_spec,
        compiler_params=_params(), name="attn",
    )(q, mem_k, mem_v)


ATTN_STEP_BATCH = 8


def _attn_step_kernel(q_ref, k_ref, v_ref, o_ref):
    for i in range(ATTN_STEP_BATCH):
        q = q_ref[0, i] * XA_DH ** -0.5
        s = jnp.sum(k_ref[0, i] * q[None], axis=-1, keepdims=True)
        e = jnp.exp(s - jnp.max(s, axis=0, keepdims=True))
        acc = jnp.sum(e * v_ref[0, i], axis=0)
        o_ref[0, i] = acc / jnp.sum(e, axis=0)


def _attn_step(q, mem_k, mem_v, layer):
    b = q.shape[0]
    bb = ATTN_STEP_BATCH
    kv_spec = pl.BlockSpec((1, bb, N_MEM, XA_HEADS, XA_DH), lambda i: (layer, i, 0, 0, 0))
    q_spec = pl.BlockSpec((1, bb, XA_HEADS, XA_DH), lambda i: (i, 0, 0, 0))
    out = pl.pallas_call(
        _attn_step_kernel, out_shape=jax.ShapeDtypeStruct((b // bb, bb, XA_HEADS, XA_DH), F32), grid=(b // bb,),
        in_specs=[q_spec, kv_spec, kv_spec], out_specs=q_spec,
        compiler_params=_params(), name="attn_step",
    )(q.reshape(b // bb, bb, XA_HEADS, XA_DH), mem_k, mem_v)
    return out.reshape(b, D_MODEL)


def _log_sigmoid(x):
    return jnp.minimum(x, 0.0) - jnp.log1p(jnp.exp(-jnp.abs(x)))


def _lane_select(h, value, into):
    lane = lax.broadcasted_iota(jnp.int32, into.shape, 1)
    return jnp.where(lane == h, value, into)


def _mlstm_chunk_kernel(p_ref, c0_ref, n0_ref, m0_ref, bi_ref, bf_ref, gh_ref,
                        o_ref, c_ref, n_ref, m_ref, *, c):
    @pl.when(pl.program_id(1) == 0)
    def _():
        c_ref[0] = c0_ref[0, 0]
        n_ref[0] = n0_ref[0, 0]
        m_ref[0] = m0_ref[0, 0]

    hq = ML_HEADS * ML_DQK
    i_pre = p_ref[0, :, 2 * hq + 2 * D_MODEL:2 * hq + 2 * D_MODEL + LANES] + bi_ref[...]
    log_f = _log_sigmoid(p_ref[0, :, 2 * hq + 2 * D_MODEL + LANES:ML_PROJ_PAD] + bf_ref[...])
    row = lax.broadcasted_iota(jnp.int32, (c, c), 0)
    col = lax.broadcasted_iota(jnp.int32, (c, c), 1)
    incl = row >= col
    b = jnp.dot(incl.astype(F32), log_f, precision=HI, preferred_element_type=F32)
    m_prev = m_ref[0]
    a_inter = b + m_prev
    xt = (i_pre - b).T
    b_last = b[c - 1:c, :]
    m_row = m_prev
    for h in range(ML_HEADS):
        hs = slice(h, h + 1)
        bcol = b[:, hs]
        dm = jnp.where(incl, bcol + xt[hs, :], -jnp.inf)
        mt = jnp.maximum(a_inter[:, hs], jnp.max(dm, axis=-1, keepdims=True))
        w_inter = jnp.exp(a_inter[:, hs] - mt)
        q = p_ref[0, :, h * ML_DQK:(h + 1) * ML_DQK]
        k = p_ref[0, :, hq + h * ML_DQK:hq + (h + 1) * ML_DQK] * ML_DQK ** -0.5
        vb = p_ref[0, :, 2 * hq + h * ML_DV:2 * hq + (h + 1) * ML_DV].astype(BF16)
        qb = q.astype(BF16)
        s = _dot_nt(qb, k.astype(BF16)) * jnp.exp(dm - mt)
        c_h = c_ref[0, h]
        n_h = n_ref[0, hs, :]
        num = w_inter * _dot(qb, c_h.astype(BF16)) + _dot(s.astype(BF16), vb)
        den = w_inter * jnp.sum(q * n_h, axis=-1, keepdims=True) + jnp.sum(s, axis=-1, keepdims=True)
        hh = num / jnp.maximum(jnp.abs(den), jnp.exp(-mt))
        m_new = mt[c - 1:c, :]
        w_k = jnp.exp(b_last[:, hs] - bcol + i_pre[:, hs] - m_new)
        decay = jnp.exp(b_last[:, hs] + m_prev[:, hs] - m_new)
        kw = k * w_k
        c_ref[0, h] = decay * c_h + _dot_tn(kw.astype(BF16), vb)
        n_ref[0, hs, :] = decay * n_h + jnp.sum(kw, axis=0, keepdims=True)
        m_row = _lane_select(h, m_new, m_row)
        vs = slice(h * ML_DV, (h + 1) * ML_DV)
        hn = hh * lax.rsqrt(jnp.mean(hh * hh, axis=-1, keepdims=True) + EPS) * gh_ref[:, vs]
        og = p_ref[0, :, 2 * hq + D_MODEL + h * ML_DV:2 * hq + D_MODEL + (h + 1) * ML_DV]
        o_ref[0, :, vs] = (hn * jax.nn.sigmoid(og)).astype(o_ref.dtype)
    m_ref[0] = m_row


def _mlstm_state_specs(bsz, layer_j, idx, bb=1):
    c_in = pl.BlockSpec((1, bb, ML_HEADS, ML_DQK, ML_DV), lambda *g: (layer_j, idx(*g), 0, 0, 0))
    n_in = pl.BlockSpec((1, bb, ML_HEADS, ML_DQK), lambda *g: (layer_j, idx(*g), 0, 0))
    m_in = pl.BlockSpec((1, bb, 1, LANES), lambda *g: (layer_j, idx(*g), 0, 0))
    c_out = pl.BlockSpec((bb, ML_HEADS, ML_DQK, ML_DV), lambda *g: (idx(*g), 0, 0, 0))
    n_out = pl.BlockSpec((bb, ML_HEADS, ML_DQK), lambda *g: (idx(*g), 0, 0))
    m_out = pl.BlockSpec((bb, 1, LANES), lambda *g: (idx(*g), 0, 0))
    shapes = (jax.ShapeDtypeStruct((bsz, ML_HEADS, ML_DQK, ML_DV), F32),
              jax.ShapeDtypeStruct((bsz, ML_HEADS, ML_DQK), F32),
              jax.ShapeDtypeStruct((bsz, 1, LANES), F32))
    return (c_in, n_in, m_in), (c_out, n_out, m_out), shapes


def _mlstm_chunked(p, c0, n0, m0, layer_j, b_i, b_f, g_head):
    bsz, length, _ = p.shape
    s_in, s_out, s_shapes = _mlstm_state_specs(bsz, layer_j, lambda i, j: i)
    return pl.pallas_call(
        functools.partial(_mlstm_chunk_kernel, c=CHUNK),
        out_shape=(jax.ShapeDtypeStruct((bsz, length, D_MODEL), BF16),) + s_shapes,
        grid=(bsz, length // CHUNK),
        in_specs=[pl.BlockSpec((1, CHUNK, ML_PROJ_PAD), lambda i, j: (i, j, 0)), *s_in,
                  _resident((1, LANES)), _resident((1, LANES)), _resident((1, D_MODEL))],
        out_specs=(pl.BlockSpec((1, CHUNK, D_MODEL), lambda i, j: (i, j, 0)),) + s_out,
        compiler_params=_params(), name="mlstm_chunk",
    )(p, c0, n0, m0, b_i, b_f, g_head)


def _row0(x, rows=SUBLANES):
    r = lax.broadcasted_iota(jnp.int32, (rows, x.shape[1]), 0)
    return jnp.where(r == 0, x, 0.0)


STEP_BATCH = 4


def _mlstm_step_kernel(p_ref, c0_ref, n0_ref, m0_ref, bi_ref, bf_ref, gh_ref,
                       o_ref, c_ref, n_ref, m_ref):
    hq = ML_HEADS * ML_DQK
    for i in range(STEP_BATCH):
        i_pre = p_ref[i, :, 2 * hq + 2 * D_MODEL:2 * hq + 2 * D_MODEL + LANES] + bi_ref[...]
        log_f = _log_sigmoid(p_ref[i, :, 2 * hq + 2 * D_MODEL + LANES:ML_PROJ_PAD] + bf_ref[...])
        m_prev = m0_ref[0, i]
        mt = jnp.maximum(log_f + m_prev, i_pre)
        w_inter = jnp.exp(log_f + m_prev - mt)
        w_k = jnp.exp(i_pre - mt)
        e_inv = jnp.exp(-mt)
        m_ref[i] = mt
        for h in range(ML_HEADS):
            hs = slice(h, h + 1)
            q = p_ref[i, :, h * ML_DQK:(h + 1) * ML_DQK]
            k = p_ref[i, :, hq + h * ML_DQK:hq + (h + 1) * ML_DQK] * ML_DQK ** -0.5
            v = p_ref[i, :, 2 * hq + h * ML_DV:2 * hq + (h + 1) * ML_DV]
            c_h = c0_ref[0, i, h]
            n_h = n0_ref[0, i, hs, :]
            wi = w_inter[:, hs]
            s = jnp.sum(q * k, axis=-1, keepdims=True) * w_k[:, hs]
            q_c = _dot(_row0(q).astype(BF16), c_h.astype(BF16))[0:1]
            num = wi * q_c + s * v
            den = wi * jnp.sum(q * n_h, axis=-1, keepdims=True) + s
            hh = num / jnp.maximum(jnp.abs(den), e_inv[:, hs])
            kw = k * w_k[:, hs]
            v8 = jnp.broadcast_to(v, (SUBLANES, ML_DV)).astype(BF16)
            c_ref[i, h] = wi * c_h + _dot_tn(_row0(kw).astype(BF16), v8)
            n_ref[i, hs, :] = wi * n_h + kw
            vs = slice(h * ML_DV, (h + 1) * ML_DV)
            hn = hh * lax.rsqrt(jnp.mean(hh * hh, axis=-1, keepdims=True) + EPS) * gh_ref[:, vs]
            og = p_ref[i, :, 2 * hq + D_MODEL + h * ML_DV:2 * hq + D_MODEL + (h + 1) * ML_DV]
            o_ref[i, :, vs] = (hn * jax.nn.sigmoid(og)).astype(o_ref.dtype)


def _mlstm_step(p, c0, n0, m0, layer_j, b_i, b_f, g_head):
    bsz = p.shape[0]
    bb = STEP_BATCH
    s_in, s_out, s_shapes = _mlstm_state_specs(bsz, layer_j, lambda i: i, bb)
    return pl.pallas_call(
        _mlstm_step_kernel,
        out_shape=(jax.ShapeDtypeStruct((bsz, 1, D_MODEL), BF16),) + s_shapes,
        grid=(bsz // bb,),
        in_specs=[pl.BlockSpec((bb, 1, ML_PROJ_PAD), lambda i: (i, 0, 0)), *s_in,
                  _resident((1, LANES)), _resident((1, LANES)), _resident((1, D_MODEL))],
        out_specs=(pl.BlockSpec((bb, 1, D_MODEL), lambda i: (i, 0, 0)),) + s_out,
        compiler_params=_params(), name="mlstm_step",
    )(p, c0, n0, m0, b_i, b_f, g_head)


def _l2n(x):
    return x * lax.rsqrt(jnp.sum(x * x, axis=-1, keepdims=True) + EPS)


INV_BASE = 16


def _unit_lower_inverses(a_list, c):
    row = lax.broadcasted_iota(jnp.int32, (c, c), 0)
    col = lax.broadcasted_iota(jnp.int32, (c, c), 1)
    eye = jnp.where(row == col, 1.0, 0.0)
    same = lambda size: (row >> (size.bit_length() - 1)) == (col >> (size.bit_length() - 1))
    ns = [jnp.where(same(INV_BASE), -a, 0.0) for a in a_list]
    ts = [eye + n for n in ns]
    nbs = [n.astype(BF16) for n in ns]
    power = 2
    while power < INV_BASE:
        ns = [_dot(nb, nb) for nb in nbs]
        nbs = [n.astype(BF16) for n in ns]
        ts = [t + _dot(t.astype(BF16), nb) for t, nb in zip(ts, nbs)]
        power *= 2
    size = INV_BASE
    while size < c:
        off = same(2 * size) & jnp.logical_not(same(size))
        tbs = [t.astype(BF16) for t in ts]
        mids = [_dot(tb, jnp.where(off, a, 0.0).astype(BF16)).astype(BF16) for tb, a in zip(tbs, a_list)]
        ts = [t - _dot(mid, tb) for t, mid, tb in zip(ts, mids, tbs)]
        size *= 2
    return ts


def _gdn_gates(p_ref, i, alog_ref, dtb_ref):
    beta = jax.nn.sigmoid(p_ref[i, :, GD_QKV + D_MODEL:GD_QKV + D_MODEL + LANES])
    a_pre = p_ref[i, :, GD_QKV + D_MODEL + LANES:GD_PROJ_PAD]
    log_g = -jnp.exp(alog_ref[...]) * _softplus(a_pre + dtb_ref[...])
    return beta, log_g


def _gdn_out(o, z, gout):
    return (o * lax.rsqrt(jnp.mean(o * o, axis=-1, keepdims=True) + EPS) * gout * _silu(z))


def _gdn_chunk_kernel(p_ref, cp_ref, cw_ref, alog_ref, dtb_ref, gout_ref, s0_ref,
                      o_ref, s_ref, nc_ref, e_ref, *, c):
    pad = SUBLANES
    keep = GD_CONV - 1

    @pl.when(pl.program_id(1) == 0)
    def _():
        s_ref[0] = s0_ref[0, 0]
        e_ref[pl.ds(pad - keep, keep), :] = cp_ref[0, 0]

    e_ref[pl.ds(pad, c), :] = p_ref[0, :, 0:GD_QKV]
    conv = e_ref[pl.ds(pad - keep, c), :] * cw_ref[0:1, :]
    for j in range(1, GD_CONV):
        conv = conv + e_ref[pl.ds(pad - keep + j, c), :] * cw_ref[j:j + 1, :]
    conv = _silu(conv)
    nc_ref[0] = e_ref[pl.ds(pad + c - keep, keep), :]
    e_ref[pl.ds(0, pad), :] = e_ref[pl.ds(c, pad), :]

    beta, log_g = _gdn_gates(p_ref, 0, alog_ref, dtb_ref)
    row = lax.broadcasted_iota(jnp.int32, (c, c), 0)
    col = lax.broadcasted_iota(jnp.int32, (c, c), 1)
    incl = row >= col
    strict = row > col
    gam = jnp.dot(incl.astype(F32), log_g, precision=HI, preferred_element_type=F32)
    gam_t = gam.T
    e_gam = jnp.exp(gam)
    g_last = gam[c - 1:c, :]
    e_last = jnp.exp(g_last)
    hk = GD_HEADS * GD_DK
    heads = range(GD_HEADS)
    hs = [slice(h, h + 1) for h in heads]
    qs = [_l2n(conv[:, h * GD_DK:(h + 1) * GD_DK]) * GD_DK ** -0.5 for h in heads]
    ks = [_l2n(conv[:, hk + h * GD_DK:hk + (h + 1) * GD_DK]) for h in heads]
    vs = [conv[:, 2 * hk + h * GD_DV:2 * hk + (h + 1) * GD_DV] for h in heads]
    kbs = [k.astype(BF16) for k in ks]
    decs = [jnp.exp(jnp.where(incl, gam[:, hs[h]] - gam_t[hs[h], :], -jnp.inf)) for h in heads]
    kqs = [_dot_nt(jnp.concatenate([kbs[h], qs[h].astype(BF16)], axis=0), kbs[h]) for h in heads]
    a_list = [jnp.where(strict, beta[:, hs[h]] * kqs[h][0:c] * decs[h], 0.0) for h in heads]
    ts = _unit_lower_inverses(a_list, c)
    rhs = [jnp.concatenate([beta[:, hs[h]] * vs[h], (beta[:, hs[h]] * e_gam[:, hs[h]]) * ks[h]], axis=1).astype(BF16)
           for h in heads]
    uws = [_dot(ts[h].astype(BF16), rhs[h]) for h in heads]
    s_old = [s_ref[0, h] for h in heads]
    wq = [jnp.concatenate([uws[h][:, GD_DV:], qs[h] * e_gam[:, hs[h]]], axis=0).astype(BF16) for h in heads]
    wqs = [_dot(wq[h], s_old[h].astype(BF16)) for h in heads]
    ubs = [(uws[h][:, :GD_DV] - wqs[h][0:c]).astype(BF16) for h in heads]
    for h in heads:
        k_dec = ks[h] * jnp.exp(g_last[:, hs[h]] - gam[:, hs[h]])
        s_ref[0, h] = e_last[:, hs[h]] * s_old[h] + _dot_tn(k_dec.astype(BF16), ubs[h])
    for h in heads:
        o = wqs[h][c:2 * c] + _dot((kqs[h][c:2 * c] * decs[h]).astype(BF16), ubs[h])
        z = p_ref[0, :, GD_QKV + h * GD_DV:GD_QKV + (h + 1) * GD_DV]
        o_ref[0, :, h * GD_DV:(h + 1) * GD_DV] = _gdn_out(o, z, gout_ref[...]).astype(o_ref.dtype)


def _gdn_specs(bsz, layer_j, idx, bb=1):
    cp_in = pl.BlockSpec((1, bb, GD_CONV - 1, GD_QKV), lambda *g: (layer_j, idx(*g), 0, 0))
    s_in = pl.BlockSpec((1, bb, GD_HEADS, GD_DK, GD_DV), lambda *g: (layer_j, idx(*g), 0, 0, 0))
    s_out = pl.BlockSpec((bb, GD_HEADS, GD_DK, GD_DV), lambda *g: (idx(*g), 0, 0, 0))
    nc_out = pl.BlockSpec((bb, GD_CONV - 1, GD_QKV), lambda *g: (idx(*g), 0, 0))
    shapes = (jax.ShapeDtypeStruct((bsz, GD_HEADS, GD_DK, GD_DV), F32),
              jax.ShapeDtypeStruct((bsz, GD_CONV - 1, GD_QKV), F32))
    return cp_in, s_in, s_out, nc_out, shapes


def _gdn_chunked(p, conv_prev, s0, layer_j, conv_w, a_log, dt_bias, g_out):
    bsz, length, _ = p.shape
    cp_in, s_in, s_out, nc_out, shapes = _gdn_specs(bsz, layer_j, lambda i, j: i)
    return pl.pallas_call(
        functools.partial(_gdn_chunk_kernel, c=CHUNK),
        out_shape=(jax.ShapeDtypeStruct((bsz, length, D_MODEL), BF16),) + shapes,
        grid=(bsz, length // CHUNK),
        in_specs=[pl.BlockSpec((1, CHUNK, GD_PROJ_PAD), lambda i, j: (i, j, 0)), cp_in,
                  _resident((GD_CONV, GD_QKV)), _resident((1, LANES)), _resident((1, LANES)),
                  _resident((1, GD_DV)), s_in],
        out_specs=(pl.BlockSpec((1, CHUNK, D_MODEL), lambda i, j: (i, j, 0)), s_out, nc_out),
        scratch_shapes=[pltpu.VMEM((CHUNK + SUBLANES, GD_QKV), F32)],
        compiler_params=_params(), name="gdn_chunk",
    )(p, conv_prev, conv_w, a_log, dt_bias, g_out, s0)


def _gdn_step_kernel(p_ref, cp_ref, cw_ref, alog_ref, dtb_ref, gout_ref, s0_ref,
                     o_ref, s_ref, nc_ref):
    hk = GD_HEADS * GD_DK
    rows = lax.broadcasted_iota(jnp.int32, (SUBLANES, GD_DK), 0)
    for i in range(STEP_BATCH):
        qkv = p_ref[i, :, 0:GD_QKV]
        prev = cp_ref[0, i]
        conv = qkv * cw_ref[GD_CONV - 1:GD_CONV, :]
        for j in range(GD_CONV - 1):
            conv = conv + prev[j:j + 1, :] * cw_ref[j:j + 1, :]
        conv = _silu(conv)
        nc_ref[i] = jnp.concatenate([prev[1:GD_CONV - 1], qkv], axis=0)
        beta, log_g = _gdn_gates(p_ref, i, alog_ref, dtb_ref)
        g = jnp.exp(log_g)
        for h in range(GD_HEADS):
            hs = slice(h, h + 1)
            q = _l2n(conv[:, h * GD_DK:(h + 1) * GD_DK]) * GD_DK ** -0.5
            k = _l2n(conv[:, hk + h * GD_DK:hk + (h + 1) * GD_DK])
            v = conv[:, 2 * hk + h * GD_DV:2 * hk + (h + 1) * GD_DV]
            s_h = s0_ref[0, i, h]
            kq = jnp.where(rows == 0, k, jnp.where(rows == 1, q, 0.0))
            kq_s = _dot(kq.astype(BF16), s_h.astype(BF16))
            g_h = g[:, hs]
            u = beta[:, hs] * (v - g_h * kq_s[0:1])
            o = g_h * kq_s[1:2] + jnp.sum(q * k, axis=-1, keepdims=True) * u
            u8 = jnp.broadcast_to(u, (SUBLANES, GD_DV)).astype(BF16)
            s_ref[i, h] = g_h * s_h + _dot_tn(_row0(k).astype(BF16), u8)
            z = p_ref[i, :, GD_QKV + h * GD_DV:GD_QKV + (h + 1) * GD_DV]
            o_ref[i, :, h * GD_DV:(h + 1) * GD_DV] = _gdn_out(o, z, gout_ref[...]).astype(o_ref.dtype)


def _gdn_step(p, conv_prev, s0, layer_j, conv_w, a_log, dt_bias, g_out):
    bsz = p.shape[0]
    bb = STEP_BATCH
    cp_in, s_in, s_out, nc_out, shapes = _gdn_specs(bsz, layer_j, lambda i: i, bb)
    return pl.pallas_call(
        _gdn_step_kernel,
        out_shape=(jax.ShapeDtypeStruct((bsz, 1, D_MODEL), BF16),) + shapes,
        grid=(bsz // bb,),
        in_specs=[pl.BlockSpec((bb, 1, GD_PROJ_PAD), lambda i: (i, 0, 0)), cp_in,
                  _resident((GD_CONV, GD_QKV)), _resident((1, LANES)), _resident((1, LANES)),
                  _resident((1, GD_DV)), s_in],
        out_specs=(pl.BlockSpec((bb, 1, D_MODEL), lambda i: (i, 0, 0)), s_out, nc_out),
        compiler_params=_params(), name="gdn_step",
    )(p, conv_prev, conv_w, a_log, dt_bias, g_out, s0)


def _lane_pad(x):
    return jnp.pad(x, [(0, 0)] * (x.ndim - 1) + [(0, LANES - x.shape[-1])])


def _prep_weights(P):
    W = {}
    for name in ('ffn1_w_gate', 'ffn1_w_up', 'ffn1_w_down', 'ffn2_w_gate', 'ffn2_w_up', 'ffn2_w_down',
                 'ml_w_out', 'gd_w_out', 'xa_w_q', 'xa_w_k', 'xa_w_v', 'xa_w_o'):
        W[name] = P[name].astype(BF16)
    ml = P['ml_w_in']
    main = 2 * ML_HEADS * ML_DQK + 2 * D_MODEL
    W['ml_w_in'] = jnp.concatenate(
        [ml[..., :main], _lane_pad(ml[..., main:main + ML_HEADS]), _lane_pad(ml[..., main + ML_HEADS:])],
        axis=-1).astype(BF16)
    gd = P['gd_w_in']
    main = GD_QKV + D_MODEL
    W['gd_w_in'] = jnp.concatenate(
        [gd[..., :main], _lane_pad(gd[..., main:main + GD_HEADS]), _lane_pad(gd[..., main + GD_HEADS:])],
        axis=-1).astype(BF16)
    return W


def _trunk(x, mem_k, mem_v, ml_c, ml_n, ml_m, gd_s, gd_conv, P, W):
    bsz, length, _ = x.shape
    single = length == 1
    x = x.reshape(bsz * length, D_MODEL)
    ml_m = _lane_pad(ml_m)[:, :, None, :]
    new_c, new_n, new_m, new_s, new_conv = [], [], [], [], []
    y = None
    for layer in range(DEPTH):
        j = layer // 2
        x = _ffn(x, P['g_ffn1'][layer], W['ffn1_w_gate'][layer], W['ffn1_w_up'][layer], W['ffn1_w_down'][layer])
        if layer % 2 == 0:
            p = _proj(x, W['ml_w_in'][j], g=P['g_mix'][layer]).reshape(bsz, length, ML_PROJ_PAD)
            fn = _mlstm_step if single else _mlstm_chunked
            a, c, n, m = fn(p, ml_c, ml_n, ml_m, j, _lane_pad(P['ml_b_i'][j][None]), _lane_pad(P['ml_b_f'][j][None]),
                            P['ml_g_head'][j][None])
            new_c.append(c)
            new_n.append(n)
            new_m.append(m[:, 0, :ML_HEADS])
            w_out = W['ml_w_out'][j]
        else:
            p = _proj(x, W['gd_w_in'][j], g=P['g_mix'][layer]).reshape(bsz, length, GD_PROJ_PAD)
            fn = _gdn_step if single else _gdn_chunked
            a, s, cv = fn(p, gd_conv, gd_s, j, P['gd_conv_w'][j], _lane_pad(P['gd_a_log'][j][None]),
                          _lane_pad(P['gd_dt_bias'][j][None]), P['gd_g_out'][j][None])
            new_s.append(s)
            new_conv.append(cv)
            w_out = W['gd_w_out'][j]
        x = _proj(a.reshape(bsz * length, D_MODEL), w_out, res=x)
        q = _proj(x, W['xa_w_q'][layer], g=P['g_xattn'][layer])
        if single:
            o = _attn_step(q, mem_k, mem_v, layer)
        else:
            o = _attn(q.reshape(bsz, length, D_MODEL), mem_k, mem_v, layer).reshape(bsz * length, D_MODEL)
        x = _proj(o, W['xa_w_o'][layer], res=x)
        if layer == DEPTH - 1:
            x, y = _ffn(x, P['g_ffn2'][layer], W['ffn2_w_gate'][layer], W['ffn2_w_up'][layer],
                        W['ffn2_w_down'][layer], g_final=P['g_final'])
        else:
            x = _ffn(x, P['g_ffn2'][layer], W['ffn2_w_gate'][layer], W['ffn2_w_up'][layer], W['ffn2_w_down'][layer])
    return (y.reshape(bsz, length, D_MODEL), jnp.stack(new_c), jnp.stack(new_n), jnp.stack(new_m),
            jnp.stack(new_s), jnp.stack(new_conv))


def kernel(x_prompt, x_sample, mem_prompt, cache_mem_k, cache_mem_v, state_mlstm_C, state_mlstm_n, state_mlstm_m, state_gdn_S, state_gdn_conv, g_ffn1, ffn1_w_gate, ffn1_w_up, ffn1_w_down, g_mix, ml_w_in, ml_b_i, ml_b_f, ml_g_head, ml_w_out, gd_w_in, gd_conv_w, gd_a_log, gd_dt_bias, gd_g_out, gd_w_out, g_xattn, g_mem, xa_w_q, xa_w_k, xa_w_v, xa_w_o, g_ffn2, ffn2_w_gate, ffn2_w_up, ffn2_w_down, g_final):
    P = dict(g_ffn1=g_ffn1, ffn1_w_gate=ffn1_w_gate, ffn1_w_up=ffn1_w_up, ffn1_w_down=ffn1_w_down, g_mix=g_mix,
             ml_w_in=ml_w_in, ml_b_i=ml_b_i, ml_b_f=ml_b_f, ml_g_head=ml_g_head, ml_w_out=ml_w_out,
             gd_w_in=gd_w_in, gd_conv_w=gd_conv_w, gd_a_log=gd_a_log, gd_dt_bias=gd_dt_bias, gd_g_out=gd_g_out,
             gd_w_out=gd_w_out, g_xattn=g_xattn, xa_w_q=xa_w_q, xa_w_k=xa_w_k, xa_w_v=xa_w_v, xa_w_o=xa_w_o,
             g_ffn2=g_ffn2, ffn2_w_gate=ffn2_w_gate, ffn2_w_up=ffn2_w_up, ffn2_w_down=ffn2_w_down, g_final=g_final)
    W = _prep_weights(P)
    batch, n_mem, _ = mem_prompt.shape
    dec_batch = x_sample.shape[0]
    n_ml, n_gd = state_mlstm_C.shape[0], state_gdn_S.shape[0]

    pk, pv = _memkv(mem_prompt.reshape(batch * n_mem, D_MODEL), g_mem, W['xa_w_k'], W['xa_w_v'])
    pk = pk.reshape(DEPTH, batch, n_mem, D_MODEL)
    pv = pv.reshape(DEPTH, batch, n_mem, D_MODEL)
    z_c = jnp.zeros((n_ml, batch, ML_HEADS, ML_DQK, ML_DV), F32)
    z_n = jnp.zeros((n_ml, batch, ML_HEADS, ML_DQK), F32)
    z_m = jnp.zeros((n_ml, batch, ML_HEADS), F32)
    z_s = jnp.zeros((n_gd, batch, GD_HEADS, GD_DK, GD_DV), F32)
    z_conv = jnp.zeros((n_gd, batch, GD_CONV - 1, GD_QKV), F32)
    y_p, p_c, p_n, p_m, p_s, p_conv = _trunk(x_prompt, pk, pv, z_c, z_n, z_m, z_s, z_conv, P, W)

    y_s, s_c, s_n, s_m, s_s, s_conv = _trunk(
        x_sample, cache_mem_k, cache_mem_v,
        state_mlstm_C, state_mlstm_n, state_mlstm_m, state_gdn_S, state_gdn_conv, P, W)

    kv_shape = (DEPTH, batch, n_mem, XA_HEADS, XA_DH)
    return (y_p, y_s, pk.reshape(kv_shape), pv.reshape(kv_shape), p_c, p_n, p_m, p_s, p_conv,
            s_c, s_n, s_m, s_s, s_conv)
```

```python
import functools

import jax
import jax.numpy as jnp
from jax import lax
from jax.experimental import pallas as pl
from jax.experimental.pallas import tpu as pltpu

F32 = jnp.float32
BF16 = jnp.bfloat16

D_MODEL = 1024
DEPTH = 4
N_MEM = 256
D_FF = 2816
FFN_RES = 0.5
EPS = 1e-6
ML_HEADS = 4
ML_DV = 256
ML_DQK = 128
ML_MAIN = 2 * ML_HEADS * ML_DQK + 2 * D_MODEL
GD_HEADS = 8
GD_DK = 128
GD_DV = 128
GD_CONV = 4
GD_QKV = 3072
GD_MAIN = GD_QKV + D_MODEL
XA_HEADS = 4
XA_DH = 256
CHUNK = 64
LANES = 128
SUBLANES = 8
GATE_COLS = 2 * LANES
VMEM_LIMIT_BYTES = 56 * 1024 * 1024
HI = lax.Precision.HIGHEST


def _params():
    return pltpu.CompilerParams(vmem_limit_bytes=VMEM_LIMIT_BYTES)


def _rms(x, g):
    return x * lax.rsqrt(jnp.mean(x * x, axis=-1, keepdims=True) + EPS) * g


def _silu(x):
    return x * jax.nn.sigmoid(x)


def _softplus(x):
    return jnp.maximum(x, 0.0) + jnp.log1p(jnp.exp(-jnp.abs(x)))


def _log_sigmoid(x):
    return jnp.minimum(x, 0.0) - jnp.log1p(jnp.exp(-jnp.abs(x)))


def _dot(a, b):
    return jnp.dot(a, b, preferred_element_type=F32)


def _dot_nt(a, b):
    return lax.dot_general(a, b, (((1,), (1,)), ((), ())), preferred_element_type=F32)


def _dot_tn(a, b):
    return lax.dot_general(a, b, (((0,), (0,)), ((), ())), preferred_element_type=F32)


def _resident(shape):
    nd = len(shape)
    return pl.BlockSpec(shape, lambda *_: (0,) * nd, pipeline_mode=pl.Buffered(1))


def _layer_block(arr, index):
    nd = arr.ndim - 1
    return pl.BlockSpec((None,) + arr.shape[1:], lambda *_: (index,) + (0,) * nd, pipeline_mode=pl.Buffered(1))


def _ffn_kernel(x_ref, g_ref, wg_ref, wu_ref, wd_ref, *rest, fchunk, final):
    if final:
        gf_ref, o_ref, y_ref, h_ref, a_ref = rest
    else:
        o_ref, h_ref, a_ref = rest
    h_ref[...] = _rms(x_ref[...], g_ref[...]).astype(BF16)
    for j in range(D_FF // fchunk):
        sl = slice(j * fchunk, (j + 1) * fchunk)
        h = h_ref[...]
        gate = _dot(h, wg_ref[:, sl])
        up = _dot(h, wu_ref[:, sl])
        a_ref[:, sl] = (_silu(gate) * up).astype(BF16)
    out = x_ref[...] + FFN_RES * _dot(a_ref[...], wd_ref[...])
    o_ref[...] = out
    if final:
        y_ref[...] = _rms(out, gf_ref[...])


def _ffn(x, g, wg, wu, wd, layer, g_final=None):
    m = x.shape[0]
    tm = min(m, 512)
    final = g_final is not None
    row = pl.BlockSpec((tm, D_MODEL), lambda i: (i, 0))
    in_specs = [row, _layer_block(g, layer), _layer_block(wg, layer), _layer_block(wu, layer), _layer_block(wd, layer)]
    args = [x, g, wg, wu, wd]
    out_shape = jax.ShapeDtypeStruct((m, D_MODEL), F32)
    out_specs = row
    if final:
        in_specs.append(_resident((1, D_MODEL)))
        args.append(g_final.reshape(1, D_MODEL))
        out_shape = (out_shape, out_shape)
        out_specs = (row, row)
    return pl.pallas_call(
        functools.partial(_ffn_kernel, fchunk=256, final=final),
        out_shape=out_shape, grid=(m // tm,), in_specs=in_specs, out_specs=out_specs,
        scratch_shapes=[pltpu.VMEM((tm, D_MODEL), BF16), pltpu.VMEM((tm, D_FF), BF16)],
        compiler_params=_params(), name="ffn_final" if final else "ffn",
    )(*args)


def _matmul_chunks(h, w_ref, o_ref, nchunk, res_ref=None):
    n = w_ref.shape[1]
    for n0 in range(0, n, nchunk):
        n1 = min(n, n0 + nchunk)
        y = _dot(h, w_ref[:, n0:n1])
        if res_ref is not None:
            y = y + res_ref[:, n0:n1]
        o_ref[:, n0:n1] = y.astype(o_ref.dtype)


def _proj_kernel(*refs, norm, res, nchunk):
    refs = list(refs)
    x_ref = refs.pop(0)
    g_ref = refs.pop(0) if norm else None
    w_ref = refs.pop(0)
    r_ref = refs.pop(0) if res else None
    o_ref = refs.pop(0)
    if norm:
        h = _rms(x_ref[...], g_ref[...]).astype(BF16)
    else:
        h = x_ref[...].astype(BF16)
    _matmul_chunks(h, w_ref, o_ref, nchunk, r_ref)


def _proj(x, w, w_index, g=None, g_index=None, res=None, out_dtype=F32):
    m, k = x.shape
    n = w.shape[-1]
    tm = min(m, 512)
    in_specs = [pl.BlockSpec((tm, k), lambda i: (i, 0))]
    args = [x]
    if g is not None:
        in_specs.append(_layer_block(g, g_index))
        args.append(g)
    in_specs.append(_layer_block(w, w_index))
    args.append(w)
    if res is not None:
        in_specs.append(pl.BlockSpec((tm, n), lambda i: (i, 0)))
        args.append(res)
    return pl.pallas_call(
        functools.partial(_proj_kernel, norm=g is not None, res=res is not None, nchunk=512),
        out_shape=jax.ShapeDtypeStruct((m, n), out_dtype), grid=(m // tm,), in_specs=in_specs,
        out_specs=pl.BlockSpec((tm, n), lambda i: (i, 0)),
        compiler_params=_params(), name="proj",
    )(*args)


def _inproj_kernel(x_ref, g_ref, wm_ref, wg_ref, om_ref, og_ref, *, nchunk):
    h = _rms(x_ref[...], g_ref[...]).astype(BF16)
    _matmul_chunks(h, wm_ref, om_ref, nchunk)
    og_ref[...] = _dot(h, wg_ref[...])


def _inproj(x, g, g_index, w_main, w_gate, w_index, main_dtype):
    m = x.shape[0]
    n = w_main.shape[-1]
    tm = min(m, 512)
    rows = lambda width: pl.BlockSpec((tm, width), lambda i: (i, 0))
    return pl.pallas_call(
        functools.partial(_inproj_kernel, nchunk=512),
        out_shape=(jax.ShapeDtypeStruct((m, n), main_dtype), jax.ShapeDtypeStruct((m, GATE_COLS), F32)),
        grid=(m // tm,),
        in_specs=[rows(D_MODEL), _layer_block(g, g_index), _layer_block(w_main, w_index), _layer_block(w_gate, w_index)],
        out_specs=(rows(n), rows(GATE_COLS)),
        compiler_params=_params(), name="inproj",
    )(x, g, w_main, w_gate)


def _memkv_kernel(x_ref, g_ref, wk_ref, wv_ref, k_ref, v_ref, kb_ref, vb_ref):
    h = _rms(x_ref[...], g_ref[...]).astype(BF16)
    k = _dot(h, wk_ref[...])
    v = _dot(h, wv_ref[...])
    kb_ref[0] = k.astype(BF16)
    vb_ref[0] = v.astype(BF16)
    for i in range(XA_HEADS):
        sl = slice(i * XA_DH, (i + 1) * XA_DH)
        k_ref[0, :, i, :] = k[:, sl]
        v_ref[0, :, i, :] = v[:, sl]


def _memkv(mem, g_mem, wk, wv):
    m = mem.shape[0]
    tm = 512
    w_spec = pl.BlockSpec((None, D_MODEL, D_MODEL), lambda l, i: (l, 0, 0))
    o_spec = pl.BlockSpec((1, tm, XA_HEADS, XA_DH), lambda l, i: (l, i, 0, 0))
    b_spec = pl.BlockSpec((1, tm, D_MODEL), lambda l, i: (l, i, 0))
    o_shape = jax.ShapeDtypeStruct((DEPTH, m, XA_HEADS, XA_DH), F32)
    b_shape = jax.ShapeDtypeStruct((DEPTH, m, D_MODEL), BF16)
    return pl.pallas_call(
        _memkv_kernel, out_shape=(o_shape, o_shape, b_shape, b_shape), grid=(DEPTH, m // tm),
        in_specs=[pl.BlockSpec((tm, D_MODEL), lambda l, i: (i, 0)),
                  pl.BlockSpec((None, 1, D_MODEL), lambda l, i: (l, 0, 0)), w_spec, w_spec],
        out_specs=(o_spec, o_spec, b_spec, b_spec), compiler_params=_params(), name="memkv",
    )(mem, g_mem, wk, wv)


def _xattn_kernel(x_ref, a_ref, k_ref, v_ref, g_ref, wout_ref, wq_ref, wo_ref, o_ref, att_ref):
    x2 = x_ref[0] + _dot(a_ref[0], wout_ref[...])
    q = _dot(_rms(x2, g_ref[...]).astype(BF16), wq_ref[...]).astype(BF16)
    for h in range(XA_HEADS):
        sl = slice(h * XA_DH, (h + 1) * XA_DH)
        s = _dot_nt(q[:, sl], k_ref[0, :, sl]) * XA_DH ** -0.5
        e = jnp.exp(s - jnp.max(s, axis=-1, keepdims=True))
        p = e / jnp.sum(e, axis=-1, keepdims=True)
        att_ref[:, sl] = _dot(p.astype(BF16), v_ref[0, :, sl]).astype(BF16)
    o_ref[0] = x2 + _dot(att_ref[...], wo_ref[...])


def _xattn_block(x, a, mem_kb, mem_vb, layer, g, w_out, w_out_index, wq, wo):
    b, l, _ = x.shape
    tq = min(l, 512)
    row = pl.BlockSpec((1, tq, D_MODEL), lambda i, j: (i, j, 0))
    kv_spec = pl.BlockSpec((None, 1, N_MEM, D_MODEL), lambda i, j: (layer, i, 0, 0))
    return pl.pallas_call(
        _xattn_kernel, out_shape=jax.ShapeDtypeStruct((b, l, D_MODEL), F32), grid=(b, l // tq),
        in_specs=[row, row, kv_spec, kv_spec, _layer_block(g, layer), _layer_block(w_out, w_out_index),
                  _layer_block(wq, layer), _layer_block(wo, layer)],
        out_specs=row, scratch_shapes=[pltpu.VMEM((tq, D_MODEL), BF16)],
        compiler_params=_params(), name="xattn",
    )(x, a, mem_kb, mem_vb, g, w_out, wq, wo)


ATTN_STEP_BATCH = 8


def _attn_step_kernel(q_ref, k_ref, v_ref, o_ref):
    for i in range(ATTN_STEP_BATCH):
        q = q_ref[0, i] * XA_DH ** -0.5
        s = jnp.sum(k_ref[0, i] * q[None], axis=-1, keepdims=True)
        e = jnp.exp(s - jnp.max(s, axis=0, keepdims=True))
        acc = jnp.sum(e * v_ref[0, i], axis=0)
        o_ref[0, i] = acc / jnp.sum(e, axis=0)


def _attn_step(q, mem_k, mem_v, layer):
    b = q.shape[0]
    bb = ATTN_STEP_BATCH
    kv_spec = pl.BlockSpec((1, bb, N_MEM, XA_HEADS, XA_DH), lambda i: (layer, i, 0, 0, 0))
    q_spec = pl.BlockSpec((1, bb, XA_HEADS, XA_DH), lambda i: (i, 0, 0, 0))
    out = pl.pallas_call(
        _attn_step_kernel, out_shape=jax.ShapeDtypeStruct((b // bb, bb, XA_HEADS, XA_DH), F32), grid=(b // bb,),
        in_specs=[q_spec, kv_spec, kv_spec], out_specs=q_spec,
        compiler_params=_params(), name="attn_step",
    )(q.reshape(b // bb, bb, XA_HEADS, XA_DH), mem_k, mem_v)
    return out.reshape(b, D_MODEL)


def _zero_other_layers(ref):
    ref[1:] = jnp.zeros((ref.shape[0] - 1,) + ref.shape[1:], F32)


def _state_out(n_layers, layer_j, bsz, bb, tail, idx, prev):
    shape = jax.ShapeDtypeStruct((n_layers, bsz) + tail, F32)
    zeros = (0,) * len(tail)
    if prev is None:
        spec = pl.BlockSpec((n_layers, bb) + tail, lambda *g: (0, idx(*g)) + zeros)
        return shape, spec, [], [], True
    spec = pl.BlockSpec((1, bb) + tail, lambda *g: (layer_j, idx(*g)) + zeros)
    return shape, spec, [pl.BlockSpec(memory_space=pl.ANY)], [prev], False


def _lane_select(h, value, into):
    lane = lax.broadcasted_iota(jnp.int32, into.shape, 1)
    return jnp.where(lane == h, value, into)


def _mlstm_chunk_kernel(p_ref, gt_ref, c0_ref, n0_ref, m0_ref, bi_ref, bf_ref, gh_ref, *rest, c, wide, aliased):
    if aliased:
        rest = rest[1:]
    o_ref, c_ref, n_ref, m_ref = rest

    @pl.when(pl.program_id(1) == 0)
    def _():
        c_ref[0, 0] = c0_ref[0, 0]
        n_ref[0] = n0_ref[0, 0]
        m_ref[0] = m0_ref[0, 0]
        if wide:
            _zero_other_layers(c_ref)

    hq = ML_HEADS * ML_DQK
    heads = range(ML_HEADS)
    hs = [slice(h, h + 1) for h in heads]
    i_pre = gt_ref[0, :, 0:LANES] + bi_ref[...]
    log_f = _log_sigmoid(gt_ref[0, :, LANES:GATE_COLS] + bf_ref[...])
    row = lax.broadcasted_iota(jnp.int32, (c, c), 0)
    col = lax.broadcasted_iota(jnp.int32, (c, c), 1)
    incl = row >= col
    b = jnp.dot(incl.astype(F32), log_f, precision=HI, preferred_element_type=F32)
    m_prev = m_ref[0]
    a_inter = b + m_prev
    xt = (i_pre - b).T
    b_last = b[c - 1:c, :]
    scale = ML_DQK ** -0.5
    qbs = [p_ref[0, :, h * ML_DQK:(h + 1) * ML_DQK] for h in heads]
    kbs = [p_ref[0, :, hq + h * ML_DQK:hq + (h + 1) * ML_DQK] for h in heads]
    vbs = [p_ref[0, :, 2 * hq + h * ML_DV:2 * hq + (h + 1) * ML_DV] for h in heads]
    dms = [jnp.where(incl, b[:, hs[h]] + xt[hs[h], :], -jnp.inf) for h in heads]
    mts = [jnp.maximum(a_inter[:, hs[h]], jnp.max(dms[h], axis=-1, keepdims=True)) for h in heads]
    w_inters = [jnp.exp(a_inter[:, hs[h]] - mts[h]) for h in heads]
    ss = [_dot_nt(qbs[h], kbs[h]) * (jnp.exp(dms[h] - mts[h]) * scale) for h in heads]
    c_old = [c_ref[0, 0, h] for h in heads]
    n_old = [n_ref[0, hs[h], :] for h in heads]
    nums = [w_inters[h] * _dot(qbs[h], c_old[h].astype(BF16)) + _dot(ss[h].astype(BF16), vbs[h]) for h in heads]
    m_row = m_prev
    for h in heads:
        m_new = mts[h][c - 1:c, :]
        w_k = jnp.exp(b_last[:, hs[h]] - b[:, hs[h]] + i_pre[:, hs[h]] - m_new) * scale
        decay = jnp.exp(b_last[:, hs[h]] + m_prev[:, hs[h]] - m_new)
        kw = kbs[h].astype(F32) * w_k
        c_ref[0, 0, h] = decay * c_old[h] + _dot_tn(kw.astype(BF16), vbs[h])
        n_ref[0, hs[h], :] = decay * n_old[h] + jnp.sum(kw, axis=0, keepdims=True)
        m_row = _lane_select(h, m_new, m_row)
    m_ref[0] = m_row
    for h in heads:
        den = (w_inters[h] * jnp.sum(qbs[h].astype(F32) * n_old[h], axis=-1, keepdims=True)
               + jnp.sum(ss[h], axis=-1, keepdims=True))
        hh = nums[h] / jnp.maximum(jnp.abs(den), jnp.exp(-mts[h]))
        vs = slice(h * ML_DV, (h + 1) * ML_DV)
        hn = hh * lax.rsqrt(jnp.mean(hh * hh, axis=-1, keepdims=True) + EPS) * gh_ref[:, vs]
        og = p_ref[0, :, 2 * hq + D_MODEL + h * ML_DV:2 * hq + D_MODEL + (h + 1) * ML_DV].astype(F32)
        o_ref[0, :, vs] = (hn * jax.nn.sigmoid(og)).astype(o_ref.dtype)


def _mlstm_small_specs(bsz, layer_j, idx, bb):
    c_in = pl.BlockSpec((1, bb, ML_HEADS, ML_DQK, ML_DV), lambda *g: (layer_j, idx(*g), 0, 0, 0))
    n_in = pl.BlockSpec((1, bb, ML_HEADS, ML_DQK), lambda *g: (layer_j, idx(*g), 0, 0))
    m_in = pl.BlockSpec((1, bb, 1, LANES), lambda *g: (layer_j, idx(*g), 0, 0))
    n_out = pl.BlockSpec((bb, ML_HEADS, ML_DQK), lambda *g: (idx(*g), 0, 0))
    m_out = pl.BlockSpec((bb, 1, LANES), lambda *g: (idx(*g), 0, 0))
    shapes = (jax.ShapeDtypeStruct((bsz, ML_HEADS, ML_DQK), F32), jax.ShapeDtypeStruct((bsz, 1, LANES), F32))
    return (c_in, n_in, m_in), (n_out, m_out), shapes


def _mlstm_call(kernel_fn, name, p, gates, c0, n0, m0, layer_j, b_i, b_f, g_head, c_prev, grid, row_block, idx, bb):
    bsz = p.shape[0]
    n_layers = c0.shape[0]
    s_in, s_out, s_shapes = _mlstm_small_specs(bsz, layer_j, idx, bb)
    c_shape, c_spec, extra_specs, extra_args, wide = _state_out(
        n_layers, layer_j, bsz, bb, (ML_HEADS, ML_DQK, ML_DV), idx, c_prev)
    n_in = 8
    return pl.pallas_call(
        functools.partial(kernel_fn, wide=wide, aliased=not wide),
        out_shape=(jax.ShapeDtypeStruct(p.shape[:2] + (D_MODEL,), BF16), c_shape) + s_shapes,
        grid=grid,
        in_specs=[row_block(ML_MAIN), row_block(GATE_COLS), *s_in, _layer_block(b_i, layer_j),
                  _layer_block(b_f, layer_j), _layer_block(g_head, layer_j), *extra_specs],
        out_specs=(row_block(D_MODEL), c_spec) + s_out,
        input_output_aliases={} if wide else {n_in: 1},
        compiler_params=_params(), name=name,
    )(p, gates, c0, n0, m0, b_i, b_f, g_head, *extra_args)


def _mlstm_chunked(p, gates, c0, n0, m0, layer_j, b_i, b_f, g_head, c_prev):
    bsz, length, _ = p.shape
    row_block = lambda width: pl.BlockSpec((1, CHUNK, width), lambda i, j: (i, j, 0))
    return _mlstm_call(functools.partial(_mlstm_chunk_kernel, c=CHUNK), "mlstm_chunk", p, gates, c0, n0, m0, layer_j,
                       b_i, b_f, g_head, c_prev, (bsz, length // CHUNK), row_block, lambda i, j: i, 1)


STEP_BATCH = 4


def _row0(x, rows=SUBLANES):
    r = lax.broadcasted_iota(jnp.int32, (rows, x.shape[1]), 0)
    return jnp.where(r == 0, x, 0.0)


def _mlstm_step_kernel(p_ref, gt_ref, c0_ref, n0_ref, m0_ref, bi_ref, bf_ref, gh_ref, *rest, wide, aliased):
    if aliased:
        rest = rest[1:]
    o_ref, c_ref, n_ref, m_ref = rest
    hq = ML_HEADS * ML_DQK
    if wide:
        _zero_other_layers(c_ref)
    for i in range(STEP_BATCH):
        i_pre = gt_ref[i, :, 0:LANES] + bi_ref[...]
        log_f = _log_sigmoid(gt_ref[i, :, LANES:GATE_COLS] + bf_ref[...])
        m_prev = m0_ref[0, i]
        mt = jnp.maximum(log_f + m_prev, i_pre)
        w_inter = jnp.exp(log_f + m_prev - mt)
        w_k = jnp.exp(i_pre - mt)
        e_inv = jnp.exp(-mt)
        m_ref[i] = mt
        for h in range(ML_HEADS):
            hs = slice(h, h + 1)
            q = p_ref[i, :, h * ML_DQK:(h + 1) * ML_DQK].astype(F32)
            k = p_ref[i, :, hq + h * ML_DQK:hq + (h + 1) * ML_DQK].astype(F32) * ML_DQK ** -0.5
            v = p_ref[i, :, 2 * hq + h * ML_DV:2 * hq + (h + 1) * ML_DV].astype(F32)
            c_h = c0_ref[0, i, h]
            n_h = n0_ref[0, i, hs, :]
            wi = w_inter[:, hs]
            s = jnp.sum(q * k, axis=-1, keepdims=True) * w_k[:, hs]
            q_c = _dot(_row0(q).astype(BF16), c_h.astype(BF16))[0:1]
            num = wi * q_c + s * v
            den = wi * jnp.sum(q * n_h, axis=-1, keepdims=True) + s
            hh = num / jnp.maximum(jnp.abs(den), e_inv[:, hs])
            kw = k * w_k[:, hs]
            v8 = jnp.broadcast_to(v, (SUBLANES, ML_DV)).astype(BF16)
            c_ref[0, i, h] = wi * c_h + _dot_tn(_row0(kw).astype(BF16), v8)
            n_ref[i, hs, :] = wi * n_h + kw
            vs = slice(h * ML_DV, (h + 1) * ML_DV)
            hn = hh * lax.rsqrt(jnp.mean(hh * hh, axis=-1, keepdims=True) + EPS) * gh_ref[:, vs]
            og = p_ref[i, :, 2 * hq + D_MODEL + h * ML_DV:2 * hq + D_MODEL + (h + 1) * ML_DV].astype(F32)
            o_ref[i, :, vs] = (hn * jax.nn.sigmoid(og)).astype(o_ref.dtype)


def _mlstm_step(p, gates, c0, n0, m0, layer_j, b_i, b_f, g_head, c_prev):
    bsz = p.shape[0]
    bb = STEP_BATCH
    row_block = lambda width: pl.BlockSpec((bb, 1, width), lambda i: (i, 0, 0))
    return _mlstm_call(_mlstm_step_kernel, "mlstm_step", p, gates, c0, n0, m0, layer_j,
                       b_i, b_f, g_head, c_prev, (bsz // bb,), row_block, lambda i: i, bb)


def _l2n(x):
    return x * lax.rsqrt(jnp.sum(x * x, axis=-1, keepdims=True) + EPS)


INV_BASE = 16


def _unit_lower_inverses(a_list, c):
    row = lax.broadcasted_iota(jnp.int32, (c, c), 0)
    col = lax.broadcasted_iota(jnp.int32, (c, c), 1)
    eye = jnp.where(row == col, 1.0, 0.0)
    same = lambda size: (row >> (size.bit_length() - 1)) == (col >> (size.bit_length() - 1))
    ns = [jnp.where(same(INV_BASE), -a, 0.0) for a in a_list]
    ts = [eye + n for n in ns]
    nbs = [n.astype(BF16) for n in ns]
    power = 2
    while power < INV_BASE:
        ns = [_dot(nb, nb) for nb in nbs]
        nbs = [n.astype(BF16) for n in ns]
        ts = [t + _dot(t.astype(BF16), nb) for t, nb in zip(ts, nbs)]
        power *= 2
    size = INV_BASE
    while size < c:
        off = same(2 * size) & jnp.logical_not(same(size))
        tbs = [t.astype(BF16) for t in ts]
        mids = [_dot(tb, jnp.where(off, a, 0.0).astype(BF16)).astype(BF16) for tb, a in zip(tbs, a_list)]
        ts = [t - _dot(mid, tb) for t, mid, tb in zip(ts, mids, tbs)]
        size *= 2
    return ts


def _gdn_gates(gt_ref, i, alog_ref, dtb_ref):
    beta = jax.nn.sigmoid(gt_ref[i, :, 0:LANES])
    log_g = -jnp.exp(alog_ref[...]) * _softplus(gt_ref[i, :, LANES:GATE_COLS] + dtb_ref[...])
    return beta, log_g


def _gdn_out(o, z, gout):
    return (o * lax.rsqrt(jnp.mean(o * o, axis=-1, keepdims=True) + EPS) * gout * _silu(z))


def _gdn_chunk_kernel(p_ref, gt_ref, cp_ref, cw_ref, alog_ref, dtb_ref, gout_ref, s0_ref, *rest, c, wide, aliased):
    if aliased:
        rest = rest[1:]
    o_ref, s_ref, nc_ref, e_ref = rest
    pad = SUBLANES
    keep = GD_CONV - 1

    @pl.when(pl.program_id(1) == 0)
    def _():
        s_ref[0, 0] = s0_ref[0, 0]
        e_ref[pl.ds(pad - keep, keep), :] = cp_ref[0, 0]
        if wide:
            _zero_other_layers(s_ref)

    e_ref[pl.ds(pad, c), :] = p_ref[0, :, 0:GD_QKV]
    conv = e_ref[pl.ds(pad - keep, c), :] * cw_ref[0:1, :]
    for j in range(1, GD_CONV):
        conv = conv + e_ref[pl.ds(pad - keep + j, c), :] * cw_ref[j:j + 1, :]
    conv = _silu(conv)
    nc_ref[0] = e_ref[pl.ds(pad + c - keep, keep), :]
    e_ref[pl.ds(0, pad), :] = e_ref[pl.ds(c, pad), :]

    beta, log_g = _gdn_gates(gt_ref, 0, alog_ref, dtb_ref)
    row = lax.broadcasted_iota(jnp.int32, (c, c), 0)
    col = lax.broadcasted_iota(jnp.int32, (c, c), 1)
    incl = row >= col
    strict = row > col
    gam = jnp.dot(incl.astype(F32), log_g, precision=HI, preferred_element_type=F32)
    gam_t = gam.T
    e_gam = jnp.exp(gam)
    g_last = gam[c - 1:c, :]
    e_last = jnp.exp(g_last)
    hk = GD_HEADS * GD_DK
    heads = range(GD_HEADS)
    hs = [slice(h, h + 1) for h in heads]
    qs = [_l2n(conv[:, h * GD_DK:(h + 1) * GD_DK]) * GD_DK ** -0.5 for h in heads]
    ks = [_l2n(conv[:, hk + h * GD_DK:hk + (h + 1) * GD_DK]) for h in heads]
    vs = [conv[:, 2 * hk + h * GD_DV:2 * hk + (h + 1) * GD_DV] for h in heads]
    kbs = [k.astype(BF16) for k in ks]
    decs = [jnp.exp(jnp.where(incl, gam[:, hs[h]] - gam_t[hs[h], :], -jnp.inf)) for h in heads]
    kqs = [_dot_nt(jnp.concatenate([kbs[h], qs[h].astype(BF16)], axis=0), kbs[h]) for h in heads]
    a_list = [jnp.where(strict, beta[:, hs[h]] * kqs[h][0:c] * decs[h], 0.0) for h in heads]
    ts = _unit_lower_inverses(a_list, c)
    rhs = [jnp.concatenate([beta[:, hs[h]] * vs[h], (beta[:, hs[h]] * e_gam[:, hs[h]]) * ks[h]], axis=1).astype(BF16)
           for h in heads]
    uws = [_dot(ts[h].astype(BF16), rhs[h]) for h in heads]
    s_old = [s_ref[0, 0, h] for h in heads]
    wq = [jnp.concatenate([uws[h][:, GD_DV:], qs[h] * e_gam[:, hs[h]]], axis=0).astype(BF16) for h in heads]
    wqs = [_dot(wq[h], s_old[h].astype(BF16)) for h in heads]
    ubs = [(uws[h][:, :GD_DV] - wqs[h][0:c]).astype(BF16) for h in heads]
    for h in heads:
        k_dec = ks[h] * jnp.exp(g_last[:, hs[h]] - gam[:, hs[h]])
        s_ref[0, 0, h] = e_last[:, hs[h]] * s_old[h] + _dot_tn(k_dec.astype(BF16), ubs[h])
    for h in heads:
        o = wqs[h][c:2 * c] + _dot((kqs[h][c:2 * c] * decs[h]).astype(BF16), ubs[h])
        z = p_ref[0, :, GD_QKV + h * GD_DV:GD_QKV + (h + 1) * GD_DV]
        o_ref[0, :, h * GD_DV:(h + 1) * GD_DV] = _gdn_out(o, z, gout_ref[...]).astype(o_ref.dtype)


def _gdn_call(kernel_fn, name, p, gates, conv_prev, s0, layer_j, conv_w, a_log, dt_bias, g_out, s_prev,
              grid, row_block, idx, bb, scratch):
    bsz = p.shape[0]
    n_layers = s0.shape[0]
    cp_in = pl.BlockSpec((1, bb, GD_CONV - 1, GD_QKV), lambda *g: (layer_j, idx(*g), 0, 0))
    s_in = pl.BlockSpec((1, bb, GD_HEADS, GD_DK, GD_DV), lambda *g: (layer_j, idx(*g), 0, 0, 0))
    nc_out = pl.BlockSpec((bb, GD_CONV - 1, GD_QKV), lambda *g: (idx(*g), 0, 0))
    s_shape, s_spec, extra_specs, extra_args, wide = _state_out(
        n_layers, layer_j, bsz, bb, (GD_HEADS, GD_DK, GD_DV), idx, s_prev)
    n_in = 8
    return pl.pallas_call(
        functools.partial(kernel_fn, wide=wide, aliased=not wide),
        out_shape=(jax.ShapeDtypeStruct(p.shape[:2] + (D_MODEL,), BF16), s_shape,
                   jax.ShapeDtypeStruct((bsz, GD_CONV - 1, GD_QKV), F32)),
        grid=grid,
        in_specs=[row_block(GD_MAIN), row_block(GATE_COLS), cp_in, _layer_block(conv_w, layer_j),
                  _layer_block(a_log, layer_j), _layer_block(dt_bias, layer_j), _layer_block(g_out, layer_j), s_in,
                  *extra_specs],
        out_specs=(row_block(D_MODEL), s_spec, nc_out),
        input_output_aliases={} if wide else {n_in: 1},
        scratch_shapes=scratch, compiler_params=_params(), name=name,
    )(p, gates, conv_prev, conv_w, a_log, dt_bias, g_out, s0, *extra_args)


def _gdn_chunked(p, gates, conv_prev, s0, layer_j, conv_w, a_log, dt_bias, g_out, s_prev):
    bsz, length, _ = p.shape
    row_block = lambda width: pl.BlockSpec((1, CHUNK, width), lambda i, j: (i, j, 0))
    return _gdn_call(functools.partial(_gdn_chunk_kernel, c=CHUNK), "gdn_chunk", p, gates, conv_prev, s0, layer_j,
                     conv_w, a_log, dt_bias, g_out, s_prev, (bsz, length // CHUNK), row_block, lambda i, j: i, 1,
                     [pltpu.VMEM((CHUNK + SUBLANES, GD_QKV), F32)])


def _gdn_step_kernel(p_ref, gt_ref, cp_ref, cw_ref, alog_ref, dtb_ref, gout_ref, s0_ref, *rest, wide, aliased):
    if aliased:
        rest = rest[1:]
    o_ref, s_ref, nc_ref = rest
    hk = GD_HEADS * GD_DK
    rows = lax.broadcasted_iota(jnp.int32, (SUBLANES, GD_DK), 0)
    if wide:
        _zero_other_layers(s_ref)
    for i in range(STEP_BATCH):
        qkv = p_ref[i, :, 0:GD_QKV]
        prev = cp_ref[0, i]
        conv = qkv * cw_ref[GD_CONV - 1:GD_CONV, :]
        for j in range(GD_CONV - 1):
            conv = conv + prev[j:j + 1, :] * cw_ref[j:j + 1, :]
        conv = _silu(conv)
        nc_ref[i] = jnp.concatenate([prev[1:GD_CONV - 1], qkv], axis=0)
        beta, log_g = _gdn_gates(gt_ref, i, alog_ref, dtb_ref)
        g = jnp.exp(log_g)
        for h in range(GD_HEADS):
            hs = slice(h, h + 1)
            q = _l2n(conv[:, h * GD_DK:(h + 1) * GD_DK]) * GD_DK ** -0.5
            k = _l2n(conv[:, hk + h * GD_DK:hk + (h + 1) * GD_DK])
            v = conv[:, 2 * hk + h * GD_DV:2 * hk + (h + 1) * GD_DV]
            s_h = s0_ref[0, i, h]
            kq = jnp.where(rows == 0, k, jnp.where(rows == 1, q, 0.0))
            kq_s = _dot(kq.astype(BF16), s_h.astype(BF16))
            g_h = g[:, hs]
            u = beta[:, hs] * (v - g_h * kq_s[0:1])
            o = g_h * kq_s[1:2] + jnp.sum(q * k, axis=-1, keepdims=True) * u
            u8 = jnp.broadcast_to(u, (SUBLANES, GD_DV)).astype(BF16)
            s_ref[0, i, h] = g_h * s_h + _dot_tn(_row0(k).astype(BF16), u8)
            z = p_ref[i, :, GD_QKV + h * GD_DV:GD_QKV + (h + 1) * GD_DV]
            o_ref[i, :, h * GD_DV:(h + 1) * GD_DV] = _gdn_out(o, z, gout_ref[...]).astype(o_ref.dtype)


def _gdn_step(p, gates, conv_prev, s0, layer_j, conv_w, a_log, dt_bias, g_out, s_prev):
    bsz = p.shape[0]
    bb = STEP_BATCH
    row_block = lambda width: pl.BlockSpec((bb, 1, width), lambda i: (i, 0, 0))
    return _gdn_call(_gdn_step_kernel, "gdn_step", p, gates, conv_prev, s0, layer_j,
                     conv_w, a_log, dt_bias, g_out, s_prev, (bsz // bb,), row_block, lambda i: i, bb, [])


def _lane_pad(x):
    return jnp.pad(x, [(0, 0)] * (x.ndim - 1) + [(0, LANES - x.shape[-1])])


def _prep_params(P):
    W = {}
    for name in ('ffn1_w_gate', 'ffn1_w_up', 'ffn1_w_down', 'ffn2_w_gate', 'ffn2_w_up', 'ffn2_w_down',
                 'ml_w_out', 'gd_w_out', 'xa_w_q', 'xa_w_k', 'xa_w_v', 'xa_w_o'):
        W[name] = P[name].astype(BF16)
    for name in ('g_ffn1', 'g_mix', 'g_xattn', 'g_mem', 'g_ffn2', 'ml_g_head', 'gd_g_out'):
        W[name] = P[name][:, None, :]
    for name in ('ml_b_i', 'ml_b_f', 'gd_a_log', 'gd_dt_bias'):
        W[name] = _lane_pad(P[name])[:, None, :]
    W['gd_conv_w'] = P['gd_conv_w']
    ml, gd = P['ml_w_in'], P['gd_w_in']
    W['ml_w_main'] = ml[..., :ML_MAIN].astype(BF16)
    W['ml_w_gate'] = jnp.concatenate(
        [_lane_pad(ml[..., ML_MAIN:ML_MAIN + ML_HEADS]), _lane_pad(ml[..., ML_MAIN + ML_HEADS:])], axis=-1).astype(BF16)
    W['gd_w_main'] = gd[..., :GD_MAIN].astype(BF16)
    W['gd_w_gate'] = jnp.concatenate(
        [_lane_pad(gd[..., GD_MAIN:GD_MAIN + GD_HEADS]), _lane_pad(gd[..., GD_MAIN + GD_HEADS:])], axis=-1).astype(BF16)
    return W


def _trunk(x, mem_k, mem_v, ml_c, ml_n, ml_m, gd_s, gd_conv, W):
    bsz, length, _ = x.shape
    single = length == 1
    x = x.reshape(bsz * length, D_MODEL)
    ml_m = _lane_pad(ml_m)[:, :, None, :]
    c_all = s_all = None
    new_n, new_m, new_conv = [], [], []
    y = None
    for layer in range(DEPTH):
        j = layer // 2
        x = _ffn(x, W['g_ffn1'], W['ffn1_w_gate'], W['ffn1_w_up'], W['ffn1_w_down'], layer)
        if layer % 2 == 0:
            p, gates = _inproj(x, W['g_mix'], layer, W['ml_w_main'], W['ml_w_gate'], j, BF16)
            fn = _mlstm_step if single else _mlstm_chunked
            a, c_all, n, m = fn(p.reshape(bsz, length, ML_MAIN), gates.reshape(bsz, length, GATE_COLS), ml_c, ml_n, ml_m,
                                j, W['ml_b_i'], W['ml_b_f'], W['ml_g_head'], c_all)
            new_n.append(n)
            new_m.append(m[:, 0, :ML_HEADS])
            w_out = W['ml_w_out']
        else:
            p, gates = _inproj(x, W['g_mix'], layer, W['gd_w_main'], W['gd_w_gate'], j, F32)
            fn = _gdn_step if single else _gdn_chunked
            a, s_all, cv = fn(p.reshape(bsz, length, GD_MAIN), gates.reshape(bsz, length, GATE_COLS), gd_conv, gd_s,
                              j, W['gd_conv_w'], W['gd_a_log'], W['gd_dt_bias'], W['gd_g_out'], s_all)
            new_conv.append(cv)
            w_out = W['gd_w_out']
        if single:
            a = a.reshape(bsz, D_MODEL)
            x = _proj(a, w_out, j, res=x)
            q = _proj(x, W['xa_w_q'], layer, g=W['g_xattn'], g_index=layer)
            x = _proj(_attn_step(q, mem_k, mem_v, layer), W['xa_w_o'], layer, res=x)
        else:
            x = _xattn_block(x.reshape(bsz, length, D_MODEL), a, mem_k, mem_v, layer, W['g_xattn'], w_out, j,
                             W['xa_w_q'], W['xa_w_o']).reshape(bsz * length, D_MODEL)
        if layer == DEPTH - 1:
            x, y = _ffn(x, W['g_ffn2'], W['ffn2_w_gate'], W['ffn2_w_up'], W['ffn2_w_down'], layer,
                        g_final=W['g_final'])
        else:
            x = _ffn(x, W['g_ffn2'], W['ffn2_w_gate'], W['ffn2_w_up'], W['ffn2_w_down'], layer)
    return (y.reshape(bsz, length, D_MODEL), c_all, jnp.stack(new_n), jnp.stack(new_m), s_all, jnp.stack(new_conv))


def kernel(x_prompt, x_sample, mem_prompt, cache_mem_k, cache_mem_v, state_mlstm_C, state_mlstm_n, state_mlstm_m, state_gdn_S, state_gdn_conv, g_ffn1, ffn1_w_gate, ffn1_w_up, ffn1_w_down, g_mix, ml_w_in, ml_b_i, ml_b_f, ml_g_head, ml_w_out, gd_w_in, gd_conv_w, gd_a_log, gd_dt_bias, gd_g_out, gd_w_out, g_xattn, g_mem, xa_w_q, xa_w_k, xa_w_v, xa_w_o, g_ffn2, ffn2_w_gate, ffn2_w_up, ffn2_w_down, g_final):
    P = dict(g_ffn1=g_ffn1, ffn1_w_gate=ffn1_w_gate, ffn1_w_up=ffn1_w_up, ffn1_w_down=ffn1_w_down, g_mix=g_mix,
             ml_w_in=ml_w_in, ml_b_i=ml_b_i, ml_b_f=ml_b_f, ml_g_head=ml_g_head, ml_w_out=ml_w_out,
             gd_w_in=gd_w_in, gd_conv_w=gd_conv_w, gd_a_log=gd_a_log, gd_dt_bias=gd_dt_bias, gd_g_out=gd_g_out,
             gd_w_out=gd_w_out, g_xattn=g_xattn, g_mem=g_mem, xa_w_q=xa_w_q, xa_w_k=xa_w_k, xa_w_v=xa_w_v,
             xa_w_o=xa_w_o, g_ffn2=g_ffn2, ffn2_w_gate=ffn2_w_gate, ffn2_w_up=ffn2_w_up, ffn2_w_down=ffn2_w_down)
    W = _prep_params(P)
    W['g_final'] = g_final
    batch, n_mem, _ = mem_prompt.shape
    n_ml, n_gd = state_mlstm_C.shape[0], state_gdn_S.shape[0]

    pk, pv, pkb, pvb = _memkv(mem_prompt.reshape(batch * n_mem, D_MODEL), W['g_mem'], W['xa_w_k'], W['xa_w_v'])
    z_c = jnp.zeros((n_ml, batch, ML_HEADS, ML_DQK, ML_DV), F32)
    z_n = jnp.zeros((n_ml, batch, ML_HEADS, ML_DQK), F32)
    z_m = jnp.zeros((n_ml, batch, ML_HEADS), F32)
    z_s = jnp.zeros((n_gd, batch, GD_HEADS, GD_DK, GD_DV), F32)
    z_conv = jnp.zeros((n_gd, batch, GD_CONV - 1, GD_QKV), F32)
    y_p, p_c, p_n, p_m, p_s, p_conv = _trunk(
        x_prompt, pkb.reshape(DEPTH, batch, n_mem, D_MODEL), pvb.reshape(DEPTH, batch, n_mem, D_MODEL),
        z_c, z_n, z_m, z_s, z_conv, W)

    y_s, s_c, s_n, s_m, s_s, s_conv = _trunk(
        x_sample, cache_mem_k, cache_mem_v,
        state_mlstm_C, state_mlstm_n, state_mlstm_m, state_gdn_S, state_gdn_conv, W)

    kv_shape = (DEPTH, batch, n_mem, XA_HEADS, XA_DH)
    return (y_p, y_s, pk.reshape(kv_shape), pv.reshape(kv_shape), p_c, p_n, p_m, p_s, p_conv,
            s_c, s_n, s_m, s_s, s_conv)
```

```python
import functools

import jax
import jax.numpy as jnp
from jax import lax
from jax.experimental import pallas as pl
from jax.experimental.pallas import tpu as pltpu

F32 = jnp.float32
BF16 = jnp.bfloat16

D_MODEL = 1024
DEPTH = 4
N_MEM = 256
D_FF = 2816
FFN_RES = 0.5
EPS = 1e-6
ML_HEADS = 4
ML_DV = 256
ML_DQK = 128
ML_MAIN = 2 * ML_HEADS * ML_DQK + 2 * D_MODEL
GD_HEADS = 8
GD_DK = 128
GD_DV = 128
GD_CONV = 4
GD_QKV = 3072
GD_MAIN = GD_QKV + D_MODEL
XA_HEADS = 4
XA_DH = 256
ML_CHUNK = 256
GD_CHUNK = 128
GD_CHUNK_ROWS = 2
ML_CHUNK_ROWS = 1
LANES = 128
SUBLANES = 8
GATE_COLS = 2 * LANES
VMEM_LIMIT_BYTES = 56 * 1024 * 1024
HI = lax.Precision.HIGHEST


def _params():
    return pltpu.CompilerParams(vmem_limit_bytes=VMEM_LIMIT_BYTES)


def _rms(x, g):
    return x * lax.rsqrt(jnp.mean(x * x, axis=-1, keepdims=True) + EPS) * g


def _silu(x):
    return x * jax.nn.sigmoid(x)


def _softplus(x):
    return jnp.maximum(x, 0.0) + jnp.log1p(jnp.exp(-jnp.abs(x)))


def _log_sigmoid(x):
    return jnp.minimum(x, 0.0) - jnp.log1p(jnp.exp(-jnp.abs(x)))


def _dot(a, b):
    return jnp.dot(a, b, preferred_element_type=F32)


def _dot_nt(a, b):
    return lax.dot_general(a, b, (((1,), (1,)), ((), ())), preferred_element_type=F32)


def _dot_tn(a, b):
    return lax.dot_general(a, b, (((0,), (0,)), ((), ())), preferred_element_type=F32)


def _resident(shape):
    nd = len(shape)
    return pl.BlockSpec(shape, lambda *_: (0,) * nd, pipeline_mode=pl.Buffered(1))


def _layer_block(arr, index):
    nd = arr.ndim - 1
    return pl.BlockSpec((None,) + arr.shape[1:], lambda *_: (index,) + (0,) * nd, pipeline_mode=pl.Buffered(1))


def _ffn_kernel(x_ref, g_ref, wg_ref, wu_ref, wd_ref, *rest, fchunk, final):
    if final:
        gf_ref, o_ref, y_ref, h_ref, a_ref = rest
    else:
        o_ref, h_ref, a_ref = rest
    h_ref[...] = _rms(x_ref[...], g_ref[...]).astype(BF16)
    for j in range(D_FF // fchunk):
        sl = slice(j * fchunk, (j + 1) * fchunk)
        h = h_ref[...]
        gate = _dot(h, wg_ref[:, sl])
        up = _dot(h, wu_ref[:, sl])
        a_ref[:, sl] = (_silu(gate) * up).astype(BF16)
    out = x_ref[...] + FFN_RES * _dot(a_ref[...], wd_ref[...])
    o_ref[...] = out
    if final:
        y_ref[...] = _rms(out, gf_ref[...])


def _ffn(x, g, wg, wu, wd, layer, g_final=None):
    m = x.shape[0]
    tm = min(m, 512)
    final = g_final is not None
    row = pl.BlockSpec((tm, D_MODEL), lambda i: (i, 0))
    in_specs = [row, _layer_block(g, layer), _layer_block(wg, layer), _layer_block(wu, layer), _layer_block(wd, layer)]
    args = [x, g, wg, wu, wd]
    out_shape = jax.ShapeDtypeStruct((m, D_MODEL), F32)
    out_specs = row
    if final:
        in_specs.append(_resident((1, D_MODEL)))
        args.append(g_final.reshape(1, D_MODEL))
        out_shape = (out_shape, out_shape)
        out_specs = (row, row)
    return pl.pallas_call(
        functools.partial(_ffn_kernel, fchunk=256, final=final),
        out_shape=out_shape, grid=(m // tm,), in_specs=in_specs, out_specs=out_specs,
        scratch_shapes=[pltpu.VMEM((tm, D_MODEL), BF16), pltpu.VMEM((tm, D_FF), BF16)],
        compiler_params=_params(), name="ffn_final" if final else "ffn",
    )(*args)


def _matmul_chunks(h, w_ref, o_ref, nchunk, res_ref=None):
    n = w_ref.shape[1]
    for n0 in range(0, n, nchunk):
        n1 = min(n, n0 + nchunk)
        y = _dot(h, w_ref[:, n0:n1])
        if res_ref is not None:
            y = y + res_ref[:, n0:n1]
        o_ref[:, n0:n1] = y.astype(o_ref.dtype)


def _proj_kernel(*refs, norm, res, nchunk):
    refs = list(refs)
    x_ref = refs.pop(0)
    g_ref = refs.pop(0) if norm else None
    w_ref = refs.pop(0)
    r_ref = refs.pop(0) if res else None
    o_ref = refs.pop(0)
    if norm:
        h = _rms(x_ref[...], g_ref[...]).astype(BF16)
    else:
        h = x_ref[...].astype(BF16)
    _matmul_chunks(h, w_ref, o_ref, nchunk, r_ref)


def _proj(x, w, w_index, g=None, g_index=None, res=None, out_dtype=F32):
    m, k = x.shape
    n = w.shape[-1]
    tm = min(m, 512)
    in_specs = [pl.BlockSpec((tm, k), lambda i: (i, 0))]
    args = [x]
    if g is not None:
        in_specs.append(_layer_block(g, g_index))
        args.append(g)
    in_specs.append(_layer_block(w, w_index))
    args.append(w)
    if res is not None:
        in_specs.append(pl.BlockSpec((tm, n), lambda i: (i, 0)))
        args.append(res)
    return pl.pallas_call(
        functools.partial(_proj_kernel, norm=g is not None, res=res is not None, nchunk=512),
        out_shape=jax.ShapeDtypeStruct((m, n), out_dtype), grid=(m // tm,), in_specs=in_specs,
        out_specs=pl.BlockSpec((tm, n), lambda i: (i, 0)),
        compiler_params=_params(), name="proj",
    )(*args)


def _inproj_kernel(x_ref, g_ref, wm_ref, wg_ref, om_ref, og_ref, *, nchunk):
    h = _rms(x_ref[...], g_ref[...]).astype(BF16)
    _matmul_chunks(h, wm_ref, om_ref, nchunk)
    og_ref[...] = _dot(h, wg_ref[...])


def _inproj(x, g, g_index, w_main, w_gate, w_index, main_dtype):
    m = x.shape[0]
    n = w_main.shape[-1]
    tm = min(m, 512)
    rows = lambda width: pl.BlockSpec((tm, width), lambda i: (i, 0))
    return pl.pallas_call(
        functools.partial(_inproj_kernel, nchunk=512),
        out_shape=(jax.ShapeDtypeStruct((m, n), main_dtype), jax.ShapeDtypeStruct((m, GATE_COLS), F32)),
        grid=(m // tm,),
        in_specs=[rows(D_MODEL), _layer_block(g, g_index), _layer_block(w_main, w_index), _layer_block(w_gate, w_index)],
        out_specs=(rows(n), rows(GATE_COLS)),
        compiler_params=_params(), name="inproj",
    )(x, g, w_main, w_gate)


def _memkv_kernel(x_ref, g_ref, wk_ref, wv_ref, k_ref, v_ref, kb_ref, vb_ref):
    h = _rms(x_ref[...], g_ref[...]).astype(BF16)
    k = _dot(h, wk_ref[...])
    v = _dot(h, wv_ref[...])
    kb_ref[0] = k.astype(BF16)
    vb_ref[0] = v.astype(BF16)
    for i in range(XA_HEADS):
        sl = slice(i * XA_DH, (i + 1) * XA_DH)
        k_ref[0, :, i, :] = k[:, sl]
        v_ref[0, :, i, :] = v[:, sl]


def _memkv(mem, g_mem, wk, wv):
    m = mem.shape[0]
    tm = 512
    w_spec = pl.BlockSpec((None, D_MODEL, D_MODEL), lambda l, i: (l, 0, 0))
    o_spec = pl.BlockSpec((1, tm, XA_HEADS, XA_DH), lambda l, i: (l, i, 0, 0))
    b_spec = pl.BlockSpec((1, tm, D_MODEL), lambda l, i: (l, i, 0))
    o_shape = jax.ShapeDtypeStruct((DEPTH, m, XA_HEADS, XA_DH), F32)
    b_shape = jax.ShapeDtypeStruct((DEPTH, m, D_MODEL), BF16)
    return pl.pallas_call(
        _memkv_kernel, out_shape=(o_shape, o_shape, b_shape, b_shape), grid=(DEPTH, m // tm),
        in_specs=[pl.BlockSpec((tm, D_MODEL), lambda l, i: (i, 0)),
                  pl.BlockSpec((None, 1, D_MODEL), lambda l, i: (l, 0, 0)), w_spec, w_spec],
        out_specs=(o_spec, o_spec, b_spec, b_spec), compiler_params=_params(), name="memkv",
    )(mem, g_mem, wk, wv)


def _xattn_kernel(x_ref, a_ref, k_ref, v_ref, g_ref, wout_ref, wq_ref, wo_ref, o_ref, att_ref):
    x2 = x_ref[0] + _dot(a_ref[0], wout_ref[...])
    q = _dot(_rms(x2, g_ref[...]).astype(BF16), wq_ref[...]).astype(BF16)
    for h in range(XA_HEADS):
        sl = slice(h * XA_DH, (h + 1) * XA_DH)
        s = _dot_nt(q[:, sl], k_ref[0, :, sl]) * XA_DH ** -0.5
        e = jnp.exp(s - jnp.max(s, axis=-1, keepdims=True))
        p = e / jnp.sum(e, axis=-1, keepdims=True)
        att_ref[:, sl] = _dot(p.astype(BF16), v_ref[0, :, sl]).astype(BF16)
    o_ref[0] = x2 + _dot(att_ref[...], wo_ref[...])


def _xattn_block(x, a, mem_kb, mem_vb, layer, g, w_out, w_out_index, wq, wo):
    b, l, _ = x.shape
    tq = min(l, 512)
    row = pl.BlockSpec((1, tq, D_MODEL), lambda i, j: (i, j, 0))
    kv_spec = pl.BlockSpec((None, 1, N_MEM, D_MODEL), lambda i, j: (layer, i, 0, 0))
    return pl.pallas_call(
        _xattn_kernel, out_shape=jax.ShapeDtypeStruct((b, l, D_MODEL), F32), grid=(b, l // tq),
        in_specs=[row, row, kv_spec, kv_spec, _layer_block(g, layer), _layer_block(w_out, w_out_index),
                  _layer_block(wq, layer), _layer_block(wo, layer)],
        out_specs=row, scratch_shapes=[pltpu.VMEM((tq, D_MODEL), BF16)],
        compiler_params=_params(), name="xattn",
    )(x, a, mem_kb, mem_vb, g, w_out, wq, wo)


ATTN_STEP_BATCH = 8


def _attn_step_kernel(q_ref, k_ref, v_ref, o_ref):
    for i in range(ATTN_STEP_BATCH):
        q = q_ref[0, i] * XA_DH ** -0.5
        s = jnp.sum(k_ref[0, i] * q[None], axis=-1, keepdims=True)
        e = jnp.exp(s - jnp.max(s, axis=0, keepdims=True))
        acc = jnp.sum(e * v_ref[0, i], axis=0)
        o_ref[0, i] = acc / jnp.sum(e, axis=0)


def _attn_step(q, mem_k, mem_v, layer):
    b = q.shape[0]
    bb = ATTN_STEP_BATCH
    kv_spec = pl.BlockSpec((1, bb, N_MEM, XA_HEADS, XA_DH), lambda i: (layer, i, 0, 0, 0))
    q_spec = pl.BlockSpec((1, bb, XA_HEADS, XA_DH), lambda i: (i, 0, 0, 0))
    out = pl.pallas_call(
        _attn_step_kernel, out_shape=jax.ShapeDtypeStruct((b // bb, bb, XA_HEADS, XA_DH), F32), grid=(b // bb,),
        in_specs=[q_spec, kv_spec, kv_spec], out_specs=q_spec,
        compiler_params=_params(), name="attn_step",
    )(q.reshape(b // bb, bb, XA_HEADS, XA_DH), mem_k, mem_v)
    return out.reshape(b, D_MODEL)


def _zero_other_layers(ref):
    ref[1:] = jnp.zeros((ref.shape[0] - 1,) + ref.shape[1:], F32)


def _state_out(n_layers, layer_j, bsz, bb, tail, idx, prev):
    shape = jax.ShapeDtypeStruct((n_layers, bsz) + tail, F32)
    zeros = (0,) * len(tail)
    if prev is None:
        spec = pl.BlockSpec((n_layers, bb) + tail, lambda *g: (0, idx(*g)) + zeros)
        return shape, spec, [], [], True
    spec = pl.BlockSpec((1, bb) + tail, lambda *g: (layer_j, idx(*g)) + zeros)
    return shape, spec, [pl.BlockSpec(memory_space=pl.ANY)], [prev], False


def _lane_select(h, value, into):
    lane = lax.broadcasted_iota(jnp.int32, into.shape, 1)
    return jnp.where(lane == h, value, into)


def _mlstm_chunk_kernel(p_ref, gt_ref, c0_ref, n0_ref, m0_ref, bi_ref, bf_ref, gh_ref, *rest, c, wide, aliased):
    if aliased:
        rest = rest[1:]
    o_ref, c_ref, n_ref, m_ref = rest
    rows = range(ML_CHUNK_ROWS)

    @pl.when(pl.program_id(1) == 0)
    def _():
        for r in rows:
            c_ref[0, r] = c0_ref[0, r]
            n_ref[r] = n0_ref[0, r]
            m_ref[r] = m0_ref[0, r]
        if wide:
            _zero_other_layers(c_ref)

    hq = ML_HEADS * ML_DQK
    row = lax.broadcasted_iota(jnp.int32, (c, c), 0)
    col = lax.broadcasted_iota(jnp.int32, (c, c), 1)
    incl = row >= col
    scale = ML_DQK ** -0.5
    i_pres, bs, xts, m_prevs = [], [], [], []
    for r in rows:
        i_pre = gt_ref[r, :, 0:LANES] + bi_ref[...]
        log_f = _log_sigmoid(gt_ref[r, :, LANES:GATE_COLS] + bf_ref[...])
        b = jnp.dot(incl.astype(F32), log_f, precision=HI, preferred_element_type=F32)
        i_pres.append(i_pre)
        bs.append(b)
        xts.append((i_pre - b).T)
        m_prevs.append(m_ref[r])
    probs = [(r, h) for r in rows for h in range(ML_HEADS)]
    n = range(len(probs))
    bcols = [bs[r][:, h:h + 1] for r, h in probs]
    icols = [i_pres[r][:, h:h + 1] for r, h in probs]
    mprev = [m_prevs[r][:, h:h + 1] for r, h in probs]
    qbs = [p_ref[r, :, h * ML_DQK:(h + 1) * ML_DQK] for r, h in probs]
    kbs = [p_ref[r, :, hq + h * ML_DQK:hq + (h + 1) * ML_DQK] for r, h in probs]
    vbs = [p_ref[r, :, 2 * hq + h * ML_DV:2 * hq + (h + 1) * ML_DV] for r, h in probs]
    dms = [jnp.where(incl, bcols[i] + xts[r][h:h + 1, :], -jnp.inf) for i, (r, h) in enumerate(probs)]
    mts = [jnp.maximum(bcols[i] + mprev[i], jnp.max(dms[i], axis=-1, keepdims=True)) for i in n]
    w_inters = [jnp.exp(bcols[i] + mprev[i] - mts[i]) for i in n]
    ss = [_dot_nt(qbs[i], kbs[i]) * (jnp.exp(dms[i] - mts[i]) * scale) for i in n]
    c_old = [c_ref[0, r, h] for r, h in probs]
    n_old = [n_ref[r, h:h + 1, :] for r, h in probs]
    nums = [w_inters[i] * _dot(qbs[i], c_old[i].astype(BF16)) + _dot(ss[i].astype(BF16), vbs[i]) for i in n]
    m_rows = list(m_prevs)
    for i, (r, h) in enumerate(probs):
        m_new = mts[i][c - 1:c, :]
        b_last = bcols[i][c - 1:c, :]
        w_k = jnp.exp(b_last - bcols[i] + icols[i] - m_new) * scale
        decay = jnp.exp(b_last + mprev[i] - m_new)
        kw = kbs[i].astype(F32) * w_k
        c_ref[0, r, h] = decay * c_old[i] + _dot_tn(kw.astype(BF16), vbs[i])
        n_ref[r, h:h + 1, :] = decay * n_old[i] + jnp.sum(kw, axis=0, keepdims=True)
        m_rows[r] = _lane_select(h, m_new, m_rows[r])
    for r in rows:
        m_ref[r] = m_rows[r]
    for i, (r, h) in enumerate(probs):
        den = (w_inters[i] * jnp.sum(qbs[i].astype(F32) * n_old[i], axis=-1, keepdims=True)
               + jnp.sum(ss[i], axis=-1, keepdims=True))
        hh = nums[i] / jnp.maximum(jnp.abs(den), jnp.exp(-mts[i]))
        vs = slice(h * ML_DV, (h + 1) * ML_DV)
        hn = hh * lax.rsqrt(jnp.mean(hh * hh, axis=-1, keepdims=True) + EPS) * gh_ref[:, vs]
        og = p_ref[r, :, 2 * hq + D_MODEL + h * ML_DV:2 * hq + D_MODEL + (h + 1) * ML_DV].astype(F32)
        o_ref[r, :, vs] = (hn * jax.nn.sigmoid(og)).astype(o_ref.dtype)


def _mlstm_small_specs(bsz, layer_j, idx, bb):
    c_in = pl.BlockSpec((1, bb, ML_HEADS, ML_DQK, ML_DV), lambda *g: (layer_j, idx(*g), 0, 0, 0))
    n_in = pl.BlockSpec((1, bb, ML_HEADS, ML_DQK), lambda *g: (layer_j, idx(*g), 0, 0))
    m_in = pl.BlockSpec((1, bb, 1, LANES), lambda *g: (layer_j, idx(*g), 0, 0))
    n_out = pl.BlockSpec((bb, ML_HEADS, ML_DQK), lambda *g: (idx(*g), 0, 0))
    m_out = pl.BlockSpec((bb, 1, LANES), lambda *g: (idx(*g), 0, 0))
    shapes = (jax.ShapeDtypeStruct((bsz, ML_HEADS, ML_DQK), F32), jax.ShapeDtypeStruct((bsz, 1, LANES), F32))
    return (c_in, n_in, m_in), (n_out, m_out), shapes


def _mlstm_call(kernel_fn, name, p, gates, c0, n0, m0, layer_j, b_i, b_f, g_head, c_prev, grid, row_block, idx, bb):
    bsz = p.shape[0]
    n_layers = c0.shape[0]
    s_in, s_out, s_shapes = _mlstm_small_specs(bsz, layer_j, idx, bb)
    c_shape, c_spec, extra_specs, extra_args, wide = _state_out(
        n_layers, layer_j, bsz, bb, (ML_HEADS, ML_DQK, ML_DV), idx, c_prev)
    n_in = 8
    return pl.pallas_call(
        functools.partial(kernel_fn, wide=wide, aliased=not wide),
        out_shape=(jax.ShapeDtypeStruct(p.shape[:2] + (D_MODEL,), BF16), c_shape) + s_shapes,
        grid=grid,
        in_specs=[row_block(ML_MAIN), row_block(GATE_COLS), *s_in, _layer_block(b_i, layer_j),
                  _layer_block(b_f, layer_j), _layer_block(g_head, layer_j), *extra_specs],
        out_specs=(row_block(D_MODEL), c_spec) + s_out,
        input_output_aliases={} if wide else {n_in: 1},
        compiler_params=_params(), name=name,
    )(p, gates, c0, n0, m0, b_i, b_f, g_head, *extra_args)


def _mlstm_chunked(p, gates, c0, n0, m0, layer_j, b_i, b_f, g_head, c_prev):
    bsz, length, _ = p.shape
    bb = ML_CHUNK_ROWS
    row_block = lambda width: pl.BlockSpec((bb, ML_CHUNK, width), lambda i, j: (i, j, 0))
    return _mlstm_call(functools.partial(_mlstm_chunk_kernel, c=ML_CHUNK), "mlstm_chunk", p, gates, c0, n0, m0, layer_j,
                       b_i, b_f, g_head, c_prev, (bsz // bb, length // ML_CHUNK), row_block, lambda i, j: i, bb)


STEP_BATCH = 4


def _row0(x, rows=SUBLANES):
    r = lax.broadcasted_iota(jnp.int32, (rows, x.shape[1]), 0)
    return jnp.where(r == 0, x, 0.0)


def _mlstm_step_kernel(p_ref, gt_ref, c0_ref, n0_ref, m0_ref, bi_ref, bf_ref, gh_ref, *rest, wide, aliased):
    if aliased:
        rest = rest[1:]
    o_ref, c_ref, n_ref, m_ref = rest
    hq = ML_HEADS * ML_DQK
    if wide:
        _zero_other_layers(c_ref)
    w_inters, w_ks, e_invs = [], [], []
    for i in range(STEP_BATCH):
        i_pre = gt_ref[i, :, 0:LANES] + bi_ref[...]
        log_f = _log_sigmoid(gt_ref[i, :, LANES:GATE_COLS] + bf_ref[...])
        m_prev = m0_ref[0, i]
        mt = jnp.maximum(log_f + m_prev, i_pre)
        w_inters.append(jnp.exp(log_f + m_prev - mt))
        w_ks.append(jnp.exp(i_pre - mt))
        e_invs.append(jnp.exp(-mt))
        m_ref[i] = mt
    probs = [(i, h) for i in range(STEP_BATCH) for h in range(ML_HEADS)]
    n = range(len(probs))
    qs = [p_ref[i, :, h * ML_DQK:(h + 1) * ML_DQK].astype(F32) for i, h in probs]
    ks = [p_ref[i, :, hq + h * ML_DQK:hq + (h + 1) * ML_DQK].astype(F32) * ML_DQK ** -0.5 for i, h in probs]
    vs = [p_ref[i, :, 2 * hq + h * ML_DV:2 * hq + (h + 1) * ML_DV].astype(F32) for i, h in probs]
    c_old = [c0_ref[0, i, h] for i, h in probs]
    n_old = [n0_ref[0, i, h:h + 1, :] for i, h in probs]
    wis = [w_inters[i][:, h:h + 1] for i, h in probs]
    wks = [w_ks[i][:, h:h + 1] for i, h in probs]
    q_cs = [_dot(_row0(qs[j]).astype(BF16), c_old[j].astype(BF16))[0:1] for j in n]
    kws = [ks[j] * wks[j] for j in n]
    for j, (i, h) in enumerate(probs):
        v8 = jnp.broadcast_to(vs[j], (SUBLANES, ML_DV)).astype(BF16)
        c_ref[0, i, h] = wis[j] * c_old[j] + _dot_tn(_row0(kws[j]).astype(BF16), v8)
        n_ref[i, h:h + 1, :] = wis[j] * n_old[j] + kws[j]
    stack = lambda xs: jnp.concatenate(xs, axis=0)
    sls = [slice(h * ML_DV, (h + 1) * ML_DV) for _, h in probs]
    q_all, k_all, v_all, wi_all, wk_all = stack(qs), stack(ks), stack(vs), stack(wis), stack(wks)
    s = jnp.sum(q_all * k_all, axis=-1, keepdims=True) * wk_all
    num = wi_all * stack(q_cs) + s * v_all
    den = wi_all * jnp.sum(q_all * stack(n_old), axis=-1, keepdims=True) + s
    hh = num / jnp.maximum(jnp.abs(den), stack([e_invs[i][:, h:h + 1] for i, h in probs]))
    hn = hh * lax.rsqrt(jnp.mean(hh * hh, axis=-1, keepdims=True) + EPS) * stack([gh_ref[:, sl] for sl in sls])
    og = stack([p_ref[i, :, 2 * hq + D_MODEL + h * ML_DV:2 * hq + D_MODEL + (h + 1) * ML_DV] for i, h in probs])
    out = (hn * jax.nn.sigmoid(og.astype(F32))).astype(o_ref.dtype)
    for j, (i, h) in enumerate(probs):
        o_ref[i, :, sls[j]] = out[j:j + 1]


def _mlstm_step(p, gates, c0, n0, m0, layer_j, b_i, b_f, g_head, c_prev):
    bsz = p.shape[0]
    bb = STEP_BATCH
    row_block = lambda width: pl.BlockSpec((bb, 1, width), lambda i: (i, 0, 0))
    return _mlstm_call(_mlstm_step_kernel, "mlstm_step", p, gates, c0, n0, m0, layer_j,
                       b_i, b_f, g_head, c_prev, (bsz // bb,), row_block, lambda i: i, bb)


def _l2n(x):
    return x * lax.rsqrt(jnp.sum(x * x, axis=-1, keepdims=True) + EPS)


INV_BASE = 16


def _unit_lower_inverses(a_list, c):
    row = lax.broadcasted_iota(jnp.int32, (c, c), 0)
    col = lax.broadcasted_iota(jnp.int32, (c, c), 1)
    eye = jnp.where(row == col, 1.0, 0.0)
    same = lambda size: (row >> (size.bit_length() - 1)) == (col >> (size.bit_length() - 1))
    ns = [jnp.where(same(INV_BASE), -a, 0.0) for a in a_list]
    ts = [eye + n for n in ns]
    nbs = [n.astype(BF16) for n in ns]
    power = 2
    while power < INV_BASE:
        ns = [_dot(nb, nb) for nb in nbs]
        nbs = [n.astype(BF16) for n in ns]
        ts = [t + _dot(t.astype(BF16), nb) for t, nb in zip(ts, nbs)]
        power *= 2
    size = INV_BASE
    while size < c:
        off = same(2 * size) & jnp.logical_not(same(size))
        tbs = [t.astype(BF16) for t in ts]
        mids = [_dot(tb, jnp.where(off, a, 0.0).astype(BF16)).astype(BF16) for tb, a in zip(tbs, a_list)]
        ts = [t - _dot(mid, tb) for t, mid, tb in zip(ts, mids, tbs)]
        size *= 2
    return ts


def _gdn_gates(gt_ref, i, alog_ref, dtb_ref):
    beta = jax.nn.sigmoid(gt_ref[i, :, 0:LANES])
    log_g = -jnp.exp(alog_ref[...]) * _softplus(gt_ref[i, :, LANES:GATE_COLS] + dtb_ref[...])
    return beta, log_g


def _gdn_out(o, z, gout):
    return (o * lax.rsqrt(jnp.mean(o * o, axis=-1, keepdims=True) + EPS) * gout * _silu(z))


def _gdn_chunk_kernel(p_ref, gt_ref, cp_ref, cw_ref, alog_ref, dtb_ref, gout_ref, s0_ref, *rest, c, wide, aliased):
    if aliased:
        rest = rest[1:]
    o_ref, s_ref, nc_ref, e_ref = rest
    pad = SUBLANES
    keep = GD_CONV - 1
    rows = range(GD_CHUNK_ROWS)

    @pl.when(pl.program_id(1) == 0)
    def _():
        for r in rows:
            s_ref[0, r] = s0_ref[0, r]
            e_ref[r, pl.ds(pad - keep, keep), :] = cp_ref[0, r]
        if wide:
            _zero_other_layers(s_ref)

    row = lax.broadcasted_iota(jnp.int32, (c, c), 0)
    col = lax.broadcasted_iota(jnp.int32, (c, c), 1)
    incl = row >= col
    strict = row > col
    hk = GD_HEADS * GD_DK
    convs, betas, gams, gam_ts, e_gams = [], [], [], [], []
    for r in rows:
        e_ref[r, pl.ds(pad, c), :] = p_ref[r, :, 0:GD_QKV]
        conv = e_ref[r, pl.ds(pad - keep, c), :] * cw_ref[0:1, :]
        for j in range(1, GD_CONV):
            conv = conv + e_ref[r, pl.ds(pad - keep + j, c), :] * cw_ref[j:j + 1, :]
        convs.append(_silu(conv))
        nc_ref[r] = e_ref[r, pl.ds(pad + c - keep, keep), :]
        e_ref[r, pl.ds(0, pad), :] = e_ref[r, pl.ds(c, pad), :]
        beta, log_g = _gdn_gates(gt_ref, r, alog_ref, dtb_ref)
        gam = jnp.dot(incl.astype(F32), log_g, precision=HI, preferred_element_type=F32)
        betas.append(beta)
        gams.append(gam)
        gam_ts.append(gam.T)
        e_gams.append(jnp.exp(gam))
    probs = [(r, h) for r in rows for h in range(GD_HEADS)]
    col_of = lambda xs, r, h: xs[r][:, h:h + 1]
    qs = [_l2n(convs[r][:, h * GD_DK:(h + 1) * GD_DK]) * GD_DK ** -0.5 for r, h in probs]
    ks = [_l2n(convs[r][:, hk + h * GD_DK:hk + (h + 1) * GD_DK]) for r, h in probs]
    vs = [convs[r][:, 2 * hk + h * GD_DV:2 * hk + (h + 1) * GD_DV] for r, h in probs]
    bcols = [col_of(betas, r, h) for r, h in probs]
    gcols = [col_of(gams, r, h) for r, h in probs]
    ecols = [col_of(e_gams, r, h) for r, h in probs]
    n = range(len(probs))
    kbs = [k.astype(BF16) for k in ks]
    decs = [jnp.exp(jnp.where(incl, gcols[i] - gam_ts[r][h:h + 1, :], -jnp.inf)) for i, (r, h) in enumerate(probs)]
    kqs = [_dot_nt(jnp.concatenate([kbs[i], qs[i].astype(BF16)], axis=0), kbs[i]) for i in n]
    a_list = [jnp.where(strict, bcols[i] * kqs[i][0:c] * decs[i], 0.0) for i in n]
    ts = _unit_lower_inverses(a_list, c)
    rhs = [jnp.concatenate([bcols[i] * vs[i], (bcols[i] * ecols[i]) * ks[i]], axis=1).astype(BF16) for i in n]
    uws = [_dot(ts[i].astype(BF16), rhs[i]) for i in n]
    s_old = [s_ref[0, r, h] for r, h in probs]
    wq = [jnp.concatenate([uws[i][:, GD_DV:], qs[i] * ecols[i]], axis=0).astype(BF16) for i in n]
    wqs = [_dot(wq[i], s_old[i].astype(BF16)) for i in n]
    ubs = [(uws[i][:, :GD_DV] - wqs[i][0:c]).astype(BF16) for i in n]
    for i, (r, h) in enumerate(probs):
        g_last = gcols[i][c - 1:c, :]
        k_dec = ks[i] * jnp.exp(g_last - gcols[i])
        s_ref[0, r, h] = jnp.exp(g_last) * s_old[i] + _dot_tn(k_dec.astype(BF16), ubs[i])
    for i, (r, h) in enumerate(probs):
        o = wqs[i][c:2 * c] + _dot((kqs[i][c:2 * c] * decs[i]).astype(BF16), ubs[i])
        z = p_ref[r, :, GD_QKV + h * GD_DV:GD_QKV + (h + 1) * GD_DV]
        o_ref[r, :, h * GD_DV:(h + 1) * GD_DV] = _gdn_out(o, z, gout_ref[...]).astype(o_ref.dtype)


def _gdn_call(kernel_fn, name, p, gates, conv_prev, s0, layer_j, conv_w, a_log, dt_bias, g_out, s_prev,
              grid, row_block, idx, bb, scratch):
    bsz = p.shape[0]
    n_layers = s0.shape[0]
    cp_in = pl.BlockSpec((1, bb, GD_CONV - 1, GD_QKV), lambda *g: (layer_j, idx(*g), 0, 0))
    s_in = pl.BlockSpec((1, bb, GD_HEADS, GD_DK, GD_DV), lambda *g: (layer_j, idx(*g), 0, 0, 0))
    nc_out = pl.BlockSpec((bb, GD_CONV - 1, GD_QKV), lambda *g: (idx(*g), 0, 0))
    s_shape, s_spec, extra_specs, extra_args, wide = _state_out(
        n_layers, layer_j, bsz, bb, (GD_HEADS, GD_DK, GD_DV), idx, s_prev)
    n_in = 8
    return pl.pallas_call(
        functools.partial(kernel_fn, wide=wide, aliased=not wide),
        out_shape=(jax.ShapeDtypeStruct(p.shape[:2] + (D_MODEL,), BF16), s_shape,
                   jax.ShapeDtypeStruct((bsz, GD_CONV - 1, GD_QKV), F32)),
        grid=grid,
        in_specs=[row_block(GD_MAIN), row_block(GATE_COLS), cp_in, _layer_block(conv_w, layer_j),
                  _layer_block(a_log, layer_j), _layer_block(dt_bias, layer_j), _layer_block(g_out, layer_j), s_in,
                  *extra_specs],
        out_specs=(row_block(D_MODEL), s_spec, nc_out),
        input_output_aliases={} if wide else {n_in: 1},
        scratch_shapes=scratch, compiler_params=_params(), name=name,
    )(p, gates, conv_prev, conv_w, a_log, dt_bias, g_out, s0, *extra_args)


def _gdn_chunked(p, gates, conv_prev, s0, layer_j, conv_w, a_log, dt_bias, g_out, s_prev):
    bsz, length, _ = p.shape
    bb = GD_CHUNK_ROWS
    row_block = lambda width: pl.BlockSpec((bb, GD_CHUNK, width), lambda i, j: (i, j, 0))
    return _gdn_call(functools.partial(_gdn_chunk_kernel, c=GD_CHUNK), "gdn_chunk", p, gates, conv_prev, s0, layer_j,
                     conv_w, a_log, dt_bias, g_out, s_prev, (bsz // bb, length // GD_CHUNK), row_block, lambda i, j: i, bb,
                     [pltpu.VMEM((bb, GD_CHUNK + SUBLANES, GD_QKV), F32)])


def _gdn_step_kernel(p_ref, gt_ref, cp_ref, cw_ref, alog_ref, dtb_ref, gout_ref, s0_ref, *rest, wide, aliased):
    if aliased:
        rest = rest[1:]
    o_ref, s_ref, nc_ref = rest
    hk = GD_HEADS * GD_DK
    rows = lax.broadcasted_iota(jnp.int32, (SUBLANES, GD_DK), 0)
    if wide:
        _zero_other_layers(s_ref)
    convs, betas, gs = [], [], []
    for i in range(STEP_BATCH):
        qkv = p_ref[i, :, 0:GD_QKV]
        prev = cp_ref[0, i]
        conv = qkv * cw_ref[GD_CONV - 1:GD_CONV, :]
        for j in range(GD_CONV - 1):
            conv = conv + prev[j:j + 1, :] * cw_ref[j:j + 1, :]
        convs.append(_silu(conv))
        nc_ref[i] = jnp.concatenate([prev[1:GD_CONV - 1], qkv], axis=0)
        beta, log_g = _gdn_gates(gt_ref, i, alog_ref, dtb_ref)
        betas.append(beta)
        gs.append(jnp.exp(log_g))
    probs = [(i, h) for i in range(STEP_BATCH) for h in range(GD_HEADS)]
    n = range(len(probs))
    stack = lambda xs: jnp.concatenate(xs, axis=0)
    q_all = _l2n(stack([convs[i][:, h * GD_DK:(h + 1) * GD_DK] for i, h in probs])) * GD_DK ** -0.5
    k_all = _l2n(stack([convs[i][:, hk + h * GD_DK:hk + (h + 1) * GD_DK] for i, h in probs]))
    v_all = stack([convs[i][:, 2 * hk + h * GD_DV:2 * hk + (h + 1) * GD_DV] for i, h in probs])
    g_all = stack([gs[i][:, h:h + 1] for i, h in probs])
    b_all = stack([betas[i][:, h:h + 1] for i, h in probs])
    s_old = [s0_ref[0, i, h] for i, h in probs]
    kq_ss = [_dot(jnp.where(rows == 0, k_all[j:j + 1], jnp.where(rows == 1, q_all[j:j + 1], 0.0)).astype(BF16),
                  s_old[j].astype(BF16)) for j in n]
    u_all = b_all * (v_all - g_all * stack([kq_s[0:1] for kq_s in kq_ss]))
    for j, (i, h) in enumerate(probs):
        u8 = jnp.broadcast_to(u_all[j:j + 1], (SUBLANES, GD_DV)).astype(BF16)
        s_ref[0, i, h] = g_all[j:j + 1] * s_old[j] + _dot_tn(_row0(k_all[j:j + 1]).astype(BF16), u8)
    o_all = g_all * stack([kq_s[1:2] for kq_s in kq_ss]) + jnp.sum(q_all * k_all, axis=-1, keepdims=True) * u_all
    z_all = stack([p_ref[i, :, GD_QKV + h * GD_DV:GD_QKV + (h + 1) * GD_DV] for i, h in probs])
    out = _gdn_out(o_all, z_all, gout_ref[...]).astype(o_ref.dtype)
    for j, (i, h) in enumerate(probs):
        o_ref[i, :, h * GD_DV:(h + 1) * GD_DV] = out[j:j + 1]


def _gdn_step(p, gates, conv_prev, s0, layer_j, conv_w, a_log, dt_bias, g_out, s_prev):
    bsz = p.shape[0]
    bb = STEP_BATCH
    row_block = lambda width: pl.BlockSpec((bb, 1, width), lambda i: (i, 0, 0))
    return _gdn_call(_gdn_step_kernel, "gdn_step", p, gates, conv_prev, s0, layer_j,
                     conv_w, a_log, dt_bias, g_out, s_prev, (bsz // bb,), row_block, lambda i: i, bb, [])


def _lane_pad(x):
    return jnp.pad(x, [(0, 0)] * (x.ndim - 1) + [(0, LANES - x.shape[-1])])


def _prep_params(P):
    W = {}
    for name in ('ffn1_w_gate', 'ffn1_w_up', 'ffn1_w_down', 'ffn2_w_gate', 'ffn2_w_up', 'ffn2_w_down',
                 'ml_w_out', 'gd_w_out', 'xa_w_q', 'xa_w_k', 'xa_w_v', 'xa_w_o'):
        W[name] = P[name].astype(BF16)
    for name in ('g_ffn1', 'g_mix', 'g_xattn', 'g_mem', 'g_ffn2', 'ml_g_head', 'gd_g_out'):
        W[name] = P[name][:, None, :]
    for name in ('ml_b_i', 'ml_b_f', 'gd_a_log', 'gd_dt_bias'):
        W[name] = _lane_pad(P[name])[:, None, :]
    W['gd_conv_w'] = P['gd_conv_w']
    ml, gd = P['ml_w_in'], P['gd_w_in']
    W['ml_w_main'] = ml[..., :ML_MAIN].astype(BF16)
    W['ml_w_gate'] = jnp.concatenate(
        [_lane_pad(ml[..., ML_MAIN:ML_MAIN + ML_HEADS]), _lane_pad(ml[..., ML_MAIN + ML_HEADS:])], axis=-1).astype(BF16)
    W['gd_w_main'] = gd[..., :GD_MAIN].astype(BF16)
    W['gd_w_gate'] = jnp.concatenate(
        [_lane_pad(gd[..., GD_MAIN:GD_MAIN + GD_HEADS]), _lane_pad(gd[..., GD_MAIN + GD_HEADS:])], axis=-1).astype(BF16)
    return W


def _trunk(x, mem_k, mem_v, ml_c, ml_n, ml_m, gd_s, gd_conv, W):
    bsz, length, _ = x.shape
    single = length == 1
    x = x.reshape(bsz * length, D_MODEL)
    ml_m = _lane_pad(ml_m)[:, :, None, :]
    c_all = s_all = None
    new_n, new_m, new_conv = [], [], []
    y = None
    for layer in range(DEPTH):
        j = layer // 2
        x = _ffn(x, W['g_ffn1'], W['ffn1_w_gate'], W['ffn1_w_up'], W['ffn1_w_down'], layer)
        if layer % 2 == 0:
            p, gates = _inproj(x, W['g_mix'], layer, W['ml_w_main'], W['ml_w_gate'], j, BF16)
            fn = _mlstm_step if single else _mlstm_chunked
            a, c_all, n, m = fn(p.reshape(bsz, length, ML_MAIN), gates.reshape(bsz, length, GATE_COLS), ml_c, ml_n, ml_m,
                                j, W['ml_b_i'], W['ml_b_f'], W['ml_g_head'], c_all)
            new_n.append(n)
            new_m.append(m[:, 0, :ML_HEADS])
            w_out = W['ml_w_out']
        else:
            p, gates = _inproj(x, W['g_mix'], layer, W['gd_w_main'], W['gd_w_gate'], j, F32)
            fn = _gdn_step if single else _gdn_chunked
            a, s_all, cv = fn(p.reshape(bsz, length, GD_MAIN), gates.reshape(bsz, length, GATE_COLS), gd_conv, gd_s,
                              j, W['gd_conv_w'], W['gd_a_log'], W['gd_dt_bias'], W['gd_g_out'], s_all)
            new_conv.append(cv)
            w_out = W['gd_w_out']
        if single:
            a = a.reshape(bsz, D_MODEL)
            x = _proj(a, w_out, j, res=x)
            q = _proj(x, W['xa_w_q'], layer, g=W['g_xattn'], g_index=layer)
            x = _proj(_attn_step(q, mem_k, mem_v, layer), W['xa_w_o'], layer, res=x)
        else:
            x = _xattn_block(x.reshape(bsz, length, D_MODEL), a, mem_k, mem_v, layer, W['g_xattn'], w_out, j,
                             W['xa_w_q'], W['xa_w_o']).reshape(bsz * length, D_MODEL)
        if layer == DEPTH - 1:
            x, y = _ffn(x, W['g_ffn2'], W['ffn2_w_gate'], W['ffn2_w_up'], W['ffn2_w_down'], layer,
                        g_final=W['g_final'])
        else:
            x = _ffn(x, W['g_ffn2'], W['ffn2_w_gate'], W['ffn2_w_up'], W['ffn2_w_down'], layer)
    return (y.reshape(bsz, length, D_MODEL), c_all, jnp.stack(new_n), jnp.stack(new_m), s_all, jnp.stack(new_conv))


def kernel(x_prompt, x_sample, mem_prompt, cache_mem_k, cache_mem_v, state_mlstm_C, state_mlstm_n, state_mlstm_m, state_gdn_S, state_gdn_conv, g_ffn1, ffn1_w_gate, ffn1_w_up, ffn1_w_down, g_mix, ml_w_in, ml_b_i, ml_b_f, ml_g_head, ml_w_out, gd_w_in, gd_conv_w, gd_a_log, gd_dt_bias, gd_g_out, gd_w_out, g_xattn, g_mem, xa_w_q, xa_w_k, xa_w_v, xa_w_o, g_ffn2, ffn2_w_gate, ffn2_w_up, ffn2_w_down, g_final):
    P = dict(g_ffn1=g_ffn1, ffn1_w_gate=ffn1_w_gate, ffn1_w_up=ffn1_w_up, ffn1_w_down=ffn1_w_down, g_mix=g_mix,
             ml_w_in=ml_w_in, ml_b_i=ml_b_i, ml_b_f=ml_b_f, ml_g_head=ml_g_head, ml_w_out=ml_w_out,
             gd_w_in=gd_w_in, gd_conv_w=gd_conv_w, gd_a_log=gd_a_log, gd_dt_bias=gd_dt_bias, gd_g_out=gd_g_out,
             gd_w_out=gd_w_out, g_xattn=g_xattn, g_mem=g_mem, xa_w_q=xa_w_q, xa_w_k=xa_w_k, xa_w_v=xa_w_v,
             xa_w_o=xa_w_o, g_ffn2=g_ffn2, ffn2_w_gate=ffn2_w_gate, ffn2_w_up=ffn2_w_up, ffn2_w_down=ffn2_w_down)
    W = _prep_params(P)
    W['g_final'] = g_final
    batch, n_mem, _ = mem_prompt.shape
    n_ml, n_gd = state_mlstm_C.shape[0], state_gdn_S.shape[0]

    pk, pv, pkb, pvb = _memkv(mem_prompt.reshape(batch * n_mem, D_MODEL), W['g_mem'], W['xa_w_k'], W['xa_w_v'])
    z_c = jnp.zeros((n_ml, batch, ML_HEADS, ML_DQK, ML_DV), F32)
    z_n = jnp.zeros((n_ml, batch, ML_HEADS, ML_DQK), F32)
    z_m = jnp.zeros((n_ml, batch, ML_HEADS), F32)
    z_s = jnp.zeros((n_gd, batch, GD_HEADS, GD_DK, GD_DV), F32)
    z_conv = jnp.zeros((n_gd, batch, GD_CONV - 1, GD_QKV), F32)
    y_p, p_c, p_n, p_m, p_s, p_conv = _trunk(
        x_prompt, pkb.reshape(DEPTH, batch, n_mem, D_MODEL), pvb.reshape(DEPTH, batch, n_mem, D_MODEL),
        z_c, z_n, z_m, z_s, z_conv, W)

    y_s, s_c, s_n, s_m, s_s, s_conv = _trunk(
        x_sample, cache_mem_k, cache_mem_v,
        state_mlstm_C, state_mlstm_n, state_mlstm_m, state_gdn_S, state_gdn_conv, W)

    kv_shape = (DEPTH, batch, n_mem, XA_HEADS, XA_DH)
    return (y_p, y_s, pk.reshape(kv_shape), pv.reshape(kv_shape), p_c, p_n, p_m, p_s, p_conv,
            s_c, s_n, s_m, s_s, s_conv)
```

```python
import functools

import jax
import jax.numpy as jnp
from jax import lax
from jax.experimental import pallas as pl
from jax.experimental.pallas import tpu as pltpu

F32 = jnp.float32
BF16 = jnp.bfloat16

D_MODEL = 1024
DEPTH = 4
N_MEM = 256
D_FF = 2816
FFN_RES = 0.5
EPS = 1e-6
ML_HEADS = 4
ML_DV = 256
ML_DQK = 128
ML_MAIN = 2 * ML_HEADS * ML_DQK + 2 * D_MODEL
GD_HEADS = 8
GD_DK = 128
GD_DV = 128
GD_CONV = 4
GD_QKV = 3072
GD_MAIN = GD_QKV + D_MODEL
XA_HEADS = 4
XA_DH = 256
ML_CHUNK = 256
GD_CHUNK = 128
GD_CHUNK_ROWS = 2
ML_CHUNK_ROWS = 1
LANES = 128
SUBLANES = 8
GATE_COLS = 2 * LANES
VMEM_LIMIT_BYTES = 56 * 1024 * 1024
HI = lax.Precision.HIGHEST


def _params():
    return pltpu.CompilerParams(vmem_limit_bytes=VMEM_LIMIT_BYTES)


def _rms(x, g):
    return x * lax.rsqrt(jnp.mean(x * x, axis=-1, keepdims=True) + EPS) * g


def _silu(x):
    return x * jax.nn.sigmoid(x)


def _softplus(x):
    return jnp.maximum(x, 0.0) + jnp.log1p(jnp.exp(-jnp.abs(x)))


def _log_sigmoid(x):
    return jnp.minimum(x, 0.0) - jnp.log1p(jnp.exp(-jnp.abs(x)))


def _dot(a, b):
    return jnp.dot(a, b, preferred_element_type=F32)


def _dot_nt(a, b):
    return lax.dot_general(a, b, (((1,), (1,)), ((), ())), preferred_element_type=F32)


def _dot_tn(a, b):
    return lax.dot_general(a, b, (((0,), (0,)), ((), ())), preferred_element_type=F32)


def _resident(shape):
    nd = len(shape)
    return pl.BlockSpec(shape, lambda *_: (0,) * nd, pipeline_mode=pl.Buffered(1))


def _layer_block(arr, index):
    nd = arr.ndim - 1
    return pl.BlockSpec((None,) + arr.shape[1:], lambda *_: (index,) + (0,) * nd, pipeline_mode=pl.Buffered(1))


def _ffn_kernel(x_ref, g_ref, wg_ref, wu_ref, wd_ref, *rest, fchunk, final):
    if final:
        gf_ref, o_ref, y_ref, h_ref, a_ref = rest
    else:
        o_ref, h_ref, a_ref = rest
    h_ref[...] = _rms(x_ref[...], g_ref[...]).astype(BF16)
    for j in range(D_FF // fchunk):
        sl = slice(j * fchunk, (j + 1) * fchunk)
        h = h_ref[...]
        gate = _dot(h, wg_ref[:, sl])
        up = _dot(h, wu_ref[:, sl])
        a_ref[:, sl] = (_silu(gate) * up).astype(BF16)
    out = x_ref[...] + FFN_RES * _dot(a_ref[...], wd_ref[...])
    o_ref[...] = out
    if final:
        y_ref[...] = _rms(out, gf_ref[...])


def _ffn(x, g, wg, wu, wd, layer, g_final=None):
    m = x.shape[0]
    tm = min(m, 512)
    final = g_final is not None
    row = pl.BlockSpec((tm, D_MODEL), lambda i: (i, 0))
    in_specs = [row, _layer_block(g, layer), _layer_block(wg, layer), _layer_block(wu, layer), _layer_block(wd, layer)]
    args = [x, g, wg, wu, wd]
    out_shape = jax.ShapeDtypeStruct((m, D_MODEL), F32)
    out_specs = row
    if final:
        in_specs.append(_resident((1, D_MODEL)))
        args.append(g_final.reshape(1, D_MODEL))
        out_shape = (out_shape, out_shape)
        out_specs = (row, row)
    return pl.pallas_call(
        functools.partial(_ffn_kernel, fchunk=256, final=final),
        out_shape=out_shape, grid=(m // tm,), in_specs=in_specs, out_specs=out_specs,
        scratch_shapes=[pltpu.VMEM((tm, D_MODEL), BF16), pltpu.VMEM((tm, D_FF), BF16)],
        compiler_params=_params(), name="ffn_final" if final else "ffn",
    )(*args)


def _matmul_chunks(h, w_ref, o_ref, nchunk, res_ref=None):
    n = w_ref.shape[1]
    for n0 in range(0, n, nchunk):
        n1 = min(n, n0 + nchunk)
        y = _dot(h, w_ref[:, n0:n1])
        if res_ref is not None:
            y = y + res_ref[:, n0:n1]
        o_ref[:, n0:n1] = y.astype(o_ref.dtype)


def _proj_kernel(*refs, norm, res, nchunk):
    refs = list(refs)
    x_ref = refs.pop(0)
    g_ref = refs.pop(0) if norm else None
    w_ref = refs.pop(0)
    r_ref = refs.pop(0) if res else None
    o_ref = refs.pop(0)
    if norm:
        h = _rms(x_ref[...], g_ref[...]).astype(BF16)
    else:
        h = x_ref[...].astype(BF16)
    _matmul_chunks(h, w_ref, o_ref, nchunk, r_ref)


def _proj(x, w, w_index, g=None, g_index=None, res=None, out_dtype=F32):
    m, k = x.shape
    n = w.shape[-1]
    tm = min(m, 512)
    in_specs = [pl.BlockSpec((tm, k), lambda i: (i, 0))]
    args = [x]
    if g is not None:
        in_specs.append(_layer_block(g, g_index))
        args.append(g)
    in_specs.append(_layer_block(w, w_index))
    args.append(w)
    if res is not None:
        in_specs.append(pl.BlockSpec((tm, n), lambda i: (i, 0)))
        args.append(res)
    return pl.pallas_call(
        functools.partial(_proj_kernel, norm=g is not None, res=res is not None, nchunk=512),
        out_shape=jax.ShapeDtypeStruct((m, n), out_dtype), grid=(m // tm,), in_specs=in_specs,
        out_specs=pl.BlockSpec((tm, n), lambda i: (i, 0)),
        compiler_params=_params(), name="proj",
    )(*args)


def _inproj_kernel(x_ref, g_ref, wm_ref, wg_ref, om_ref, og_ref, *, nchunk):
    h = _rms(x_ref[...], g_ref[...]).astype(BF16)
    _matmul_chunks(h, wm_ref, om_ref, nchunk)
    og_ref[...] = _dot(h, wg_ref[...])


def _inproj(x, g, g_index, w_main, w_gate, w_index, main_dtype):
    m = x.shape[0]
    n = w_main.shape[-1]
    tm = min(m, 512)
    rows = lambda width: pl.BlockSpec((tm, width), lambda i: (i, 0))
    return pl.pallas_call(
        functools.partial(_inproj_kernel, nchunk=512),
        out_shape=(jax.ShapeDtypeStruct((m, n), main_dtype), jax.ShapeDtypeStruct((m, GATE_COLS), F32)),
        grid=(m // tm,),
        in_specs=[rows(D_MODEL), _layer_block(g, g_index), _layer_block(w_main, w_index), _layer_block(w_gate, w_index)],
        out_specs=(rows(n), rows(GATE_COLS)),
        compiler_params=_params(), name="inproj",
    )(x, g, w_main, w_gate)


def _memkv_kernel(x_ref, g_ref, wk_ref, wv_ref, k_ref, v_ref, kb_ref, vb_ref):
    h = _rms(x_ref[...], g_ref[...]).astype(BF16)
    k = _dot(h, wk_ref[...])
    v = _dot(h, wv_ref[...])
    kb_ref[0] = k.astype(BF16)
    vb_ref[0] = v.astype(BF16)
    for i in range(XA_HEADS):
        sl = slice(i * XA_DH, (i + 1) * XA_DH)
        k_ref[0, :, i, :] = k[:, sl]
        v_ref[0, :, i, :] = v[:, sl]


def _memkv(mem, g_mem, wk, wv):
    m = mem.shape[0]
    tm = 512
    w_spec = pl.BlockSpec((None, D_MODEL, D_MODEL), lambda l, i: (l, 0, 0))
    o_spec = pl.BlockSpec((1, tm, XA_HEADS, XA_DH), lambda l, i: (l, i, 0, 0))
    b_spec = pl.BlockSpec((1, tm, D_MODEL), lambda l, i: (l, i, 0))
    o_shape = jax.ShapeDtypeStruct((DEPTH, m, XA_HEADS, XA_DH), F32)
    b_shape = jax.ShapeDtypeStruct((DEPTH, m, D_MODEL), BF16)
    return pl.pallas_call(
        _memkv_kernel, out_shape=(o_shape, o_shape, b_shape, b_shape), grid=(DEPTH, m // tm),
        in_specs=[pl.BlockSpec((tm, D_MODEL), lambda l, i: (i, 0)),
                  pl.BlockSpec((None, 1, D_MODEL), lambda l, i: (l, 0, 0)), w_spec, w_spec],
        out_specs=(o_spec, o_spec, b_spec, b_spec), compiler_params=_params(), name="memkv",
    )(mem, g_mem, wk, wv)


def _xattn_kernel(x_ref, a_ref, k_ref, v_ref, g_ref, wout_ref, wq_ref, wo_ref, o_ref, att_ref):
    x2 = x_ref[0] + _dot(a_ref[0], wout_ref[...])
    q = _dot(_rms(x2, g_ref[...]).astype(BF16), wq_ref[...]).astype(BF16)
    for h in range(XA_HEADS):
        sl = slice(h * XA_DH, (h + 1) * XA_DH)
        s = _dot_nt(q[:, sl], k_ref[0, :, sl]) * XA_DH ** -0.5
        e = jnp.exp(s - jnp.max(s, axis=-1, keepdims=True))
        p = e / jnp.sum(e, axis=-1, keepdims=True)
        att_ref[:, sl] = _dot(p.astype(BF16), v_ref[0, :, sl]).astype(BF16)
    o_ref[0] = x2 + _dot(att_ref[...], wo_ref[...])


def _xattn_block(x, a, mem_kb, mem_vb, layer, g, w_out, w_out_index, wq, wo):
    b, l, _ = x.shape
    tq = min(l, 512)
    row = pl.BlockSpec((1, tq, D_MODEL), lambda i, j: (i, j, 0))
    kv_spec = pl.BlockSpec((None, 1, N_MEM, D_MODEL), lambda i, j: (layer, i, 0, 0))
    return pl.pallas_call(
        _xattn_kernel, out_shape=jax.ShapeDtypeStruct((b, l, D_MODEL), F32), grid=(b, l // tq),
        in_specs=[row, row, kv_spec, kv_spec, _layer_block(g, layer), _layer_block(w_out, w_out_index),
                  _layer_block(wq, layer), _layer_block(wo, layer)],
        out_specs=row, scratch_shapes=[pltpu.VMEM((tq, D_MODEL), BF16)],
        compiler_params=_params(), name="xattn",
    )(x, a, mem_kb, mem_vb, g, w_out, wq, wo)


ATTN_STEP_BATCH = 8


def _attn_step_kernel(q_ref, k_ref, v_ref, o_ref):
    for i in range(ATTN_STEP_BATCH):
        q = q_ref[0, i] * XA_DH ** -0.5
        s = jnp.sum(k_ref[0, i] * q[None], axis=-1, keepdims=True)
        e = jnp.exp(s - jnp.max(s, axis=0, keepdims=True))
        acc = jnp.sum(e * v_ref[0, i], axis=0)
        o_ref[0, i] = acc / jnp.sum(e, axis=0)


def _attn_step(q, mem_k, mem_v, layer):
    b = q.shape[0]
    bb = ATTN_STEP_BATCH
    kv_spec = pl.BlockSpec((1, bb, N_MEM, XA_HEADS, XA_DH), lambda i: (layer, i, 0, 0, 0))
    q_spec = pl.BlockSpec((1, bb, XA_HEADS, XA_DH), lambda i: (i, 0, 0, 0))
    out = pl.pallas_call(
        _attn_step_kernel, out_shape=jax.ShapeDtypeStruct((b // bb, bb, XA_HEADS, XA_DH), F32), grid=(b // bb,),
        in_specs=[q_spec, kv_spec, kv_spec], out_specs=q_spec,
        compiler_params=_params(), name="attn_step",
    )(q.reshape(b // bb, bb, XA_HEADS, XA_DH), mem_k, mem_v)
    return out.reshape(b, D_MODEL)


def _zero_other_layers(ref):
    ref[1:] = jnp.zeros((ref.shape[0] - 1,) + ref.shape[1:], F32)


def _state_out(n_layers, layer_j, bsz, bb, tail, idx, prev):
    shape = jax.ShapeDtypeStruct((n_layers, bsz) + tail, F32)
    zeros = (0,) * len(tail)
    if prev is None:
        spec = pl.BlockSpec((n_layers, bb) + tail, lambda *g: (0, idx(*g)) + zeros)
        return shape, spec, [], [], True
    spec = pl.BlockSpec((1, bb) + tail, lambda *g: (layer_j, idx(*g)) + zeros)
    return shape, spec, [pl.BlockSpec(memory_space=pl.ANY)], [prev], False


def _lane_select(h, value, into):
    lane = lax.broadcasted_iota(jnp.int32, into.shape, 1)
    return jnp.where(lane == h, value, into)


def _mlstm_chunk_kernel(p_ref, gt_ref, c0_ref, n0_ref, m0_ref, bi_ref, bf_ref, gh_ref, *rest, c, wide, aliased):
    if aliased:
        rest = rest[1:]
    o_ref, c_ref, n_ref, m_ref = rest
    rows = range(ML_CHUNK_ROWS)

    @pl.when(pl.program_id(1) == 0)
    def _():
        for r in rows:
            c_ref[0, r] = c0_ref[0, r]
            n_ref[r] = n0_ref[0, r]
            m_ref[r] = m0_ref[0, r]
        if wide:
            _zero_other_layers(c_ref)

    hq = ML_HEADS * ML_DQK
    row = lax.broadcasted_iota(jnp.int32, (c, c), 0)
    col = lax.broadcasted_iota(jnp.int32, (c, c), 1)
    incl = row >= col
    scale = ML_DQK ** -0.5
    i_pres, bs, xts, m_prevs = [], [], [], []
    for r in rows:
        i_pre = gt_ref[r, :, 0:LANES] + bi_ref[...]
        log_f = _log_sigmoid(gt_ref[r, :, LANES:GATE_COLS] + bf_ref[...])
        b = jnp.dot(incl.astype(F32), log_f, precision=HI, preferred_element_type=F32)
        i_pres.append(i_pre)
        bs.append(b)
        xts.append((i_pre - b).T)
        m_prevs.append(m_ref[r])
    probs = [(r, h) for r in rows for h in range(ML_HEADS)]
    n = range(len(probs))
    bcols = [bs[r][:, h:h + 1] for r, h in probs]
    icols = [i_pres[r][:, h:h + 1] for r, h in probs]
    mprev = [m_prevs[r][:, h:h + 1] for r, h in probs]
    qbs = [p_ref[r, :, h * ML_DQK:(h + 1) * ML_DQK] for r, h in probs]
    kbs = [p_ref[r, :, hq + h * ML_DQK:hq + (h + 1) * ML_DQK] for r, h in probs]
    vbs = [p_ref[r, :, 2 * hq + h * ML_DV:2 * hq + (h + 1) * ML_DV] for r, h in probs]
    dms = [jnp.where(incl, bcols[i] + xts[r][h:h + 1, :], -jnp.inf) for i, (r, h) in enumerate(probs)]
    mts = [jnp.maximum(bcols[i] + mprev[i], jnp.max(dms[i], axis=-1, keepdims=True)) for i in n]
    w_inters = [jnp.exp(bcols[i] + mprev[i] - mts[i]) for i in n]
    ss = [_dot_nt(qbs[i], kbs[i]) * (jnp.exp(dms[i] - mts[i]) * scale) for i in n]
    c_old = [c_ref[0, r, h] for r, h in probs]
    n_old = [n_ref[r, h:h + 1, :] for r, h in probs]
    nums = [w_inters[i] * _dot(qbs[i], c_old[i].astype(BF16)) + _dot(ss[i].astype(BF16), vbs[i]) for i in n]
    m_rows = list(m_prevs)
    for i, (r, h) in enumerate(probs):
        m_new = mts[i][c - 1:c, :]
        b_last = bcols[i][c - 1:c, :]
        w_k = jnp.exp(b_last - bcols[i] + icols[i] - m_new) * scale
        decay = jnp.exp(b_last + mprev[i] - m_new)
        kw = kbs[i].astype(F32) * w_k
        c_ref[0, r, h] = decay * c_old[i] + _dot_tn(kw.astype(BF16), vbs[i])
        n_ref[r, h:h + 1, :] = decay * n_old[i] + jnp.sum(kw, axis=0, keepdims=True)
        m_rows[r] = _lane_select(h, m_new, m_rows[r])
    for r in rows:
        m_ref[r] = m_rows[r]
    for i, (r, h) in enumerate(probs):
        den = (w_inters[i] * jnp.sum(qbs[i].astype(F32) * n_old[i], axis=-1, keepdims=True)
               + jnp.sum(ss[i], axis=-1, keepdims=True))
        hh = nums[i] / jnp.maximum(jnp.abs(den), jnp.exp(-mts[i]))
        vs = slice(h * ML_DV, (h + 1) * ML_DV)
        hn = hh * lax.rsqrt(jnp.mean(hh * hh, axis=-1, keepdims=True) + EPS) * gh_ref[:, vs]
        og = p_ref[r, :, 2 * hq + D_MODEL + h * ML_DV:2 * hq + D_MODEL + (h + 1) * ML_DV].astype(F32)
        o_ref[r, :, vs] = (hn * jax.nn.sigmoid(og)).astype(o_ref.dtype)


def _mlstm_small_specs(bsz, layer_j, idx, bb):
    c_in = pl.BlockSpec((1, bb, ML_HEADS, ML_DQK, ML_DV), lambda *g: (layer_j, idx(*g), 0, 0, 0))
    n_in = pl.BlockSpec((1, bb, ML_HEADS, ML_DQK), lambda *g: (layer_j, idx(*g), 0, 0))
    m_in = pl.BlockSpec((1, bb, 1, LANES), lambda *g: (layer_j, idx(*g), 0, 0))
    n_out = pl.BlockSpec((bb, ML_HEADS, ML_DQK), lambda *g: (idx(*g), 0, 0))
    m_out = pl.BlockSpec((bb, 1, LANES), lambda *g: (idx(*g), 0, 0))
    shapes = (jax.ShapeDtypeStruct((bsz, ML_HEADS, ML_DQK), F32), jax.ShapeDtypeStruct((bsz, 1, LANES), F32))
    return (c_in, n_in, m_in), (n_out, m_out), shapes


def _mlstm_call(kernel_fn, name, p, gates, c0, n0, m0, layer_j, b_i, b_f, g_head, c_prev, grid, row_block, idx, bb):
    bsz = p.shape[0]
    n_layers = c0.shape[0]
    s_in, s_out, s_shapes = _mlstm_small_specs(bsz, layer_j, idx, bb)
    c_shape, c_spec, extra_specs, extra_args, wide = _state_out(
        n_layers, layer_j, bsz, bb, (ML_HEADS, ML_DQK, ML_DV), idx, c_prev)
    n_in = 8
    return pl.pallas_call(
        functools.partial(kernel_fn, wide=wide, aliased=not wide),
        out_shape=(jax.ShapeDtypeStruct(p.shape[:2] + (D_MODEL,), BF16), c_shape) + s_shapes,
        grid=grid,
        in_specs=[row_block(ML_MAIN), row_block(GATE_COLS), *s_in, _layer_block(b_i, layer_j),
                  _layer_block(b_f, layer_j), _layer_block(g_head, layer_j), *extra_specs],
        out_specs=(row_block(D_MODEL), c_spec) + s_out,
        input_output_aliases={} if wide else {n_in: 1},
        compiler_params=_params(), name=name,
    )(p, gates, c0, n0, m0, b_i, b_f, g_head, *extra_args)


def _mlstm_chunked(p, gates, c0, n0, m0, layer_j, b_i, b_f, g_head, c_prev):
    bsz, length, _ = p.shape
    bb = ML_CHUNK_ROWS
    row_block = lambda width: pl.BlockSpec((bb, ML_CHUNK, width), lambda i, j: (i, j, 0))
    return _mlstm_call(functools.partial(_mlstm_chunk_kernel, c=ML_CHUNK), "mlstm_chunk", p, gates, c0, n0, m0, layer_j,
                       b_i, b_f, g_head, c_prev, (bsz // bb, length // ML_CHUNK), row_block, lambda i, j: i, bb)


STEP_BATCH = 4


def _row0(x, rows=SUBLANES):
    r = lax.broadcasted_iota(jnp.int32, (rows, x.shape[1]), 0)
    return jnp.where(r == 0, x, 0.0)


def _mlstm_step_kernel(p_ref, gt_ref, c0_ref, n0_ref, m0_ref, bi_ref, bf_ref, gh_ref, *rest, wide, aliased):
    if aliased:
        rest = rest[1:]
    o_ref, c_ref, n_ref, m_ref = rest
    hq = ML_HEADS * ML_DQK
    if wide:
        _zero_other_layers(c_ref)
    w_inters, w_ks, e_invs = [], [], []
    for i in range(STEP_BATCH):
        i_pre = gt_ref[i, :, 0:LANES] + bi_ref[...]
        log_f = _log_sigmoid(gt_ref[i, :, LANES:GATE_COLS] + bf_ref[...])
        m_prev = m0_ref[0, i]
        mt = jnp.maximum(log_f + m_prev, i_pre)
        w_inters.append(jnp.exp(log_f + m_prev - mt))
        w_ks.append(jnp.exp(i_pre - mt))
        e_invs.append(jnp.exp(-mt))
        m_ref[i] = mt
    probs = [(i, h) for i in range(STEP_BATCH) for h in range(ML_HEADS)]
    n = range(len(probs))
    qs = [p_ref[i, :, h * ML_DQK:(h + 1) * ML_DQK].astype(F32) for i, h in probs]
    ks = [p_ref[i, :, hq + h * ML_DQK:hq + (h + 1) * ML_DQK].astype(F32) * ML_DQK ** -0.5 for i, h in probs]
    vs = [p_ref[i, :, 2 * hq + h * ML_DV:2 * hq + (h + 1) * ML_DV].astype(F32) for i, h in probs]
    c_old = [c0_ref[0, i, h] for i, h in probs]
    n_old = [n0_ref[0, i, h:h + 1, :] for i, h in probs]
    wis = [w_inters[i][:, h:h + 1] for i, h in probs]
    wks = [w_ks[i][:, h:h + 1] for i, h in probs]
    q_cs = [_dot(_row0(qs[j]).astype(BF16), c_old[j].astype(BF16))[0:1] for j in n]
    kws = [ks[j] * wks[j] for j in n]
    for j, (i, h) in enumerate(probs):
        v8 = jnp.broadcast_to(vs[j], (SUBLANES, ML_DV)).astype(BF16)
        c_ref[0, i, h] = wis[j] * c_old[j] + _dot_tn(_row0(kws[j]).astype(BF16), v8)
        n_ref[i, h:h + 1, :] = wis[j] * n_old[j] + kws[j]
    stack = lambda xs: jnp.concatenate(xs, axis=0)
    sls = [slice(h * ML_DV, (h + 1) * ML_DV) for _, h in probs]
    q_all, k_all, v_all, wi_all, wk_all = stack(qs), stack(ks), stack(vs), stack(wis), stack(wks)
    s = jnp.sum(q_all * k_all, axis=-1, keepdims=True) * wk_all
    num = wi_all * stack(q_cs) + s * v_all
    den = wi_all * jnp.sum(q_all * stack(n_old), axis=-1, keepdims=True) + s
    hh = num / jnp.maximum(jnp.abs(den), stack([e_invs[i][:, h:h + 1] for i, h in probs]))
    hn = hh * lax.rsqrt(jnp.mean(hh * hh, axis=-1, keepdims=True) + EPS) * stack([gh_ref[:, sl] for sl in sls])
    og = stack([p_ref[i, :, 2 * hq + D_MODEL + h * ML_DV:2 * hq + D_MODEL + (h + 1) * ML_DV] for i, h in probs])
    out = (hn * jax.nn.sigmoid(og.astype(F32))).astype(o_ref.dtype)
    for j, (i, h) in enumerate(probs):
        o_ref[i, :, sls[j]] = out[j:j + 1]


def _mlstm_step(p, gates, c0, n0, m0, layer_j, b_i, b_f, g_head, c_prev):
    bsz = p.shape[0]
    bb = STEP_BATCH
    row_block = lambda width: pl.BlockSpec((bb, 1, width), lambda i: (i, 0, 0))
    return _mlstm_call(_mlstm_step_kernel, "mlstm_step", p, gates, c0, n0, m0, layer_j,
                       b_i, b_f, g_head, c_prev, (bsz // bb,), row_block, lambda i: i, bb)


def _l2n(x):
    return x * lax.rsqrt(jnp.sum(x * x, axis=-1, keepdims=True) + EPS)


INV_BASE = 16


def _unit_lower_inverses(a_list, c):
    row = lax.broadcasted_iota(jnp.int32, (c, c), 0)
    col = lax.broadcasted_iota(jnp.int32, (c, c), 1)
    eye = jnp.where(row == col, 1.0, 0.0)
    same = lambda size: (row >> (size.bit_length() - 1)) == (col >> (size.bit_length() - 1))
    ns = [jnp.where(same(INV_BASE), -a, 0.0) for a in a_list]
    ts = [eye + n for n in ns]
    nbs = [n.astype(BF16) for n in ns]
    power = 2
    while power < INV_BASE:
        ns = [_dot(nb, nb) for nb in nbs]
        nbs = [n.astype(BF16) for n in ns]
        ts = [t + _dot(t.astype(BF16), nb) for t, nb in zip(ts, nbs)]
        power *= 2
    size = INV_BASE
    while size < c:
        off = same(2 * size) & jnp.logical_not(same(size))
        tbs = [t.astype(BF16) for t in ts]
        mids = [_dot(tb, jnp.where(off, a, 0.0).astype(BF16)).astype(BF16) for tb, a in zip(tbs, a_list)]
        ts = [t - _dot(mid, tb) for t, mid, tb in zip(ts, mids, tbs)]
        size *= 2
    return ts


def _gdn_gates(gt_ref, i, alog_ref, dtb_ref):
    beta = jax.nn.sigmoid(gt_ref[i, :, 0:LANES])
    log_g = -jnp.exp(alog_ref[...]) * _softplus(gt_ref[i, :, LANES:GATE_COLS] + dtb_ref[...])
    return beta, log_g


def _gdn_out(o, z, gout):
    return (o * lax.rsqrt(jnp.mean(o * o, axis=-1, keepdims=True) + EPS) * gout * _silu(z))


def _inproj_gdn_kernel(x_ref, g_ref, wm_ref, wg_ref, cp_ref, cw_ref, om_ref, og_ref, nc_ref, e_ref,
                       *, tiles_per_seq, nchunk):
    assert GD_CONV == 4, "the conv below pairs four taps"
    tm = x_ref.shape[0]
    pad = SUBLANES
    keep = GD_CONV - 1
    h = _rms(x_ref[...], g_ref[...]).astype(BF16)
    og_ref[...] = _dot(h, wg_ref[...])

    @pl.when(pl.program_id(0) % tiles_per_seq == 0)
    def _():
        e_ref[...] = jnp.zeros((pad, GD_QKV), F32)
        e_ref[pl.ds(pad - keep, keep), :] = cp_ref[0, 0]

    for n0 in range(GD_QKV, GD_MAIN, nchunk):
        om_ref[:, n0:n0 + nchunk] = _dot(h, wm_ref[:, n0:n0 + nchunk]).astype(om_ref.dtype)
    hk = GD_HEADS * GD_DK
    for n0 in range(0, GD_QKV, nchunk):
        tile = _dot(h, wm_ref[:, n0:n0 + nchunk])
        ext = jnp.concatenate([e_ref[:, n0:n0 + nchunk], tile], axis=0)
        nc_ref[0, :, n0:n0 + nchunk] = tile[tm - keep:]
        e_ref[:, n0:n0 + nchunk] = tile[tm - pad:]
        for d0 in range(0, nchunk, GD_DK):
            sl = slice(n0 + d0, n0 + d0 + GD_DK)
            x = ext[:, d0:d0 + GD_DK]
            x1 = pltpu.roll(x, 1, 0)
            near = x * cw_ref[3:4, sl] + x1 * cw_ref[2:3, sl]
            far = x * cw_ref[1:2, sl] + x1 * cw_ref[0:1, sl]
            conv = _silu((near + pltpu.roll(far, 2, 0))[pad:])
            if n0 + d0 < hk:
                conv = _l2n(conv) * GD_DK ** -0.5
            elif n0 + d0 < 2 * hk:
                conv = _l2n(conv)
            om_ref[:, sl] = conv.astype(om_ref.dtype)


def _inproj_gdn(x, g, g_index, w_main, w_gate, w_index, conv_prev, conv_w, bsz):
    m = x.shape[0]
    tm = 512
    tiles_per_seq = m // bsz // tm
    rows = lambda width: pl.BlockSpec((tm, width), lambda i: (i, 0))
    return pl.pallas_call(
        functools.partial(_inproj_gdn_kernel, tiles_per_seq=tiles_per_seq, nchunk=256),
        out_shape=(jax.ShapeDtypeStruct((m, GD_MAIN), BF16), jax.ShapeDtypeStruct((m, GATE_COLS), F32),
                   jax.ShapeDtypeStruct((bsz, GD_CONV - 1, GD_QKV), F32)),
        grid=(m // tm,),
        in_specs=[rows(D_MODEL), _layer_block(g, g_index), _layer_block(w_main, w_index), _layer_block(w_gate, w_index),
                  pl.BlockSpec((1, 1, GD_CONV - 1, GD_QKV), lambda i: (w_index, i // tiles_per_seq, 0, 0)),
                  _layer_block(conv_w, w_index)],
        out_specs=(rows(GD_MAIN), rows(GATE_COLS),
                   pl.BlockSpec((1, GD_CONV - 1, GD_QKV), lambda i: (i // tiles_per_seq, 0, 0))),
        scratch_shapes=[pltpu.VMEM((SUBLANES, GD_QKV), F32)],
        compiler_params=_params(), name="inproj_gdn",
    )(x, g, w_main, w_gate, conv_prev, conv_w)


def _gdn_chunk_kernel(p_ref, gt_ref, alog_ref, dtb_ref, gout_ref, s0_ref, *rest, c, wide, aliased):
    if aliased:
        rest = rest[1:]
    o_ref, s_ref = rest
    rows = range(GD_CHUNK_ROWS)

    @pl.when(pl.program_id(1) == 0)
    def _():
        for r in rows:
            s_ref[0, r] = s0_ref[0, r]
        if wide:
            _zero_other_layers(s_ref)

    row = lax.broadcasted_iota(jnp.int32, (c, c), 0)
    col = lax.broadcasted_iota(jnp.int32, (c, c), 1)
    incl = row >= col
    strict = row > col
    hk = GD_HEADS * GD_DK
    betas, gams, gam_ts, e_gams = [], [], [], []
    for r in rows:
        beta, log_g = _gdn_gates(gt_ref, r, alog_ref, dtb_ref)
        gam = jnp.dot(incl.astype(F32), log_g, precision=HI, preferred_element_type=F32)
        betas.append(beta)
        gams.append(gam)
        gam_ts.append(gam.T)
        e_gams.append(jnp.exp(gam))
    probs = [(r, h) for r in rows for h in range(GD_HEADS)]
    col_of = lambda xs, r, h: xs[r][:, h:h + 1]
    qbs = [p_ref[r, :, h * GD_DK:(h + 1) * GD_DK] for r, h in probs]
    kbs = [p_ref[r, :, hk + h * GD_DK:hk + (h + 1) * GD_DK] for r, h in probs]
    vs = [p_ref[r, :, 2 * hk + h * GD_DV:2 * hk + (h + 1) * GD_DV].astype(F32) for r, h in probs]
    ks = [kb.astype(F32) for kb in kbs]
    bcols = [col_of(betas, r, h) for r, h in probs]
    gcols = [col_of(gams, r, h) for r, h in probs]
    ecols = [col_of(e_gams, r, h) for r, h in probs]
    n = range(len(probs))
    decs = [jnp.exp(jnp.where(incl, gcols[i] - gam_ts[r][h:h + 1, :], -jnp.inf)) for i, (r, h) in enumerate(probs)]
    kqs = [_dot_nt(jnp.concatenate([kbs[i], qbs[i]], axis=0), kbs[i]) for i in n]
    a_list = [jnp.where(strict, bcols[i] * kqs[i][0:c] * decs[i], 0.0) for i in n]
    ts = _unit_lower_inverses(a_list, c)
    rhs = [jnp.concatenate([bcols[i] * vs[i], (bcols[i] * ecols[i]) * ks[i]], axis=1).astype(BF16) for i in n]
    uws = [_dot(ts[i].astype(BF16), rhs[i]) for i in n]
    s_old = [s_ref[0, r, h] for r, h in probs]
    wq = [jnp.concatenate([uws[i][:, GD_DV:], qbs[i].astype(F32) * ecols[i]], axis=0).astype(BF16) for i in n]
    wqs = [_dot(wq[i], s_old[i].astype(BF16)) for i in n]
    ubs = [(uws[i][:, :GD_DV] - wqs[i][0:c]).astype(BF16) for i in n]
    for i, (r, h) in enumerate(probs):
        g_last = gcols[i][c - 1:c, :]
        k_dec = ks[i] * jnp.exp(g_last - gcols[i])
        s_ref[0, r, h] = jnp.exp(g_last) * s_old[i] + _dot_tn(k_dec.astype(BF16), ubs[i])
    for i, (r, h) in enumerate(probs):
        o = wqs[i][c:2 * c] + _dot((kqs[i][c:2 * c] * decs[i]).astype(BF16), ubs[i])
        z = p_ref[r, :, GD_QKV + h * GD_DV:GD_QKV + (h + 1) * GD_DV].astype(F32)
        o_ref[r, :, h * GD_DV:(h + 1) * GD_DV] = _gdn_out(o, z, gout_ref[...]).astype(o_ref.dtype)


def _gdn_chunked(p, gates, s0, layer_j, a_log, dt_bias, g_out, s_prev):
    bsz, length, _ = p.shape
    n_layers = s0.shape[0]
    bb = GD_CHUNK_ROWS
    idx = lambda i, j: i
    row_block = lambda width: pl.BlockSpec((bb, GD_CHUNK, width), lambda i, j: (i, j, 0))
    s_in = pl.BlockSpec((1, bb, GD_HEADS, GD_DK, GD_DV), lambda i, j: (layer_j, i, 0, 0, 0))
    s_shape, s_spec, extra_specs, extra_args, wide = _state_out(
        n_layers, layer_j, bsz, bb, (GD_HEADS, GD_DK, GD_DV), idx, s_prev)
    n_in = 6
    return pl.pallas_call(
        functools.partial(_gdn_chunk_kernel, c=GD_CHUNK, wide=wide, aliased=not wide),
        out_shape=(jax.ShapeDtypeStruct((bsz, length, D_MODEL), BF16), s_shape),
        grid=(bsz // bb, length // GD_CHUNK),
        in_specs=[row_block(GD_MAIN), row_block(GATE_COLS), _layer_block(a_log, layer_j),
                  _layer_block(dt_bias, layer_j), _layer_block(g_out, layer_j), s_in, *extra_specs],
        out_specs=(row_block(D_MODEL), s_spec),
        input_output_aliases={} if wide else {n_in: 1},
        compiler_params=_params(), name="gdn_chunk",
    )(p, gates, a_log, dt_bias, g_out, s0, *extra_args)


def _gdn_call(kernel_fn, name, p, gates, conv_prev, s0, layer_j, conv_w, a_log, dt_bias, g_out, s_prev,
              grid, row_block, idx, bb, scratch):
    bsz = p.shape[0]
    n_layers = s0.shape[0]
    cp_in = pl.BlockSpec((1, bb, GD_CONV - 1, GD_QKV), lambda *g: (layer_j, idx(*g), 0, 0))
    s_in = pl.BlockSpec((1, bb, GD_HEADS, GD_DK, GD_DV), lambda *g: (layer_j, idx(*g), 0, 0, 0))
    nc_out = pl.BlockSpec((bb, GD_CONV - 1, GD_QKV), lambda *g: (idx(*g), 0, 0))
    s_shape, s_spec, extra_specs, extra_args, wide = _state_out(
        n_layers, layer_j, bsz, bb, (GD_HEADS, GD_DK, GD_DV), idx, s_prev)
    n_in = 8
    return pl.pallas_call(
        functools.partial(kernel_fn, wide=wide, aliased=not wide),
        out_shape=(jax.ShapeDtypeStruct(p.shape[:2] + (D_MODEL,), BF16), s_shape,
                   jax.ShapeDtypeStruct((bsz, GD_CONV - 1, GD_QKV), F32)),
        grid=grid,
        in_specs=[row_block(GD_MAIN), row_block(GATE_COLS), cp_in, _layer_block(conv_w, layer_j),
                  _layer_block(a_log, layer_j), _layer_block(dt_bias, layer_j), _layer_block(g_out, layer_j), s_in,
                  *extra_specs],
        out_specs=(row_block(D_MODEL), s_spec, nc_out),
        input_output_aliases={} if wide else {n_in: 1},
        scratch_shapes=scratch, compiler_params=_params(), name=name,
    )(p, gates, conv_prev, conv_w, a_log, dt_bias, g_out, s0, *extra_args)


def _gdn_step_kernel(p_ref, gt_ref, cp_ref, cw_ref, alog_ref, dtb_ref, gout_ref, s0_ref, *rest, wide, aliased):
    if aliased:
        rest = rest[1:]
    o_ref, s_ref, nc_ref = rest
    hk = GD_HEADS * GD_DK
    rows = lax.broadcasted_iota(jnp.int32, (SUBLANES, GD_DK), 0)
    if wide:
        _zero_other_layers(s_ref)
    convs, betas, gs = [], [], []
    for i in range(STEP_BATCH):
        qkv = p_ref[i, :, 0:GD_QKV]
        prev = cp_ref[0, i]
        conv = qkv * cw_ref[GD_CONV - 1:GD_CONV, :]
        for j in range(GD_CONV - 1):
            conv = conv + prev[j:j + 1, :] * cw_ref[j:j + 1, :]
        convs.append(_silu(conv))
        nc_ref[i] = jnp.concatenate([prev[1:GD_CONV - 1], qkv], axis=0)
        beta, log_g = _gdn_gates(gt_ref, i, alog_ref, dtb_ref)
        betas.append(beta)
        gs.append(jnp.exp(log_g))
    probs = [(i, h) for i in range(STEP_BATCH) for h in range(GD_HEADS)]
    n = range(len(probs))
    stack = lambda xs: jnp.concatenate(xs, axis=0)
    q_all = _l2n(stack([convs[i][:, h * GD_DK:(h + 1) * GD_DK] for i, h in probs])) * GD_DK ** -0.5
    k_all = _l2n(stack([convs[i][:, hk + h * GD_DK:hk + (h + 1) * GD_DK] for i, h in probs]))
    v_all = stack([convs[i][:, 2 * hk + h * GD_DV:2 * hk + (h + 1) * GD_DV] for i, h in probs])
    g_all = stack([gs[i][:, h:h + 1] for i, h in probs])
    b_all = stack([betas[i][:, h:h + 1] for i, h in probs])
    s_old = [s0_ref[0, i, h] for i, h in probs]
    kq_ss = [_dot(jnp.where(rows == 0, k_all[j:j + 1], jnp.where(rows == 1, q_all[j:j + 1], 0.0)).astype(BF16),
                  s_old[j].astype(BF16)) for j in n]
    u_all = b_all * (v_all - g_all * stack([kq_s[0:1] for kq_s in kq_ss]))
    for j, (i, h) in enumerate(probs):
        u8 = jnp.broadcast_to(u_all[j:j + 1], (SUBLANES, GD_DV)).astype(BF16)
        s_ref[0, i, h] = g_all[j:j + 1] * s_old[j] + _dot_tn(_row0(k_all[j:j + 1]).astype(BF16), u8)
    o_all = g_all * stack([kq_s[1:2] for kq_s in kq_ss]) + jnp.sum(q_all * k_all, axis=-1, keepdims=True) * u_all
    z_all = stack([p_ref[i, :, GD_QKV + h * GD_DV:GD_QKV + (h + 1) * GD_DV] for i, h in probs])
    out = _gdn_out(o_all, z_all, gout_ref[...]).astype(o_ref.dtype)
    for j, (i, h) in enumerate(probs):
        o_ref[i, :, h * GD_DV:(h + 1) * GD_DV] = out[j:j + 1]


def _gdn_step(p, gates, conv_prev, s0, layer_j, conv_w, a_log, dt_bias, g_out, s_prev):
    bsz = p.shape[0]
    bb = STEP_BATCH
    row_block = lambda width: pl.BlockSpec((bb, 1, width), lambda i: (i, 0, 0))
    return _gdn_call(_gdn_step_kernel, "gdn_step", p, gates, conv_prev, s0, layer_j,
                     conv_w, a_log, dt_bias, g_out, s_prev, (bsz // bb,), row_block, lambda i: i, bb, [])


def _lane_pad(x):
    return jnp.pad(x, [(0, 0)] * (x.ndim - 1) + [(0, LANES - x.shape[-1])])


def _prep_params(P):
    W = {}
    for name in ('ffn1_w_gate', 'ffn1_w_up', 'ffn1_w_down', 'ffn2_w_gate', 'ffn2_w_up', 'ffn2_w_down',
                 'ml_w_out', 'gd_w_out', 'xa_w_q', 'xa_w_k', 'xa_w_v', 'xa_w_o'):
        W[name] = P[name].astype(BF16)
    for name in ('g_ffn1', 'g_mix', 'g_xattn', 'g_mem', 'g_ffn2', 'ml_g_head', 'gd_g_out'):
        W[name] = P[name][:, None, :]
    for name in ('ml_b_i', 'ml_b_f', 'gd_a_log', 'gd_dt_bias'):
        W[name] = _lane_pad(P[name])[:, None, :]
    W['gd_conv_w'] = P['gd_conv_w']
    ml, gd = P['ml_w_in'], P['gd_w_in']
    W['ml_w_main'] = ml[..., :ML_MAIN].astype(BF16)
    W['ml_w_gate'] = jnp.concatenate(
        [_lane_pad(ml[..., ML_MAIN:ML_MAIN + ML_HEADS]), _lane_pad(ml[..., ML_MAIN + ML_HEADS:])], axis=-1).astype(BF16)
    W['gd_w_main'] = gd[..., :GD_MAIN].astype(BF16)
    W['gd_w_gate'] = jnp.concatenate(
        [_lane_pad(gd[..., GD_MAIN:GD_MAIN + GD_HEADS]), _lane_pad(gd[..., GD_MAIN + GD_HEADS:])], axis=-1).astype(BF16)
    return W


def _trunk(x, mem_k, mem_v, ml_c, ml_n, ml_m, gd_s, gd_conv, W):
    bsz, length, _ = x.shape
    single = length == 1
    x = x.reshape(bsz * length, D_MODEL)
    ml_m = _lane_pad(ml_m)[:, :, None, :]
    c_all = s_all = None
    new_n, new_m, new_conv = [], [], []
    y = None
    for layer in range(DEPTH):
        j = layer // 2
        x = _ffn(x, W['g_ffn1'], W['ffn1_w_gate'], W['ffn1_w_up'], W['ffn1_w_down'], layer)
        if layer % 2 == 0:
            p, gates = _inproj(x, W['g_mix'], layer, W['ml_w_main'], W['ml_w_gate'], j, BF16)
            fn = _mlstm_step if single else _mlstm_chunked
            a, c_all, n, m = fn(p.reshape(bsz, length, ML_MAIN), gates.reshape(bsz, length, GATE_COLS), ml_c, ml_n, ml_m,
                                j, W['ml_b_i'], W['ml_b_f'], W['ml_g_head'], c_all)
            new_n.append(n)
            new_m.append(m[:, 0, :ML_HEADS])
            w_out = W['ml_w_out']
        else:
            if single:
                p, gates = _inproj(x, W['g_mix'], layer, W['gd_w_main'], W['gd_w_gate'], j, F32)
                a, s_all, cv = _gdn_step(p.reshape(bsz, 1, GD_MAIN), gates.reshape(bsz, 1, GATE_COLS), gd_conv, gd_s,
                                         j, W['gd_conv_w'], W['gd_a_log'], W['gd_dt_bias'], W['gd_g_out'], s_all)
            else:
                p, gates, cv = _inproj_gdn(x, W['g_mix'], layer, W['gd_w_main'], W['gd_w_gate'], j, gd_conv,
                                           W['gd_conv_w'], bsz)
                a, s_all = _gdn_chunked(p.reshape(bsz, length, GD_MAIN), gates.reshape(bsz, length, GATE_COLS), gd_s,
                                        j, W['gd_a_log'], W['gd_dt_bias'], W['gd_g_out'], s_all)
            new_conv.append(cv)
            w_out = W['gd_w_out']
        if single:
            a = a.reshape(bsz, D_MODEL)
            x = _proj(a, w_out, j, res=x)
            q = _proj(x, W['xa_w_q'], layer, g=W['g_xattn'], g_index=layer)
            x = _proj(_attn_step(q, mem_k, mem_v, layer), W['xa_w_o'], layer, res=x)
        else:
            x = _xattn_block(x.reshape(bsz, length, D_MODEL), a, mem_k, mem_v, layer, W['g_xattn'], w_out, j,
                             W['xa_w_q'], W['xa_w_o']).reshape(bsz * length, D_MODEL)
        if layer == DEPTH - 1:
            x, y = _ffn(x, W['g_ffn2'], W['ffn2_w_gate'], W['ffn2_w_up'], W['ffn2_w_down'], layer,
                        g_final=W['g_final'])
        else:
            x = _ffn(x, W['g_ffn2'], W['ffn2_w_gate'], W['ffn2_w_up'], W['ffn2_w_down'], layer)
    return (y.reshape(bsz, length, D_MODEL), c_all, jnp.stack(new_n), jnp.stack(new_m), s_all, jnp.stack(new_conv))


def kernel(x_prompt, x_sample, mem_prompt, cache_mem_k, cache_mem_v, state_mlstm_C, state_mlstm_n, state_mlstm_m, state_gdn_S, state_gdn_conv, g_ffn1, ffn1_w_gate, ffn1_w_up, ffn1_w_down, g_mix, ml_w_in, ml_b_i, ml_b_f, ml_g_head, ml_w_out, gd_w_in, gd_conv_w, gd_a_log, gd_dt_bias, gd_g_out, gd_w_out, g_xattn, g_mem, xa_w_q, xa_w_k, xa_w_v, xa_w_o, g_ffn2, ffn2_w_gate, ffn2_w_up, ffn2_w_down, g_final):
    P = dict(g_ffn1=g_ffn1, ffn1_w_gate=ffn1_w_gate, ffn1_w_up=ffn1_w_up, ffn1_w_down=ffn1_w_down, g_mix=g_mix,
             ml_w_in=ml_w_in, ml_b_i=ml_b_i, ml_b_f=ml_b_f, ml_g_head=ml_g_head, ml_w_out=ml_w_out,
             gd_w_in=gd_w_in, gd_conv_w=gd_conv_w, gd_a_log=gd_a_log, gd_dt_bias=gd_dt_bias, gd_g_out=gd_g_out,
             gd_w_out=gd_w_out, g_xattn=g_xattn, g_mem=g_mem, xa_w_q=xa_w_q, xa_w_k=xa_w_k, xa_w_v=xa_w_v,
             xa_w_o=xa_w_o, g_ffn2=g_ffn2, ffn2_w_gate=ffn2_w_gate, ffn2_w_up=ffn2_w_up, ffn2_w_down=ffn2_w_down)
    W = _prep_params(P)
    W['g_final'] = g_final
    batch, n_mem, _ = mem_prompt.shape
    n_ml, n_gd = state_mlstm_C.shape[0], state_gdn_S.shape[0]

    pk, pv, pkb, pvb = _memkv(mem_prompt.reshape(batch * n_mem, D_MODEL), W['g_mem'], W['xa_w_k'], W['xa_w_v'])
    z_c = jnp.zeros((n_ml, batch, ML_HEADS, ML_DQK, ML_DV), F32)
    z_n = jnp.zeros((n_ml, batch, ML_HEADS, ML_DQK), F32)
    z_m = jnp.zeros((n_ml, batch, ML_HEADS), F32)
    z_s = jnp.zeros((n_gd, batch, GD_HEADS, GD_DK, GD_DV), F32)
    z_conv = jnp.zeros((n_gd, batch, GD_CONV - 1, GD_QKV), F32)
    y_p, p_c, p_n, p_m, p_s, p_conv = _trunk(
        x_prompt, pkb.reshape(DEPTH, batch, n_mem, D_MODEL), pvb.reshape(DEPTH, batch, n_mem, D_MODEL),
        z_c, z_n, z_m, z_s, z_conv, W)

    y_s, s_c, s_n, s_m, s_s, s_conv = _trunk(
        x_sample, cache_mem_k, cache_mem_v,
        state_mlstm_C, state_mlstm_n, state_mlstm_m, state_gdn_S, state_gdn_conv, W)

    kv_shape = (DEPTH, batch, n_mem, XA_HEADS, XA_DH)
    return (y_p, y_s, pk.reshape(kv_shape), pv.reshape(kv_shape), p_c, p_n, p_m, p_s, p_conv,
            s_c, s_n, s_m, s_s, s_conv)
```

```python
import functools

import jax
import jax.numpy as jnp
from jax import lax
from jax.experimental import pallas as pl
from jax.experimental.pallas import tpu as pltpu

F32 = jnp.float32
BF16 = jnp.bfloat16

D_MODEL = 1024
DEPTH = 4
N_MEM = 256
D_FF = 2816
FFN_RES = 0.5
EPS = 1e-6
ML_HEADS = 4
ML_DV = 256
ML_DQK = 128
ML_MAIN = 2 * ML_HEADS * ML_DQK + 2 * D_MODEL
GD_HEADS = 8
GD_DK = 128
GD_DV = 128
GD_CONV = 4
GD_QKV = 3072
GD_MAIN = GD_QKV + D_MODEL
XA_HEADS = 4
XA_DH = 256
ML_CHUNK = 256
GD_CHUNK = 128
GD_CHUNK_ROWS = 2
ML_CHUNK_ROWS = 1
LANES = 128
SUBLANES = 8
GATE_COLS = 2 * LANES
VMEM_LIMIT_BYTES = 56 * 1024 * 1024
HI = lax.Precision.HIGHEST


def _params():
    return pltpu.CompilerParams(vmem_limit_bytes=VMEM_LIMIT_BYTES)


def _rms(x, g):
    return x * lax.rsqrt(jnp.mean(x * x, axis=-1, keepdims=True) + EPS) * g


def _silu(x):
    return x * jax.nn.sigmoid(x)


def _softplus(x):
    return jnp.maximum(x, 0.0) + jnp.log1p(jnp.exp(-jnp.abs(x)))


def _log_sigmoid(x):
    return jnp.minimum(x, 0.0) - jnp.log1p(jnp.exp(-jnp.abs(x)))


def _dot(a, b):
    return jnp.dot(a, b, preferred_element_type=F32)


def _dot_nt(a, b):
    return lax.dot_general(a, b, (((1,), (1,)), ((), ())), preferred_element_type=F32)


def _dot_tn(a, b):
    return lax.dot_general(a, b, (((0,), (0,)), ((), ())), preferred_element_type=F32)


def _resident(shape):
    nd = len(shape)
    return pl.BlockSpec(shape, lambda *_: (0,) * nd, pipeline_mode=pl.Buffered(1))


def _layer_block(arr, index):
    nd = arr.ndim - 1
    return pl.BlockSpec((None,) + arr.shape[1:], lambda *_: (index,) + (0,) * nd, pipeline_mode=pl.Buffered(1))


def _ffn_kernel(x_ref, g_ref, wg_ref, wu_ref, wd_ref, *rest, fchunk, final):
    if final:
        gf_ref, o_ref, y_ref, h_ref, a_ref = rest
    else:
        o_ref, h_ref, a_ref = rest
    h_ref[...] = _rms(x_ref[...], g_ref[...]).astype(BF16)
    for j in range(D_FF // fchunk):
        sl = slice(j * fchunk, (j + 1) * fchunk)
        h = h_ref[...]
        gate = _dot(h, wg_ref[:, sl])
        up = _dot(h, wu_ref[:, sl])
        a_ref[:, sl] = (_silu(gate) * up).astype(BF16)
    out = x_ref[...] + FFN_RES * _dot(a_ref[...], wd_ref[...])
    o_ref[...] = out
    if final:
        y_ref[...] = _rms(out, gf_ref[...])


def _ffn(x, g, wg, wu, wd, layer, g_final=None):
    m = x.shape[0]
    tm = min(m, 1024)
    final = g_final is not None
    row = pl.BlockSpec((tm, D_MODEL), lambda i: (i, 0))
    in_specs = [row, _layer_block(g, layer), _layer_block(wg, layer), _layer_block(wu, layer), _layer_block(wd, layer)]
    args = [x, g, wg, wu, wd]
    out_shape = jax.ShapeDtypeStruct((m, D_MODEL), F32)
    out_specs = row
    if final:
        in_specs.append(_resident((1, D_MODEL)))
        args.append(g_final.reshape(1, D_MODEL))
        out_shape = (out_shape, out_shape)
        out_specs = (row, row)
    return pl.pallas_call(
        functools.partial(_ffn_kernel, fchunk=256, final=final),
        out_shape=out_shape, grid=(m // tm,), in_specs=in_specs, out_specs=out_specs,
        scratch_shapes=[pltpu.VMEM((tm, D_MODEL), BF16), pltpu.VMEM((tm, D_FF), BF16)],
        compiler_params=_params(), name="ffn_final" if final else "ffn",
    )(*args)


def _matmul_chunks(h, w_ref, o_ref, nchunk, res_ref=None):
    n = w_ref.shape[1]
    for n0 in range(0, n, nchunk):
        n1 = min(n, n0 + nchunk)
        y = _dot(h, w_ref[:, n0:n1])
        if res_ref is not None:
            y = y + res_ref[:, n0:n1]
        o_ref[:, n0:n1] = y.astype(o_ref.dtype)


def _proj_kernel(*refs, norm, res, nchunk):
    refs = list(refs)
    x_ref = refs.pop(0)
    g_ref = refs.pop(0) if norm else None
    w_ref = refs.pop(0)
    r_ref = refs.pop(0) if res else None
    o_ref = refs.pop(0)
    if norm:
        h = _rms(x_ref[...], g_ref[...]).astype(BF16)
    else:
        h = x_ref[...].astype(BF16)
    _matmul_chunks(h, w_ref, o_ref, nchunk, r_ref)


def _proj(x, w, w_index, g=None, g_index=None, res=None, out_dtype=F32):
    m, k = x.shape
    n = w.shape[-1]
    tm = min(m, 512)
    in_specs = [pl.BlockSpec((tm, k), lambda i: (i, 0))]
    args = [x]
    if g is not None:
        in_specs.append(_layer_block(g, g_index))
        args.append(g)
    in_specs.append(_layer_block(w, w_index))
    args.append(w)
    if res is not None:
        in_specs.append(pl.BlockSpec((tm, n), lambda i: (i, 0)))
        args.append(res)
    return pl.pallas_call(
        functools.partial(_proj_kernel, norm=g is not None, res=res is not None, nchunk=512),
        out_shape=jax.ShapeDtypeStruct((m, n), out_dtype), grid=(m // tm,), in_specs=in_specs,
        out_specs=pl.BlockSpec((tm, n), lambda i: (i, 0)),
        compiler_params=_params(), name="proj",
    )(*args)


def _inproj_kernel(x_ref, g_ref, wm_ref, wg_ref, om_ref, og_ref, *, nchunk):
    h = _rms(x_ref[...], g_ref[...]).astype(BF16)
    _matmul_chunks(h, wm_ref, om_ref, nchunk)
    og_ref[...] = _dot(h, wg_ref[...])


def _inproj(x, g, g_index, w_main, w_gate, w_index, main_dtype):
    m = x.shape[0]
    n = w_main.shape[-1]
    tm = min(m, 512)
    rows = lambda width: pl.BlockSpec((tm, width), lambda i: (i, 0))
    return pl.pallas_call(
        functools.partial(_inproj_kernel, nchunk=512),
        out_shape=(jax.ShapeDtypeStruct((m, n), main_dtype), jax.ShapeDtypeStruct((m, GATE_COLS), F32)),
        grid=(m // tm,),
        in_specs=[rows(D_MODEL), _layer_block(g, g_index), _layer_block(w_main, w_index), _layer_block(w_gate, w_index)],
        out_specs=(rows(n), rows(GATE_COLS)),
        compiler_params=_params(), name="inproj",
    )(x, g, w_main, w_gate)


def _memkv_kernel(x_ref, g_ref, wk_ref, wv_ref, k_ref, v_ref, kb_ref, vb_ref):
    h = _rms(x_ref[...], g_ref[...]).astype(BF16)
    k = _dot(h, wk_ref[...])
    v = _dot(h, wv_ref[...])
    kb_ref[0] = k.astype(BF16)
    vb_ref[0] = v.astype(BF16)
    for i in range(XA_HEADS):
        sl = slice(i * XA_DH, (i + 1) * XA_DH)
        k_ref[0, :, i, :] = k[:, sl]
        v_ref[0, :, i, :] = v[:, sl]


def _memkv(mem, g_mem, wk, wv):
    m = mem.shape[0]
    tm = 512
    w_spec = pl.BlockSpec((None, D_MODEL, D_MODEL), lambda l, i: (l, 0, 0))
    o_spec = pl.BlockSpec((1, tm, XA_HEADS, XA_DH), lambda l, i: (l, i, 0, 0))
    b_spec = pl.BlockSpec((1, tm, D_MODEL), lambda l, i: (l, i, 0))
    o_shape = jax.ShapeDtypeStruct((DEPTH, m, XA_HEADS, XA_DH), F32)
    b_shape = jax.ShapeDtypeStruct((DEPTH, m, D_MODEL), BF16)
    return pl.pallas_call(
        _memkv_kernel, out_shape=(o_shape, o_shape, b_shape, b_shape), grid=(DEPTH, m // tm),
        in_specs=[pl.BlockSpec((tm, D_MODEL), lambda l, i: (i, 0)),
                  pl.BlockSpec((None, 1, D_MODEL), lambda l, i: (l, 0, 0)), w_spec, w_spec],
        out_specs=(o_spec, o_spec, b_spec, b_spec), compiler_params=_params(), name="memkv",
    )(mem, g_mem, wk, wv)


def _xattn_kernel(x_ref, a_ref, k_ref, v_ref, g_ref, wout_ref, wq_ref, wo_ref, o_ref, att_ref):
    x2 = x_ref[0] + _dot(a_ref[0], wout_ref[...])
    q = _dot(_rms(x2, g_ref[...]).astype(BF16), wq_ref[...]).astype(BF16)
    for h in range(XA_HEADS):
        sl = slice(h * XA_DH, (h + 1) * XA_DH)
        s = _dot_nt(q[:, sl], k_ref[0, :, sl]) * XA_DH ** -0.5
        e = jnp.exp(s - jnp.max(s, axis=-1, keepdims=True))
        p = e / jnp.sum(e, axis=-1, keepdims=True)
        att_ref[:, sl] = _dot(p.astype(BF16), v_ref[0, :, sl]).astype(BF16)
    o_ref[0] = x2 + _dot(att_ref[...], wo_ref[...])


def _xattn_block(x, a, mem_kb, mem_vb, layer, g, w_out, w_out_index, wq, wo):
    b, l, _ = x.shape
    tq = min(l, 512)
    row = pl.BlockSpec((1, tq, D_MODEL), lambda i, j: (i, j, 0))
    kv_spec = pl.BlockSpec((None, 1, N_MEM, D_MODEL), lambda i, j: (layer, i, 0, 0))
    return pl.pallas_call(
        _xattn_kernel, out_shape=jax.ShapeDtypeStruct((b, l, D_MODEL), F32), grid=(b, l // tq),
        in_specs=[row, row, kv_spec, kv_spec, _layer_block(g, layer), _layer_block(w_out, w_out_index),
                  _layer_block(wq, layer), _layer_block(wo, layer)],
        out_specs=row, scratch_shapes=[pltpu.VMEM((tq, D_MODEL), BF16)],
        compiler_params=_params(), name="xattn",
    )(x, a, mem_kb, mem_vb, g, w_out, wq, wo)


ATTN_STEP_BATCH = 8


def _attn_step_kernel(q_ref, k_ref, v_ref, o_ref):
    nh = XA_HEADS
    pairs = (N_MEM // 2, 2 * nh, XA_DH)
    for i in range(ATTN_STEP_BATCH):
        q = q_ref[0, i] * XA_DH ** -0.5
        q2 = jnp.concatenate([q, q], axis=0)
        s = jnp.sum(k_ref[0, i].reshape(pairs) * q2[None], axis=-1, keepdims=True)
        mx = jnp.max(s, axis=0, keepdims=True)
        mx = jnp.maximum(mx[:, :nh], mx[:, nh:])
        e = jnp.exp(s - jnp.concatenate([mx, mx], axis=1))
        acc = jnp.sum(e * v_ref[0, i].reshape(pairs), axis=0)
        den = jnp.sum(e, axis=0)
        o_ref[0, i] = (acc[:nh] + acc[nh:]) / (den[:nh] + den[nh:])


def _attn_step(q, mem_k, mem_v, layer):
    b = q.shape[0]
    bb = ATTN_STEP_BATCH
    kv_spec = pl.BlockSpec((1, bb, N_MEM, XA_HEADS, XA_DH), lambda i: (layer, i, 0, 0, 0))
    q_spec = pl.BlockSpec((1, bb, XA_HEADS, XA_DH), lambda i: (i, 0, 0, 0))
    out = pl.pallas_call(
        _attn_step_kernel, out_shape=jax.ShapeDtypeStruct((b // bb, bb, XA_HEADS, XA_DH), F32), grid=(b // bb,),
        in_specs=[q_spec, kv_spec, kv_spec], out_specs=q_spec,
        compiler_params=_params(), name="attn_step",
    )(q.reshape(b // bb, bb, XA_HEADS, XA_DH), mem_k, mem_v)
    return out.reshape(b, D_MODEL)


def _zero_other_layers(ref):
    ref[1:] = jnp.zeros((ref.shape[0] - 1,) + ref.shape[1:], F32)


def _state_out(n_layers, layer_j, bsz, bb, tail, idx, prev):
    shape = jax.ShapeDtypeStruct((n_layers, bsz) + tail, F32)
    zeros = (0,) * len(tail)
    if prev is None:
        spec = pl.BlockSpec((n_layers, bb) + tail, lambda *g: (0, idx(*g)) + zeros)
        return shape, spec, [], [], True
    spec = pl.BlockSpec((1, bb) + tail, lambda *g: (layer_j, idx(*g)) + zeros)
    return shape, spec, [pl.BlockSpec(memory_space=pl.ANY)], [prev], False


def _lane_select(h, value, into):
    lane = lax.broadcasted_iota(jnp.int32, into.shape, 1)
    return jnp.where(lane == h, value, into)


def _mlstm_chunk_kernel(p_ref, gt_ref, c0_ref, n0_ref, m0_ref, bi_ref, bf_ref, gh_ref, *rest, c, wide, aliased):
    if aliased:
        rest = rest[1:]
    o_ref, c_ref, n_ref, m_ref = rest
    rows = range(ML_CHUNK_ROWS)

    @pl.when(pl.program_id(1) == 0)
    def _():
        for r in rows:
            c_ref[0, r] = c0_ref[0, r]
            n_ref[r] = n0_ref[0, r]
            m_ref[r] = m0_ref[0, r]
        if wide:
            _zero_other_layers(c_ref)

    hq = ML_HEADS * ML_DQK
    row = lax.broadcasted_iota(jnp.int32, (c, c), 0)
    col = lax.broadcasted_iota(jnp.int32, (c, c), 1)
    incl = row >= col
    scale = ML_DQK ** -0.5
    i_pres, bs, xts, m_prevs = [], [], [], []
    for r in rows:
        i_pre = gt_ref[r, :, 0:LANES] + bi_ref[...]
        log_f = _log_sigmoid(gt_ref[r, :, LANES:GATE_COLS] + bf_ref[...])
        b = jnp.dot(incl.astype(F32), log_f, precision=HI, preferred_element_type=F32)
        i_pres.append(i_pre)
        bs.append(b)
        xts.append((i_pre - b).T)
        m_prevs.append(m_ref[r])
    probs = [(r, h) for r in rows for h in range(ML_HEADS)]
    n = range(len(probs))
    bcols = [bs[r][:, h:h + 1] for r, h in probs]
    icols = [i_pres[r][:, h:h + 1] for r, h in probs]
    mprev = [m_prevs[r][:, h:h + 1] for r, h in probs]
    qbs = [p_ref[r, :, h * ML_DQK:(h + 1) * ML_DQK] for r, h in probs]
    kbs = [p_ref[r, :, hq + h * ML_DQK:hq + (h + 1) * ML_DQK] for r, h in probs]
    vbs = [p_ref[r, :, 2 * hq + h * ML_DV:2 * hq + (h + 1) * ML_DV] for r, h in probs]
    dms = [jnp.where(incl, bcols[i] + xts[r][h:h + 1, :], -jnp.inf) for i, (r, h) in enumerate(probs)]
    mts = [jnp.maximum(bcols[i] + mprev[i], jnp.max(dms[i], axis=-1, keepdims=True)) for i in n]
    w_inters = [jnp.exp(bcols[i] + mprev[i] - mts[i]) for i in n]
    ss = [_dot_nt(qbs[i], kbs[i]) * (jnp.exp(dms[i] - mts[i]) * scale) for i in n]
    c_old = [c_ref[0, r, h] for r, h in probs]
    n_old = [n_ref[r, h:h + 1, :] for r, h in probs]
    nums = [w_inters[i] * _dot(qbs[i], c_old[i].astype(BF16)) + _dot(ss[i].astype(BF16), vbs[i]) for i in n]
    m_rows = list(m_prevs)
    for i, (r, h) in enumerate(probs):
        m_new = mts[i][c - 1:c, :]
        b_last = bcols[i][c - 1:c, :]
        w_k = jnp.exp(b_last - bcols[i] + icols[i] - m_new) * scale
        decay = jnp.exp(b_last + mprev[i] - m_new)
        kw = kbs[i].astype(F32) * w_k
        c_ref[0, r, h] = decay * c_old[i] + _dot_tn(kw.astype(BF16), vbs[i])
        n_ref[r, h:h + 1, :] = decay * n_old[i] + jnp.sum(kw, axis=0, keepdims=True)
        m_rows[r] = _lane_select(h, m_new, m_rows[r])
    for r in rows:
        m_ref[r] = m_rows[r]
    for i, (r, h) in enumerate(probs):
        den = (w_inters[i] * jnp.sum(qbs[i].astype(F32) * n_old[i], axis=-1, keepdims=True)
               + jnp.sum(ss[i], axis=-1, keepdims=True))
        hh = nums[i] / jnp.maximum(jnp.abs(den), jnp.exp(-mts[i]))
        vs = slice(h * ML_DV, (h + 1) * ML_DV)
        hn = hh * lax.rsqrt(jnp.mean(hh * hh, axis=-1, keepdims=True) + EPS) * gh_ref[:, vs]
        og = p_ref[r, :, 2 * hq + D_MODEL + h * ML_DV:2 * hq + D_MODEL + (h + 1) * ML_DV].astype(F32)
        o_ref[r, :, vs] = (hn * jax.nn.sigmoid(og)).astype(o_ref.dtype)


def _mlstm_small_specs(bsz, layer_j, idx, bb):
    c_in = pl.BlockSpec((1, bb, ML_HEADS, ML_DQK, ML_DV), lambda *g: (layer_j, idx(*g), 0, 0, 0))
    n_in = pl.BlockSpec((1, bb, ML_HEADS, ML_DQK), lambda *g: (layer_j, idx(*g), 0, 0))
    m_in = pl.BlockSpec((1, bb, 1, LANES), lambda *g: (layer_j, idx(*g), 0, 0))
    n_out = pl.BlockSpec((bb, ML_HEADS, ML_DQK), lambda *g: (idx(*g), 0, 0))
    m_out = pl.BlockSpec((bb, 1, LANES), lambda *g: (idx(*g), 0, 0))
    shapes = (jax.ShapeDtypeStruct((bsz, ML_HEADS, ML_DQK), F32), jax.ShapeDtypeStruct((bsz, 1, LANES), F32))
    return (c_in, n_in, m_in), (n_out, m_out), shapes


def _mlstm_call(kernel_fn, name, p, gates, c0, n0, m0, layer_j, b_i, b_f, g_head, c_prev, grid, row_block, idx, bb):
    bsz = p.shape[0]
    n_layers = c0.shape[0]
    s_in, s_out, s_shapes = _mlstm_small_specs(bsz, layer_j, idx, bb)
    c_shape, c_spec, extra_specs, extra_args, wide = _state_out(
        n_layers, layer_j, bsz, bb, (ML_HEADS, ML_DQK, ML_DV), idx, c_prev)
    n_in = 8
    return pl.pallas_call(
        functools.partial(kernel_fn, wide=wide, aliased=not wide),
        out_shape=(jax.ShapeDtypeStruct(p.shape[:2] + (D_MODEL,), BF16), c_shape) + s_shapes,
        grid=grid,
        in_specs=[row_block(ML_MAIN), row_block(GATE_COLS), *s_in, _layer_block(b_i, layer_j),
                  _layer_block(b_f, layer_j), _layer_block(g_head, layer_j), *extra_specs],
        out_specs=(row_block(D_MODEL), c_spec) + s_out,
        input_output_aliases={} if wide else {n_in: 1},
        compiler_params=_params(), name=name,
    )(p, gates, c0, n0, m0, b_i, b_f, g_head, *extra_args)


def _mlstm_chunked(p, gates, c0, n0, m0, layer_j, b_i, b_f, g_head, c_prev):
    bsz, length, _ = p.shape
    bb = ML_CHUNK_ROWS
    row_block = lambda width: pl.BlockSpec((bb, ML_CHUNK, width), lambda i, j: (i, j, 0))
    return _mlstm_call(functools.partial(_mlstm_chunk_kernel, c=ML_CHUNK), "mlstm_chunk", p, gates, c0, n0, m0, layer_j,
                       b_i, b_f, g_head, c_prev, (bsz // bb, length // ML_CHUNK), row_block, lambda i, j: i, bb)


STEP_BATCH = 4


def _row0(x, rows=SUBLANES):
    r = lax.broadcasted_iota(jnp.int32, (rows, x.shape[1]), 0)
    return jnp.where(r == 0, x, 0.0)


def _mlstm_step_kernel(p_ref, gt_ref, c0_ref, n0_ref, m0_ref, bi_ref, bf_ref, gh_ref, *rest, wide, aliased):
    if aliased:
        rest = rest[1:]
    o_ref, c_ref, n_ref, m_ref = rest
    hq = ML_HEADS * ML_DQK
    if wide:
        _zero_other_layers(c_ref)
    w_inters, w_ks, e_invs = [], [], []
    for i in range(STEP_BATCH):
        i_pre = gt_ref[i, :, 0:LANES] + bi_ref[...]
        log_f = _log_sigmoid(gt_ref[i, :, LANES:GATE_COLS] + bf_ref[...])
        m_prev = m0_ref[0, i]
        mt = jnp.maximum(log_f + m_prev, i_pre)
        w_inters.append(jnp.exp(log_f + m_prev - mt))
        w_ks.append(jnp.exp(i_pre - mt))
        e_invs.append(jnp.exp(-mt))
        m_ref[i] = mt
    probs = [(i, h) for i in range(STEP_BATCH) for h in range(ML_HEADS)]
    n = range(len(probs))
    qs = [p_ref[i, :, h * ML_DQK:(h + 1) * ML_DQK].astype(F32) for i, h in probs]
    ks = [p_ref[i, :, hq + h * ML_DQK:hq + (h + 1) * ML_DQK].astype(F32) * ML_DQK ** -0.5 for i, h in probs]
    vs = [p_ref[i, :, 2 * hq + h * ML_DV:2 * hq + (h + 1) * ML_DV].astype(F32) for i, h in probs]
    c_old = [c0_ref[0, i, h] for i, h in probs]
    n_old = [n0_ref[0, i, h:h + 1, :] for i, h in probs]
    wis = [w_inters[i][:, h:h + 1] for i, h in probs]
    wks = [w_ks[i][:, h:h + 1] for i, h in probs]
    q_cs = [_dot(_row0(qs[j]).astype(BF16), c_old[j].astype(BF16))[0:1] for j in n]
    kws = [ks[j] * wks[j] for j in n]
    for j, (i, h) in enumerate(probs):
        v8 = jnp.broadcast_to(vs[j], (SUBLANES, ML_DV)).astype(BF16)
        c_ref[0, i, h] = wis[j] * c_old[j] + _dot_tn(_row0(kws[j]).astype(BF16), v8)
        n_ref[i, h:h + 1, :] = wis[j] * n_old[j] + kws[j]
    stack = lambda xs: jnp.concatenate(xs, axis=0)
    sls = [slice(h * ML_DV, (h + 1) * ML_DV) for _, h in probs]
    q_all, k_all, v_all, wi_all, wk_all = stack(qs), stack(ks), stack(vs), stack(wis), stack(wks)
    s = jnp.sum(q_all * k_all, axis=-1, keepdims=True) * wk_all
    num = wi_all * stack(q_cs) + s * v_all
    den = wi_all * jnp.sum(q_all * stack(n_old), axis=-1, keepdims=True) + s
    hh = num / jnp.maximum(jnp.abs(den), stack([e_invs[i][:, h:h + 1] for i, h in probs]))
    hn = hh * lax.rsqrt(jnp.mean(hh * hh, axis=-1, keepdims=True) + EPS) * stack([gh_ref[:, sl] for sl in sls])
    og = stack([p_ref[i, :, 2 * hq + D_MODEL + h * ML_DV:2 * hq + D_MODEL + (h + 1) * ML_DV] for i, h in probs])
    out = (hn * jax.nn.sigmoid(og.astype(F32))).astype(o_ref.dtype)
    for j, (i, h) in enumerate(probs):
        o_ref[i, :, sls[j]] = out[j:j + 1]


def _mlstm_step(p, gates, c0, n0, m0, layer_j, b_i, b_f, g_head, c_prev):
    bsz = p.shape[0]
    bb = STEP_BATCH
    row_block = lambda width: pl.BlockSpec((bb, 1, width), lambda i: (i, 0, 0))
    return _mlstm_call(_mlstm_step_kernel, "mlstm_step", p, gates, c0, n0, m0, layer_j,
                       b_i, b_f, g_head, c_prev, (bsz // bb,), row_block, lambda i: i, bb)


def _l2n(x):
    return x * lax.rsqrt(jnp.sum(x * x, axis=-1, keepdims=True) + EPS)


INV_BASE = 16


def _unit_lower_inverses(a_list, c):
    row = lax.broadcasted_iota(jnp.int32, (c, c), 0)
    col = lax.broadcasted_iota(jnp.int32, (c, c), 1)
    eye = jnp.where(row == col, 1.0, 0.0)
    same = lambda size: (row >> (size.bit_length() - 1)) == (col >> (size.bit_length() - 1))
    ns = [jnp.where(same(INV_BASE), -a, 0.0) for a in a_list]
    ts = [eye + n for n in ns]
    nbs = [n.astype(BF16) for n in ns]
    power = 2
    while power < INV_BASE:
        ns = [_dot(nb, nb) for nb in nbs]
        nbs = [n.astype(BF16) for n in ns]
        ts = [t + _dot(t.astype(BF16), nb) for t, nb in zip(ts, nbs)]
        power *= 2
    size = INV_BASE
    while size < c:
        off = same(2 * size) & jnp.logical_not(same(size))
        tbs = [t.astype(BF16) for t in ts]
        mids = [_dot(tb, jnp.where(off, a, 0.0).astype(BF16)).astype(BF16) for tb, a in zip(tbs, a_list)]
        ts = [t - _dot(mid, tb) for t, mid, tb in zip(ts, mids, tbs)]
        size *= 2
    return ts


def _gdn_gates(gt_ref, i, alog_ref, dtb_ref):
    beta = jax.nn.sigmoid(gt_ref[i, :, 0:LANES])
    log_g = -jnp.exp(alog_ref[...]) * _softplus(gt_ref[i, :, LANES:GATE_COLS] + dtb_ref[...])
    return beta, log_g


def _gdn_out(o, z, gout):
    return (o * lax.rsqrt(jnp.mean(o * o, axis=-1, keepdims=True) + EPS) * gout * _silu(z))


def _inproj_gdn_kernel(x_ref, g_ref, wm_ref, wg_ref, cp_ref, cw_ref, om_ref, og_ref, nc_ref, e_ref,
                       *, tiles_per_seq, nchunk):
    assert GD_CONV == 4, "the conv below pairs four taps"
    tm = x_ref.shape[0]
    pad = SUBLANES
    keep = GD_CONV - 1
    h = _rms(x_ref[...], g_ref[...]).astype(BF16)
    og_ref[...] = _dot(h, wg_ref[...])

    @pl.when(pl.program_id(0) % tiles_per_seq == 0)
    def _():
        e_ref[...] = jnp.zeros((pad, GD_QKV), F32)
        e_ref[pl.ds(pad - keep, keep), :] = cp_ref[0, 0]

    hk = GD_HEADS * GD_DK
    starts = list(range(0, GD_QKV, nchunk))
    z_starts = list(range(GD_QKV, GD_MAIN, nchunk))
    every = len(starts) // len(z_starts)
    for idx, n0 in enumerate(starts):
        if idx % every == 0:
            z0 = z_starts[idx // every]
            om_ref[:, z0:z0 + nchunk] = _dot(h, wm_ref[:, z0:z0 + nchunk]).astype(om_ref.dtype)
        tile = _dot(h, wm_ref[:, n0:n0 + nchunk])
        ext = jnp.concatenate([e_ref[:, n0:n0 + nchunk], tile], axis=0)
        nc_ref[0, :, n0:n0 + nchunk] = tile[tm - keep:]
        e_ref[:, n0:n0 + nchunk] = tile[tm - pad:]
        for d0 in range(0, nchunk, GD_DK):
            sl = slice(n0 + d0, n0 + d0 + GD_DK)
            x = ext[:, d0:d0 + GD_DK]
            x1 = pltpu.roll(x, 1, 0)
            near = x * cw_ref[3:4, sl] + x1 * cw_ref[2:3, sl]
            far = x * cw_ref[1:2, sl] + x1 * cw_ref[0:1, sl]
            conv = _silu((near + pltpu.roll(far, 2, 0))[pad:])
            if n0 + d0 < hk:
                conv = _l2n(conv) * GD_DK ** -0.5
            elif n0 + d0 < 2 * hk:
                conv = _l2n(conv)
            om_ref[:, sl] = conv.astype(om_ref.dtype)


def _inproj_gdn(x, g, g_index, w_main, w_gate, w_index, conv_prev, conv_w, bsz):
    m = x.shape[0]
    tm = 512
    tiles_per_seq = m // bsz // tm
    rows = lambda width: pl.BlockSpec((tm, width), lambda i: (i, 0))
    return pl.pallas_call(
        functools.partial(_inproj_gdn_kernel, tiles_per_seq=tiles_per_seq, nchunk=256),
        out_shape=(jax.ShapeDtypeStruct((m, GD_MAIN), BF16), jax.ShapeDtypeStruct((m, GATE_COLS), F32),
                   jax.ShapeDtypeStruct((bsz, GD_CONV - 1, GD_QKV), F32)),
        grid=(m // tm,),
        in_specs=[rows(D_MODEL), _layer_block(g, g_index), _layer_block(w_main, w_index), _layer_block(w_gate, w_index),
                  pl.BlockSpec((1, 1, GD_CONV - 1, GD_QKV), lambda i: (w_index, i // tiles_per_seq, 0, 0)),
                  _layer_block(conv_w, w_index)],
        out_specs=(rows(GD_MAIN), rows(GATE_COLS),
                   pl.BlockSpec((1, GD_CONV - 1, GD_QKV), lambda i: (i // tiles_per_seq, 0, 0))),
        scratch_shapes=[pltpu.VMEM((SUBLANES, GD_QKV), F32)],
        compiler_params=_params(), name="inproj_gdn",
    )(x, g, w_main, w_gate, conv_prev, conv_w)


def _gdn_chunk_kernel(p_ref, gt_ref, alog_ref, dtb_ref, gout_ref, s0_ref, *rest, c, wide, aliased):
    if aliased:
        rest = rest[1:]
    o_ref, s_ref = rest
    rows = range(GD_CHUNK_ROWS)

    @pl.when(pl.program_id(1) == 0)
    def _():
        for r in rows:
            s_ref[0, r] = s0_ref[0, r]
        if wide:
            _zero_other_layers(s_ref)

    row = lax.broadcasted_iota(jnp.int32, (c, c), 0)
    col = lax.broadcasted_iota(jnp.int32, (c, c), 1)
    incl = row >= col
    strict = row > col
    hk = GD_HEADS * GD_DK
    betas, gams, gam_ts, e_gams = [], [], [], []
    for r in rows:
        beta, log_g = _gdn_gates(gt_ref, r, alog_ref, dtb_ref)
        gam = jnp.dot(incl.astype(F32), log_g, precision=HI, preferred_element_type=F32)
        betas.append(beta)
        gams.append(gam)
        gam_ts.append(gam.T)
        e_gams.append(jnp.exp(gam))
    probs = [(r, h) for r in rows for h in range(GD_HEADS)]
    col_of = lambda xs, r, h: xs[r][:, h:h + 1]
    qbs = [p_ref[r, :, h * GD_DK:(h + 1) * GD_DK] for r, h in probs]
    kbs = [p_ref[r, :, hk + h * GD_DK:hk + (h + 1) * GD_DK] for r, h in probs]
    vs = [p_ref[r, :, 2 * hk + h * GD_DV:2 * hk + (h + 1) * GD_DV].astype(F32) for r, h in probs]
    ks = [kb.astype(F32) for kb in kbs]
    bcols = [col_of(betas, r, h) for r, h in probs]
    gcols = [col_of(gams, r, h) for r, h in probs]
    ecols = [col_of(e_gams, r, h) for r, h in probs]
    n = range(len(probs))
    decs = [jnp.exp(jnp.where(incl, gcols[i] - gam_ts[r][h:h + 1, :], -jnp.inf)) for i, (r, h) in enumerate(probs)]
    kqs = [_dot_nt(jnp.concatenate([kbs[i], qbs[i]], axis=0), kbs[i]) for i in n]
    a_list = [jnp.where(strict, bcols[i] * kqs[i][0:c] * decs[i], 0.0) for i in n]
    ts = _unit_lower_inverses(a_list, c)
    rhs = [jnp.concatenate([bcols[i] * vs[i], (bcols[i] * ecols[i]) * ks[i]], axis=1).astype(BF16) for i in n]
    uws = [_dot(ts[i].astype(BF16), rhs[i]) for i in n]
    s_old = [s_ref[0, r, h] for r, h in probs]
    wq = [jnp.concatenate([uws[i][:, GD_DV:], qbs[i].astype(F32) * ecols[i]], axis=0).astype(BF16) for i in n]
    wqs = [_dot(wq[i], s_old[i].astype(BF16)) for i in n]
    ubs = [(uws[i][:, :GD_DV] - wqs[i][0:c]).astype(BF16) for i in n]
    for i, (r, h) in enumerate(probs):
        g_last = gcols[i][c - 1:c, :]
        k_dec = ks[i] * jnp.exp(g_last - gcols[i])
        s_ref[0, r, h] = jnp.exp(g_last) * s_old[i] + _dot_tn(k_dec.astype(BF16), ubs[i])
    for i, (r, h) in enumerate(probs):
        o = wqs[i][c:2 * c] + _dot((kqs[i][c:2 * c] * decs[i]).astype(BF16), ubs[i])
        z = p_ref[r, :, GD_QKV + h * GD_DV:GD_QKV + (h + 1) * GD_DV].astype(F32)
        o_ref[r, :, h * GD_DV:(h + 1) * GD_DV] = _gdn_out(o, z, gout_ref[...]).astype(o_ref.dtype)


def _gdn_chunked(p, gates, s0, layer_j, a_log, dt_bias, g_out, s_prev):
    bsz, length, _ = p.shape
    n_layers = s0.shape[0]
    bb = GD_CHUNK_ROWS
    idx = lambda i, j: i
    row_block = lambda width: pl.BlockSpec((bb, GD_CHUNK, width), lambda i, j: (i, j, 0))
    s_in = pl.BlockSpec((1, bb, GD_HEADS, GD_DK, GD_DV), lambda i, j: (layer_j, i, 0, 0, 0))
    s_shape, s_spec, extra_specs, extra_args, wide = _state_out(
        n_layers, layer_j, bsz, bb, (GD_HEADS, GD_DK, GD_DV), idx, s_prev)
    n_in = 6
    return pl.pallas_call(
        functools.partial(_gdn_chunk_kernel, c=GD_CHUNK, wide=wide, aliased=not wide),
        out_shape=(jax.ShapeDtypeStruct((bsz, length, D_MODEL), BF16), s_shape),
        grid=(bsz // bb, length // GD_CHUNK),
        in_specs=[row_block(GD_MAIN), row_block(GATE_COLS), _layer_block(a_log, layer_j),
                  _layer_block(dt_bias, layer_j), _layer_block(g_out, layer_j), s_in, *extra_specs],
        out_specs=(row_block(D_MODEL), s_spec),
        input_output_aliases={} if wide else {n_in: 1},
        compiler_params=_params(), name="gdn_chunk",
    )(p, gates, a_log, dt_bias, g_out, s0, *extra_args)


def _gdn_call(kernel_fn, name, p, gates, conv_prev, s0, layer_j, conv_w, a_log, dt_bias, g_out, s_prev,
              grid, row_block, idx, bb, scratch):
    bsz = p.shape[0]
    n_layers = s0.shape[0]
    cp_in = pl.BlockSpec((1, bb, GD_CONV - 1, GD_QKV), lambda *g: (layer_j, idx(*g), 0, 0))
    s_in = pl.BlockSpec((1, bb, GD_HEADS, GD_DK, GD_DV), lambda *g: (layer_j, idx(*g), 0, 0, 0))
    nc_out = pl.BlockSpec((bb, GD_CONV - 1, GD_QKV), lambda *g: (idx(*g), 0, 0))
    s_shape, s_spec, extra_specs, extra_args, wide = _state_out(
        n_layers, layer_j, bsz, bb, (GD_HEADS, GD_DK, GD_DV), idx, s_prev)
    n_in = 8
    return pl.pallas_call(
        functools.partial(kernel_fn, wide=wide, aliased=not wide),
        out_shape=(jax.ShapeDtypeStruct(p.shape[:2] + (D_MODEL,), BF16), s_shape,
                   jax.ShapeDtypeStruct((bsz, GD_CONV - 1, GD_QKV), F32)),
        grid=grid,
        in_specs=[row_block(GD_MAIN), row_block(GATE_COLS), cp_in, _layer_block(conv_w, layer_j),
                  _layer_block(a_log, layer_j), _layer_block(dt_bias, layer_j), _layer_block(g_out, layer_j), s_in,
                  *extra_specs],
        out_specs=(row_block(D_MODEL), s_spec, nc_out),
        input_output_aliases={} if wide else {n_in: 1},
        scratch_shapes=scratch, compiler_params=_params(), name=name,
    )(p, gates, conv_prev, conv_w, a_log, dt_bias, g_out, s0, *extra_args)


def _gdn_step_kernel(p_ref, gt_ref, cp_ref, cw_ref, alog_ref, dtb_ref, gout_ref, s0_ref, *rest, wide, aliased):
    if aliased:
        rest = rest[1:]
    o_ref, s_ref, nc_ref = rest
    hk = GD_HEADS * GD_DK
    rows = lax.broadcasted_iota(jnp.int32, (SUBLANES, GD_DK), 0)
    if wide:
        _zero_other_layers(s_ref)
    convs, betas, gs = [], [], []
    for i in range(STEP_BATCH):
        qkv = p_ref[i, :, 0:GD_QKV]
        prev = cp_ref[0, i]
        conv = qkv * cw_ref[GD_CONV - 1:GD_CONV, :]
        for j in range(GD_CONV - 1):
            conv = conv + prev[j:j + 1, :] * cw_ref[j:j + 1, :]
        convs.append(_silu(conv))
        nc_ref[i] = jnp.concatenate([prev[1:GD_CONV - 1], qkv], axis=0)
        beta, log_g = _gdn_gates(gt_ref, i, alog_ref, dtb_ref)
        betas.append(beta)
        gs.append(jnp.exp(log_g))
    probs = [(i, h) for i in range(STEP_BATCH) for h in range(GD_HEADS)]
    n = range(len(probs))
    stack = lambda xs: jnp.concatenate(xs, axis=0)
    q_all = _l2n(stack([convs[i][:, h * GD_DK:(h + 1) * GD_DK] for i, h in probs])) * GD_DK ** -0.5
    k_all = _l2n(stack([convs[i][:, hk + h * GD_DK:hk + (h + 1) * GD_DK] for i, h in probs]))
    v_all = stack([convs[i][:, 2 * hk + h * GD_DV:2 * hk + (h + 1) * GD_DV] for i, h in probs])
    g_all = stack([gs[i][:, h:h + 1] for i, h in probs])
    b_all = stack([betas[i][:, h:h + 1] for i, h in probs])
    s_old = [s0_ref[0, i, h] for i, h in probs]
    kq_ss = [_dot(jnp.where(rows == 0, k_all[j:j + 1], jnp.where(rows == 1, q_all[j:j + 1], 0.0)).astype(BF16),
                  s_old[j].astype(BF16)) for j in n]
    u_all = b_all * (v_all - g_all * stack([kq_s[0:1] for kq_s in kq_ss]))
    for j, (i, h) in enumerate(probs):
        u8 = jnp.broadcast_to(u_all[j:j + 1], (SUBLANES, GD_DV)).astype(BF16)
        s_ref[0, i, h] = g_all[j:j + 1] * s_old[j] + _dot_tn(_row0(k_all[j:j + 1]).astype(BF16), u8)
    o_all = g_all * stack([kq_s[1:2] for kq_s in kq_ss]) + jnp.sum(q_all * k_all, axis=-1, keepdims=True) * u_all
    z_all = stack([p_ref[i, :, GD_QKV + h * GD_DV:GD_QKV + (h + 1) * GD_DV] for i, h in probs])
    out = _gdn_out(o_all, z_all, gout_ref[...]).astype(o_ref.dtype)
    for j, (i, h) in enumerate(probs):
        o_ref[i, :, h * GD_DV:(h + 1) * GD_DV] = out[j:j + 1]


def _gdn_step(p, gates, conv_prev, s0, layer_j, conv_w, a_log, dt_bias, g_out, s_prev):
    bsz = p.shape[0]
    bb = STEP_BATCH
    row_block = lambda width: pl.BlockSpec((bb, 1, width), lambda i: (i, 0, 0))
    return _gdn_call(_gdn_step_kernel, "gdn_step", p, gates, conv_prev, s0, layer_j,
                     conv_w, a_log, dt_bias, g_out, s_prev, (bsz // bb,), row_block, lambda i: i, bb, [])


def _lane_pad(x):
    return jnp.pad(x, [(0, 0)] * (x.ndim - 1) + [(0, LANES - x.shape[-1])])


def _prep_params(P):
    W = {}
    for name in ('ffn1_w_gate', 'ffn1_w_up', 'ffn1_w_down', 'ffn2_w_gate', 'ffn2_w_up', 'ffn2_w_down',
                 'ml_w_out', 'gd_w_out', 'xa_w_q', 'xa_w_k', 'xa_w_v', 'xa_w_o'):
        W[name] = P[name].astype(BF16)
    for name in ('g_ffn1', 'g_mix', 'g_xattn', 'g_mem', 'g_ffn2', 'ml_g_head', 'gd_g_out'):
        W[name] = P[name][:, None, :]
    for name in ('ml_b_i', 'ml_b_f', 'gd_a_log', 'gd_dt_bias'):
        W[name] = _lane_pad(P[name])[:, None, :]
    W['gd_conv_w'] = P['gd_conv_w']
    ml, gd = P['ml_w_in'], P['gd_w_in']
    W['ml_w_main'] = ml[..., :ML_MAIN].astype(BF16)
    W['ml_w_gate'] = jnp.concatenate(
        [_lane_pad(ml[..., ML_MAIN:ML_MAIN + ML_HEADS]), _lane_pad(ml[..., ML_MAIN + ML_HEADS:])], axis=-1).astype(BF16)
    W['gd_w_main'] = gd[..., :GD_MAIN].astype(BF16)
    W['gd_w_gate'] = jnp.concatenate(
        [_lane_pad(gd[..., GD_MAIN:GD_MAIN + GD_HEADS]), _lane_pad(gd[..., GD_MAIN + GD_HEADS:])], axis=-1).astype(BF16)
    return W


def _trunk(x, mem_k, mem_v, ml_c, ml_n, ml_m, gd_s, gd_conv, W):
    bsz, length, _ = x.shape
    single = length == 1
    x = x.reshape(bsz * length, D_MODEL)
    ml_m = _lane_pad(ml_m)[:, :, None, :]
    c_all = s_all = None
    new_n, new_m, new_conv = [], [], []
    y = None
    for layer in range(DEPTH):
        j = layer // 2
        x = _ffn(x, W['g_ffn1'], W['ffn1_w_gate'], W['ffn1_w_up'], W['ffn1_w_down'], layer)
        if layer % 2 == 0:
            p, gates = _inproj(x, W['g_mix'], layer, W['ml_w_main'], W['ml_w_gate'], j, BF16)
            fn = _mlstm_step if single else _mlstm_chunked
            a, c_all, n, m = fn(p.reshape(bsz, length, ML_MAIN), gates.reshape(bsz, length, GATE_COLS), ml_c, ml_n, ml_m,
                                j, W['ml_b_i'], W['ml_b_f'], W['ml_g_head'], c_all)
            new_n.append(n)
            new_m.append(m[:, 0, :ML_HEADS])
            w_out = W['ml_w_out']
        else:
            if single:
                p, gates = _inproj(x, W['g_mix'], layer, W['gd_w_main'], W['gd_w_gate'], j, F32)
                a, s_all, cv = _gdn_step(p.reshape(bsz, 1, GD_MAIN), gates.reshape(bsz, 1, GATE_COLS), gd_conv, gd_s,
                                         j, W['gd_conv_w'], W['gd_a_log'], W['gd_dt_bias'], W['gd_g_out'], s_all)
            else:
                p, gates, cv = _inproj_gdn(x, W['g_mix'], layer, W['gd_w_main'], W['gd_w_gate'], j, gd_conv,
                                           W['gd_conv_w'], bsz)
                a, s_all = _gdn_chunked(p.reshape(bsz, length, GD_MAIN), gates.reshape(bsz, length, GATE_COLS), gd_s,
                                        j, W['gd_a_log'], W['gd_dt_bias'], W['gd_g_out'], s_all)
            new_conv.append(cv)
            w_out = W['gd_w_out']
        if single:
            a = a.reshape(bsz, D_MODEL)
            x = _proj(a, w_out, j, res=x)
            q = _proj(x, W['xa_w_q'], layer, g=W['g_xattn'], g_index=layer)
            x = _proj(_attn_step(q, mem_k, mem_v, layer), W['xa_w_o'], layer, res=x)
        else:
            x = _xattn_block(x.reshape(bsz, length, D_MODEL), a, mem_k, mem_v, layer, W['g_xattn'], w_out, j,
                             W['xa_w_q'], W['xa_w_o']).reshape(bsz * length, D_MODEL)
        if layer == DEPTH - 1:
            x, y = _ffn(x, W['g_ffn2'], W['ffn2_w_gate'], W['ffn2_w_up'], W['ffn2_w_down'], layer,
                        g_final=W['g_final'])
        else:
            x = _ffn(x, W['g_ffn2'], W['ffn2_w_gate'], W['ffn2_w_up'], W['ffn2_w_down'], layer)
    return (y.reshape(bsz, length, D_MODEL), c_all, jnp.stack(new_n), jnp.stack(new_m), s_all, jnp.stack(new_conv))


def kernel(x_prompt, x_sample, mem_prompt, cache_mem_k, cache_mem_v, state_mlstm_C, state_mlstm_n, state_mlstm_m, state_gdn_S, state_gdn_conv, g_ffn1, ffn1_w_gate, ffn1_w_up, ffn1_w_down, g_mix, ml_w_in, ml_b_i, ml_b_f, ml_g_head, ml_w_out, gd_w_in, gd_conv_w, gd_a_log, gd_dt_bias, gd_g_out, gd_w_out, g_xattn, g_mem, xa_w_q, xa_w_k, xa_w_v, xa_w_o, g_ffn2, ffn2_w_gate, ffn2_w_up, ffn2_w_down, g_final):
    P = dict(g_ffn1=g_ffn1, ffn1_w_gate=ffn1_w_gate, ffn1_w_up=ffn1_w_up, ffn1_w_down=ffn1_w_down, g_mix=g_mix,
             ml_w_in=ml_w_in, ml_b_i=ml_b_i, ml_b_f=ml_b_f, ml_g_head=ml_g_head, ml_w_out=ml_w_out,
             gd_w_in=gd_w_in, gd_conv_w=gd_conv_w, gd_a_log=gd_a_log, gd_dt_bias=gd_dt_bias, gd_g_out=gd_g_out,
             gd_w_out=gd_w_out, g_xattn=g_xattn, g_mem=g_mem, xa_w_q=xa_w_q, xa_w_k=xa_w_k, xa_w_v=xa_w_v,
             xa_w_o=xa_w_o, g_ffn2=g_ffn2, ffn2_w_gate=ffn2_w_gate, ffn2_w_up=ffn2_w_up, ffn2_w_down=ffn2_w_down)
    W = _prep_params(P)
    W['g_final'] = g_final
    batch, n_mem, _ = mem_prompt.shape
    n_ml, n_gd = state_mlstm_C.shape[0], state_gdn_S.shape[0]

    pk, pv, pkb, pvb = _memkv(mem_prompt.reshape(batch * n_mem, D_MODEL), W['g_mem'], W['xa_w_k'], W['xa_w_v'])
    z_c = jnp.zeros((n_ml, batch, ML_HEADS, ML_DQK, ML_DV), F32)
    z_n = jnp.zeros((n_ml, batch, ML_HEADS, ML_DQK), F32)
    z_m = jnp.zeros((n_ml, batch, ML_HEADS), F32)
    z_s = jnp.zeros((n_gd, batch, GD_HEADS, GD_DK, GD_DV), F32)
    z_conv = jnp.zeros((n_gd, batch, GD_CONV - 1, GD_QKV), F32)
    y_p, p_c, p_n, p_m, p_s, p_conv = _trunk(
        x_prompt, pkb.reshape(DEPTH, batch, n_mem, D_MODEL), pvb.reshape(DEPTH, batch, n_mem, D_MODEL),
        z_c, z_n, z_m, z_s, z_conv, W)

    y_s, s_c, s_n, s_m, s_s, s_conv = _trunk(
        x_sample, cache_mem_k, cache_mem_v,
        state_mlstm_C, state_mlstm_n, state_mlstm_m, state_gdn_S, state_gdn_conv, W)

    kv_shape = (DEPTH, batch, n_mem, XA_HEADS, XA_DH)
    return (y_p, y_s, pk.reshape(kv_shape), pv.reshape(kv_shape), p_c, p_n, p_m, p_s, p_conv,
            s_c, s_n, s_m, s_s, s_conv)
```

```python
import functools

import jax
import jax.numpy as jnp
from jax import lax
from jax.experimental import pallas as pl
from jax.experimental.pallas import tpu as pltpu

F32 = jnp.float32
BF16 = jnp.bfloat16

D_MODEL = 1024
DEPTH = 4
N_MEM = 256
D_FF = 2816
FFN_RES = 0.5
EPS = 1e-6
ML_HEADS = 4
ML_DV = 256
ML_DQK = 128
ML_MAIN = 2 * ML_HEADS * ML_DQK + 2 * D_MODEL
GD_HEADS = 8
GD_DK = 128
GD_DV = 128
GD_CONV = 4
GD_QKV = 3072
GD_MAIN = GD_QKV + D_MODEL
XA_HEADS = 4
XA_DH = 256
ML_CHUNK = 256
GD_CHUNK = 128
GD_CHUNK_ROWS = 2
ML_CHUNK_ROWS = 2
LANES = 128
SUBLANES = 8
GATE_COLS = 2 * LANES
VMEM_LIMIT_BYTES = 56 * 1024 * 1024
HI = lax.Precision.HIGHEST


def _params():
    return pltpu.CompilerParams(vmem_limit_bytes=VMEM_LIMIT_BYTES)


def _rms(x, g):
    return x * lax.rsqrt(jnp.mean(x * x, axis=-1, keepdims=True) + EPS) * g


def _silu(x):
    return x * jax.nn.sigmoid(x)


def _softplus(x):
    return jnp.maximum(x, 0.0) + jnp.log1p(jnp.exp(-jnp.abs(x)))


def _log_sigmoid(x):
    return jnp.minimum(x, 0.0) - jnp.log1p(jnp.exp(-jnp.abs(x)))


def _dot(a, b):
    return jnp.dot(a, b, preferred_element_type=F32)


def _dot_nt(a, b):
    return lax.dot_general(a, b, (((1,), (1,)), ((), ())), preferred_element_type=F32)


def _dot_tn(a, b):
    return lax.dot_general(a, b, (((0,), (0,)), ((), ())), preferred_element_type=F32)


def _resident(shape):
    nd = len(shape)
    return pl.BlockSpec(shape, lambda *_: (0,) * nd, pipeline_mode=pl.Buffered(1))


def _layer_block(arr, index):
    nd = arr.ndim - 1
    return pl.BlockSpec((None,) + arr.shape[1:], lambda *_: (index,) + (0,) * nd, pipeline_mode=pl.Buffered(1))


def _ffn_kernel(x_ref, g_ref, wg_ref, wu_ref, wd_ref, *rest, fchunk, final):
    if final:
        gf_ref, o_ref, y_ref, h_ref, a_ref = rest
    else:
        o_ref, h_ref, a_ref = rest
    h_ref[...] = _rms(x_ref[...], g_ref[...]).astype(BF16)
    for j in range(D_FF // fchunk):
        sl = slice(j * fchunk, (j + 1) * fchunk)
        h = h_ref[...]
        gate = _dot(h, wg_ref[:, sl])
        up = _dot(h, wu_ref[:, sl])
        a_ref[:, sl] = (_silu(gate) * up).astype(BF16)
    out = x_ref[...] + FFN_RES * _dot(a_ref[...], wd_ref[...])
    o_ref[...] = out
    if final:
        y_ref[...] = _rms(out, gf_ref[...])


def _ffn(x, g, wg, wu, wd, layer, g_final=None):
    m = x.shape[0]
    tm = min(m, 1024)
    final = g_final is not None
    row = pl.BlockSpec((tm, D_MODEL), lambda i: (i, 0))
    in_specs = [row, _layer_block(g, layer), _layer_block(wg, layer), _layer_block(wu, layer), _layer_block(wd, layer)]
    args = [x, g, wg, wu, wd]
    out_shape = jax.ShapeDtypeStruct((m, D_MODEL), F32)
    out_specs = row
    if final:
        in_specs.append(_resident((1, D_MODEL)))
        args.append(g_final.reshape(1, D_MODEL))
        out_shape = (out_shape, out_shape)
        out_specs = (row, row)
    return pl.pallas_call(
        functools.partial(_ffn_kernel, fchunk=256, final=final),
        out_shape=out_shape, grid=(m // tm,), in_specs=in_specs, out_specs=out_specs,
        scratch_shapes=[pltpu.VMEM((tm, D_MODEL), BF16), pltpu.VMEM((tm, D_FF), BF16)],
        compiler_params=_params(), name="ffn_final" if final else "ffn",
    )(*args)


def _matmul_chunks(h, w_ref, o_ref, nchunk, res_ref=None):
    n = w_ref.shape[1]
    for n0 in range(0, n, nchunk):
        n1 = min(n, n0 + nchunk)
        y = _dot(h, w_ref[:, n0:n1])
        if res_ref is not None:
            y = y + res_ref[:, n0:n1]
        o_ref[:, n0:n1] = y.astype(o_ref.dtype)


def _proj_kernel(*refs, norm, res, nchunk):
    refs = list(refs)
    x_ref = refs.pop(0)
    g_ref = refs.pop(0) if norm else None
    w_ref = refs.pop(0)
    r_ref = refs.pop(0) if res else None
    o_ref = refs.pop(0)
    if norm:
        h = _rms(x_ref[...], g_ref[...]).astype(BF16)
    else:
        h = x_ref[...].astype(BF16)
    _matmul_chunks(h, w_ref, o_ref, nchunk, r_ref)


def _proj(x, w, w_index, g=None, g_index=None, res=None, out_dtype=F32):
    m, k = x.shape
    n = w.shape[-1]
    tm = min(m, 512)
    in_specs = [pl.BlockSpec((tm, k), lambda i: (i, 0))]
    args = [x]
    if g is not None:
        in_specs.append(_layer_block(g, g_index))
        args.append(g)
    in_specs.append(_layer_block(w, w_index))
    args.append(w)
    if res is not None:
        in_specs.append(pl.BlockSpec((tm, n), lambda i: (i, 0)))
        args.append(res)
    return pl.pallas_call(
        functools.partial(_proj_kernel, norm=g is not None, res=res is not None, nchunk=512),
        out_shape=jax.ShapeDtypeStruct((m, n), out_dtype), grid=(m // tm,), in_specs=in_specs,
        out_specs=pl.BlockSpec((tm, n), lambda i: (i, 0)),
        compiler_params=_params(), name="proj",
    )(*args)


def _inproj_kernel(x_ref, g_ref, wm_ref, wg_ref, om_ref, og_ref, *, nchunk):
    h = _rms(x_ref[...], g_ref[...]).astype(BF16)
    _matmul_chunks(h, wm_ref, om_ref, nchunk)
    og_ref[...] = _dot(h, wg_ref[...].astype(BF16))


def _inproj(x, g, g_index, w_main, w_gate, w_index, main_dtype):
    m = x.shape[0]
    n = w_main.shape[-1]
    tm = min(m, 1024)
    rows = lambda width: pl.BlockSpec((tm, width), lambda i: (i, 0))
    return pl.pallas_call(
        functools.partial(_inproj_kernel, nchunk=512),
        out_shape=(jax.ShapeDtypeStruct((m, n), main_dtype), jax.ShapeDtypeStruct((m, GATE_COLS), F32)),
        grid=(m // tm,),
        in_specs=[rows(D_MODEL), _layer_block(g, g_index), _layer_block(w_main, w_index), _layer_block(w_gate, w_index)],
        out_specs=(rows(n), rows(GATE_COLS)),
        compiler_params=_params(), name="inproj",
    )(x, g, w_main, w_gate)


def _memkv_kernel(x_ref, g_ref, wk_ref, wv_ref, k_ref, v_ref, kb_ref, vb_ref):
    h = _rms(x_ref[...], g_ref[...]).astype(BF16)
    k = _dot(h, wk_ref[...])
    v = _dot(h, wv_ref[...])
    kb_ref[0] = k.astype(BF16)
    vb_ref[0] = v.astype(BF16)
    for i in range(XA_HEADS):
        sl = slice(i * XA_DH, (i + 1) * XA_DH)
        k_ref[0, :, i, :] = k[:, sl]
        v_ref[0, :, i, :] = v[:, sl]


def _memkv(mem, g_mem, wk, wv):
    m = mem.shape[0]
    tm = 512
    w_spec = pl.BlockSpec((None, D_MODEL, D_MODEL), lambda l, i: (l, 0, 0))
    o_spec = pl.BlockSpec((1, tm, XA_HEADS, XA_DH), lambda l, i: (l, i, 0, 0))
    b_spec = pl.BlockSpec((1, tm, D_MODEL), lambda l, i: (l, i, 0))
    o_shape = jax.ShapeDtypeStruct((DEPTH, m, XA_HEADS, XA_DH), F32)
    b_shape = jax.ShapeDtypeStruct((DEPTH, m, D_MODEL), BF16)
    return pl.pallas_call(
        _memkv_kernel, out_shape=(o_shape, o_shape, b_shape, b_shape), grid=(DEPTH, m // tm),
        in_specs=[pl.BlockSpec((tm, D_MODEL), lambda l, i: (i, 0)),
                  pl.BlockSpec((None, 1, D_MODEL), lambda l, i: (l, 0, 0)), w_spec, w_spec],
        out_specs=(o_spec, o_spec, b_spec, b_spec), compiler_params=_params(), name="memkv",
    )(mem, g_mem, wk, wv)


def _xattn_kernel(x_ref, a_ref, k_ref, v_ref, g_ref, wout_ref, wq_ref, wo_ref, o_ref, att_ref):
    x2 = x_ref[0] + _dot(a_ref[0], wout_ref[...])
    q = _dot(_rms(x2, g_ref[...]).astype(BF16), wq_ref[...]).astype(BF16)
    for h in range(XA_HEADS):
        sl = slice(h * XA_DH, (h + 1) * XA_DH)
        s = _dot_nt(q[:, sl], k_ref[0, :, sl]) * XA_DH ** -0.5
        e = jnp.exp(s - jnp.max(s, axis=-1, keepdims=True))
        p = e / jnp.sum(e, axis=-1, keepdims=True)
        att_ref[:, sl] = _dot(p.astype(BF16), v_ref[0, :, sl]).astype(BF16)
    o_ref[0] = x2 + _dot(att_ref[...], wo_ref[...])


def _xattn_block(x, a, mem_kb, mem_vb, layer, g, w_out, w_out_index, wq, wo):
    b, l, _ = x.shape
    tq = min(l, 1024)
    row = pl.BlockSpec((1, tq, D_MODEL), lambda i, j: (i, j, 0))
    kv_spec = pl.BlockSpec((None, 1, N_MEM, D_MODEL), lambda i, j: (layer, i, 0, 0))
    return pl.pallas_call(
        _xattn_kernel, out_shape=jax.ShapeDtypeStruct((b, l, D_MODEL), F32), grid=(b, l // tq),
        in_specs=[row, row, kv_spec, kv_spec, _layer_block(g, layer), _layer_block(w_out, w_out_index),
                  _layer_block(wq, layer), _layer_block(wo, layer)],
        out_specs=row, scratch_shapes=[pltpu.VMEM((tq, D_MODEL), BF16)],
        compiler_params=_params(), name="xattn",
    )(x, a, mem_kb, mem_vb, g, w_out, wq, wo)


ATTN_STEP_BATCH = 8


def _attn_step_kernel(q_ref, k_ref, v_ref, o_ref):
    nh = XA_HEADS
    pairs = (N_MEM // 2, 2 * nh, XA_DH)
    for i in range(ATTN_STEP_BATCH):
        q = q_ref[0, i] * XA_DH ** -0.5
        q2 = jnp.concatenate([q, q], axis=0)
        s = jnp.sum(k_ref[0, i].reshape(pairs) * q2[None], axis=-1, keepdims=True)
        mx = jnp.max(s, axis=0, keepdims=True)
        mx = jnp.maximum(mx[:, :nh], mx[:, nh:])
        e = jnp.exp(s - jnp.concatenate([mx, mx], axis=1))
        acc = jnp.sum(e * v_ref[0, i].reshape(pairs), axis=0)
        den = jnp.sum(e, axis=0)
        o_ref[0, i] = (acc[:nh] + acc[nh:]) / (den[:nh] + den[nh:])


def _attn_step(q, mem_k, mem_v, layer):
    b = q.shape[0]
    bb = ATTN_STEP_BATCH
    kv_spec = pl.BlockSpec((1, bb, N_MEM, XA_HEADS, XA_DH), lambda i: (layer, i, 0, 0, 0))
    q_spec = pl.BlockSpec((1, bb, XA_HEADS, XA_DH), lambda i: (i, 0, 0, 0))
    out = pl.pallas_call(
        _attn_step_kernel, out_shape=jax.ShapeDtypeStruct((b // bb, bb, XA_HEADS, XA_DH), F32), grid=(b // bb,),
        in_specs=[q_spec, kv_spec, kv_spec], out_specs=q_spec,
        compiler_params=_params(), name="attn_step",
    )(q.reshape(b // bb, bb, XA_HEADS, XA_DH), mem_k, mem_v)
    return out.reshape(b, D_MODEL)


def _zero_other_layers(ref):
    ref[1:] = jnp.zeros((ref.shape[0] - 1,) + ref.shape[1:], F32)


def _state_out(n_layers, layer_j, bsz, bb, tail, idx, prev):
    shape = jax.ShapeDtypeStruct((n_layers, bsz) + tail, F32)
    zeros = (0,) * len(tail)
    if prev is None:
        spec = pl.BlockSpec((n_layers, bb) + tail, lambda *g: (0, idx(*g)) + zeros)
        return shape, spec, [], [], True
    spec = pl.BlockSpec((1, bb) + tail, lambda *g: (layer_j, idx(*g)) + zeros)
    return shape, spec, [pl.BlockSpec(memory_space=pl.ANY)], [prev], False


def _lane_select(h, value, into):
    lane = lax.broadcasted_iota(jnp.int32, into.shape, 1)
    return jnp.where(lane == h, value, into)


def _mlstm_chunk_kernel(p_ref, gt_ref, c0_ref, n0_ref, m0_ref, bi_ref, bf_ref, gh_ref, *rest, c, wide, aliased):
    if aliased:
        rest = rest[1:]
    o_ref, c_ref, n_ref, m_ref = rest
    rows = range(ML_CHUNK_ROWS)

    @pl.when(pl.program_id(1) == 0)
    def _():
        for r in rows:
            c_ref[0, r] = c0_ref[0, r]
            n_ref[r] = n0_ref[0, r]
            m_ref[r] = m0_ref[0, r]
        if wide:
            _zero_other_layers(c_ref)

    hq = ML_HEADS * ML_DQK
    row = lax.broadcasted_iota(jnp.int32, (c, c), 0)
    col = lax.broadcasted_iota(jnp.int32, (c, c), 1)
    incl = row >= col
    scale = ML_DQK ** -0.5
    i_pres, bs, xts, m_prevs = [], [], [], []
    for r in rows:
        i_pre = gt_ref[r, :, 0:LANES] + bi_ref[...]
        log_f = _log_sigmoid(gt_ref[r, :, LANES:GATE_COLS] + bf_ref[...])
        b = jnp.dot(incl.astype(F32), log_f, precision=HI, preferred_element_type=F32)
        i_pres.append(i_pre)
        bs.append(b)
        xts.append((i_pre - b).T)
        m_prevs.append(m_ref[r])
    probs = [(r, h) for r in rows for h in range(ML_HEADS)]
    n = range(len(probs))
    bcols = [bs[r][:, h:h + 1] for r, h in probs]
    icols = [i_pres[r][:, h:h + 1] for r, h in probs]
    mprev = [m_prevs[r][:, h:h + 1] for r, h in probs]
    qbs = [p_ref[r, :, h * ML_DQK:(h + 1) * ML_DQK] for r, h in probs]
    kbs = [p_ref[r, :, hq + h * ML_DQK:hq + (h + 1) * ML_DQK] for r, h in probs]
    vbs = [p_ref[r, :, 2 * hq + h * ML_DV:2 * hq + (h + 1) * ML_DV] for r, h in probs]
    dms = [jnp.where(incl, bcols[i] + xts[r][h:h + 1, :], -jnp.inf) for i, (r, h) in enumerate(probs)]
    mts = [jnp.maximum(bcols[i] + mprev[i], jnp.max(dms[i], axis=-1, keepdims=True)) for i in n]
    w_inters = [jnp.exp(bcols[i] + mprev[i] - mts[i]) for i in n]
    ss = [_dot_nt(qbs[i], kbs[i]) * (jnp.exp(dms[i] - mts[i]) * scale) for i in n]
    c_old = [c_ref[0, r, h] for r, h in probs]
    n_old = [n_ref[r, h:h + 1, :] for r, h in probs]
    nums = [w_inters[i] * _dot(qbs[i], c_old[i].astype(BF16)) + _dot(ss[i].astype(BF16), vbs[i]) for i in n]
    m_rows = list(m_prevs)
    for i, (r, h) in enumerate(probs):
        m_new = mts[i][c - 1:c, :]
        b_last = bcols[i][c - 1:c, :]
        w_k = jnp.exp(b_last - bcols[i] + icols[i] - m_new) * scale
        decay = jnp.exp(b_last + mprev[i] - m_new)
        kw = kbs[i].astype(F32) * w_k
        c_ref[0, r, h] = decay * c_old[i] + _dot_tn(kw.astype(BF16), vbs[i])
        n_ref[r, h:h + 1, :] = decay * n_old[i] + jnp.sum(kw, axis=0, keepdims=True)
        m_rows[r] = _lane_select(h, m_new, m_rows[r])
    for r in rows:
        m_ref[r] = m_rows[r]
    for i, (r, h) in enumerate(probs):
        den = (w_inters[i] * jnp.sum(qbs[i].astype(F32) * n_old[i], axis=-1, keepdims=True)
               + jnp.sum(ss[i], axis=-1, keepdims=True))
        hh = nums[i] / jnp.maximum(jnp.abs(den), jnp.exp(-mts[i]))
        vs = slice(h * ML_DV, (h + 1) * ML_DV)
        hn = hh * lax.rsqrt(jnp.mean(hh * hh, axis=-1, keepdims=True) + EPS) * gh_ref[:, vs]
        og = p_ref[r, :, 2 * hq + D_MODEL + h * ML_DV:2 * hq + D_MODEL + (h + 1) * ML_DV].astype(F32)
        o_ref[r, :, vs] = (hn * jax.nn.sigmoid(og)).astype(o_ref.dtype)


def _mlstm_small_specs(bsz, layer_j, idx, bb):
    c_in = pl.BlockSpec((1, bb, ML_HEADS, ML_DQK, ML_DV), lambda *g: (layer_j, idx(*g), 0, 0, 0))
    n_in = pl.BlockSpec((1, bb, ML_HEADS, ML_DQK), lambda *g: (layer_j, idx(*g), 0, 0))
    m_in = pl.BlockSpec((1, bb, 1, LANES), lambda *g: (layer_j, idx(*g), 0, 0))
    n_out = pl.BlockSpec((bb, ML_HEADS, ML_DQK), lambda *g: (idx(*g), 0, 0))
    m_out = pl.BlockSpec((bb, 1, LANES), lambda *g: (idx(*g), 0, 0))
    shapes = (jax.ShapeDtypeStruct((bsz, ML_HEADS, ML_DQK), F32), jax.ShapeDtypeStruct((bsz, 1, LANES), F32))
    return (c_in, n_in, m_in), (n_out, m_out), shapes


def _mlstm_call(kernel_fn, name, p, gates, c0, n0, m0, layer_j, b_i, b_f, g_head, c_prev, grid, row_block, idx, bb):
    bsz = p.shape[0]
    n_layers = c0.shape[0]
    s_in, s_out, s_shapes = _mlstm_small_specs(bsz, layer_j, idx, bb)
    c_shape, c_spec, extra_specs, extra_args, wide = _state_out(
        n_layers, layer_j, bsz, bb, (ML_HEADS, ML_DQK, ML_DV), idx, c_prev)
    n_in = 8
    return pl.pallas_call(
        functools.partial(kernel_fn, wide=wide, aliased=not wide),
        out_shape=(jax.ShapeDtypeStruct(p.shape[:2] + (D_MODEL,), BF16), c_shape) + s_shapes,
        grid=grid,
        in_specs=[row_block(ML_MAIN), row_block(GATE_COLS), *s_in, _layer_block(b_i, layer_j),
                  _layer_block(b_f, layer_j), _layer_block(g_head, layer_j), *extra_specs],
        out_specs=(row_block(D_MODEL), c_spec) + s_out,
        input_output_aliases={} if wide else {n_in: 1},
        compiler_params=_params(), name=name,
    )(p, gates, c0, n0, m0, b_i, b_f, g_head, *extra_args)


def _mlstm_chunked(p, gates, c0, n0, m0, layer_j, b_i, b_f, g_head, c_prev):
    bsz, length, _ = p.shape
    bb = ML_CHUNK_ROWS
    row_block = lambda width: pl.BlockSpec((bb, ML_CHUNK, width), lambda i, j: (i, j, 0))
    return _mlstm_call(functools.partial(_mlstm_chunk_kernel, c=ML_CHUNK), "mlstm_chunk", p, gates, c0, n0, m0, layer_j,
                       b_i, b_f, g_head, c_prev, (bsz // bb, length // ML_CHUNK), row_block, lambda i, j: i, bb)


STEP_BATCH = 4


def _row0(x, rows=SUBLANES):
    r = lax.broadcasted_iota(jnp.int32, (rows, x.shape[1]), 0)
    return jnp.where(r == 0, x, 0.0)


def _mlstm_step_kernel(p_ref, gt_ref, c0_ref, n0_ref, m0_ref, bi_ref, bf_ref, gh_ref, *rest, wide, aliased):
    if aliased:
        rest = rest[1:]
    o_ref, c_ref, n_ref, m_ref = rest
    hq = ML_HEADS * ML_DQK
    if wide:
        _zero_other_layers(c_ref)
    w_inters, w_ks, e_invs = [], [], []
    for i in range(STEP_BATCH):
        i_pre = gt_ref[i, :, 0:LANES] + bi_ref[...]
        log_f = _log_sigmoid(gt_ref[i, :, LANES:GATE_COLS] + bf_ref[...])
        m_prev = m0_ref[0, i]
        mt = jnp.maximum(log_f + m_prev, i_pre)
        w_inters.append(jnp.exp(log_f + m_prev - mt))
        w_ks.append(jnp.exp(i_pre - mt))
        e_invs.append(jnp.exp(-mt))
        m_ref[i] = mt
    probs = [(i, h) for i in range(STEP_BATCH) for h in range(ML_HEADS)]
    n = range(len(probs))
    qs = [p_ref[i, :, h * ML_DQK:(h + 1) * ML_DQK].astype(F32) for i, h in probs]
    ks = [p_ref[i, :, hq + h * ML_DQK:hq + (h + 1) * ML_DQK].astype(F32) * ML_DQK ** -0.5 for i, h in probs]
    vs = [p_ref[i, :, 2 * hq + h * ML_DV:2 * hq + (h + 1) * ML_DV].astype(F32) for i, h in probs]
    c_old = [c0_ref[0, i, h] for i, h in probs]
    n_old = [n0_ref[0, i, h:h + 1, :] for i, h in probs]
    wis = [w_inters[i][:, h:h + 1] for i, h in probs]
    wks = [w_ks[i][:, h:h + 1] for i, h in probs]
    q_cs = [_dot(_row0(qs[j]).astype(BF16), c_old[j].astype(BF16))[0:1] for j in n]
    kws = [ks[j] * wks[j] for j in n]
    for j, (i, h) in enumerate(probs):
        v8 = jnp.broadcast_to(vs[j], (SUBLANES, ML_DV)).astype(BF16)
        c_ref[0, i, h] = wis[j] * c_old[j] + _dot_tn(_row0(kws[j]).astype(BF16), v8)
        n_ref[i, h:h + 1, :] = wis[j] * n_old[j] + kws[j]
    stack = lambda xs: jnp.concatenate(xs, axis=0)
    sls = [slice(h * ML_DV, (h + 1) * ML_DV) for _, h in probs]
    q_all, k_all, v_all, wi_all, wk_all = stack(qs), stack(ks), stack(vs), stack(wis), stack(wks)
    s = jnp.sum(q_all * k_all, axis=-1, keepdims=True) * wk_all
    num = wi_all * stack(q_cs) + s * v_all
    den = wi_all * jnp.sum(q_all * stack(n_old), axis=-1, keepdims=True) + s
    hh = num / jnp.maximum(jnp.abs(den), stack([e_invs[i][:, h:h + 1] for i, h in probs]))
    hn = hh * lax.rsqrt(jnp.mean(hh * hh, axis=-1, keepdims=True) + EPS) * stack([gh_ref[:, sl] for sl in sls])
    og = stack([p_ref[i, :, 2 * hq + D_MODEL + h * ML_DV:2 * hq + D_MODEL + (h + 1) * ML_DV] for i, h in probs])
    out = (hn * jax.nn.sigmoid(og.astype(F32))).astype(o_ref.dtype)
    for j, (i, h) in enumerate(probs):
        o_ref[i, :, sls[j]] = out[j:j + 1]


def _mlstm_step(p, gates, c0, n0, m0, layer_j, b_i, b_f, g_head, c_prev):
    bsz = p.shape[0]
    bb = STEP_BATCH
    row_block = lambda width: pl.BlockSpec((bb, 1, width), lambda i: (i, 0, 0))
    return _mlstm_call(_mlstm_step_kernel, "mlstm_step", p, gates, c0, n0, m0, layer_j,
                       b_i, b_f, g_head, c_prev, (bsz // bb,), row_block, lambda i: i, bb)


def _l2n(x):
    return x * lax.rsqrt(jnp.sum(x * x, axis=-1, keepdims=True) + EPS)


INV_BASE = 16


def _unit_lower_inverses(a_list, c):
    row = lax.broadcasted_iota(jnp.int32, (c, c), 0)
    col = lax.broadcasted_iota(jnp.int32, (c, c), 1)
    eye = jnp.where(row == col, 1.0, 0.0)
    same = lambda size: (row >> (size.bit_length() - 1)) == (col >> (size.bit_length() - 1))
    ns = [jnp.where(same(INV_BASE), -a, 0.0) for a in a_list]
    ts = [eye + n for n in ns]
    nbs = [n.astype(BF16) for n in ns]
    power = 2
    while power < INV_BASE:
        ns = [_dot(nb, nb) for nb in nbs]
        nbs = [n.astype(BF16) for n in ns]
        ts = [t + _dot(t.astype(BF16), nb) for t, nb in zip(ts, nbs)]
        power *= 2
    size = INV_BASE
    while size < c:
        off = same(2 * size) & jnp.logical_not(same(size))
        tbs = [t.astype(BF16) for t in ts]
        mids = [_dot(tb, jnp.where(off, a, 0.0).astype(BF16)).astype(BF16) for tb, a in zip(tbs, a_list)]
        ts = [t - _dot(mid, tb) for t, mid, tb in zip(ts, mids, tbs)]
        size *= 2
    return ts


def _gdn_gates(gt_ref, i, alog_ref, dtb_ref):
    beta = jax.nn.sigmoid(gt_ref[i, :, 0:LANES])
    log_g = -jnp.exp(alog_ref[...]) * _softplus(gt_ref[i, :, LANES:GATE_COLS] + dtb_ref[...])
    return beta, log_g


def _gdn_out(o, z, gout):
    return (o * lax.rsqrt(jnp.mean(o * o, axis=-1, keepdims=True) + EPS) * gout * _silu(z))


def _inproj_gdn_kernel(x_ref, g_ref, wm_ref, wg_ref, cp_ref, cw_ref, om_ref, og_ref, nc_ref, e_ref,
                       *, tiles_per_seq, nchunk):
    assert GD_CONV == 4, "the conv below pairs four taps"
    tm = x_ref.shape[0]
    pad = SUBLANES
    keep = GD_CONV - 1
    h = _rms(x_ref[...], g_ref[...]).astype(BF16)
    og_ref[...] = _dot(h, wg_ref[...].astype(BF16))

    @pl.when(pl.program_id(0) % tiles_per_seq == 0)
    def _():
        e_ref[...] = jnp.zeros((pad, GD_QKV), F32)
        e_ref[pl.ds(pad - keep, keep), :] = cp_ref[0, 0]

    hk = GD_HEADS * GD_DK
    starts = list(range(0, GD_QKV, nchunk))
    z_starts = list(range(GD_QKV, GD_MAIN, nchunk))
    every = len(starts) // len(z_starts)
    for idx, n0 in enumerate(starts):
        if idx % every == 0:
            z0 = z_starts[idx // every]
            om_ref[:, z0:z0 + nchunk] = _dot(h, wm_ref[:, z0:z0 + nchunk]).astype(om_ref.dtype)
        tile = _dot(h, wm_ref[:, n0:n0 + nchunk])
        ext = jnp.concatenate([e_ref[:, n0:n0 + nchunk], tile], axis=0)
        nc_ref[0, :, n0:n0 + nchunk] = tile[tm - keep:]
        e_ref[:, n0:n0 + nchunk] = tile[tm - pad:]
        for d0 in range(0, nchunk, GD_DK):
            sl = slice(n0 + d0, n0 + d0 + GD_DK)
            x = ext[:, d0:d0 + GD_DK]
            x1 = pltpu.roll(x, 1, 0)
            near = x * cw_ref[3:4, sl] + x1 * cw_ref[2:3, sl]
            far = x * cw_ref[1:2, sl] + x1 * cw_ref[0:1, sl]
            conv = _silu((near + pltpu.roll(far, 2, 0))[pad:])
            if n0 + d0 < hk:
                conv = _l2n(conv) * GD_DK ** -0.5
            elif n0 + d0 < 2 * hk:
                conv = _l2n(conv)
            om_ref[:, sl] = conv.astype(om_ref.dtype)


def _inproj_gdn(x, g, g_index, w_main, w_gate, w_index, conv_prev, conv_w, bsz):
    m = x.shape[0]
    tm = 1024
    tiles_per_seq = m // bsz // tm
    rows = lambda width: pl.BlockSpec((tm, width), lambda i: (i, 0))
    return pl.pallas_call(
        functools.partial(_inproj_gdn_kernel, tiles_per_seq=tiles_per_seq, nchunk=256),
        out_shape=(jax.ShapeDtypeStruct((m, GD_MAIN), BF16), jax.ShapeDtypeStruct((m, GATE_COLS), F32),
                   jax.ShapeDtypeStruct((bsz, GD_CONV - 1, GD_QKV), F32)),
        grid=(m // tm,),
        in_specs=[rows(D_MODEL), _layer_block(g, g_index), _layer_block(w_main, w_index), _layer_block(w_gate, w_index),
                  pl.BlockSpec((1, 1, GD_CONV - 1, GD_QKV), lambda i: (w_index, i // tiles_per_seq, 0, 0)),
                  _layer_block(conv_w, w_index)],
        out_specs=(rows(GD_MAIN), rows(GATE_COLS),
                   pl.BlockSpec((1, GD_CONV - 1, GD_QKV), lambda i: (i // tiles_per_seq, 0, 0))),
        scratch_shapes=[pltpu.VMEM((SUBLANES, GD_QKV), F32)],
        compiler_params=_params(), name="inproj_gdn",
    )(x, g, w_main, w_gate, conv_prev, conv_w)


def _gdn_chunk_kernel(p_ref, gt_ref, alog_ref, dtb_ref, gout_ref, s0_ref, *rest, c, wide, aliased):
    if aliased:
        rest = rest[1:]
    o_ref, s_ref = rest
    rows = range(GD_CHUNK_ROWS)

    @pl.when(pl.program_id(1) == 0)
    def _():
        for r in rows:
            s_ref[0, r] = s0_ref[0, r]
        if wide:
            _zero_other_layers(s_ref)

    row = lax.broadcasted_iota(jnp.int32, (c, c), 0)
    col = lax.broadcasted_iota(jnp.int32, (c, c), 1)
    incl = row >= col
    strict = row > col
    hk = GD_HEADS * GD_DK
    betas, gams, gam_ts, e_gams = [], [], [], []
    for r in rows:
        beta, log_g = _gdn_gates(gt_ref, r, alog_ref, dtb_ref)
        gam = jnp.dot(incl.astype(F32), log_g, precision=HI, preferred_element_type=F32)
        betas.append(beta)
        gams.append(gam)
        gam_ts.append(gam.T)
        e_gams.append(jnp.exp(gam))
    probs = [(r, h) for r in rows for h in range(GD_HEADS)]
    col_of = lambda xs, r, h: xs[r][:, h:h + 1]
    qbs = [p_ref[r, :, h * GD_DK:(h + 1) * GD_DK] for r, h in probs]
    kbs = [p_ref[r, :, hk + h * GD_DK:hk + (h + 1) * GD_DK] for r, h in probs]
    vs = [p_ref[r, :, 2 * hk + h * GD_DV:2 * hk + (h + 1) * GD_DV].astype(F32) for r, h in probs]
    ks = [kb.astype(F32) for kb in kbs]
    bcols = [col_of(betas, r, h) for r, h in probs]
    gcols = [col_of(gams, r, h) for r, h in probs]
    ecols = [col_of(e_gams, r, h) for r, h in probs]
    n = range(len(probs))
    decs = [jnp.exp(jnp.where(incl, gcols[i] - gam_ts[r][h:h + 1, :], -jnp.inf)) for i, (r, h) in enumerate(probs)]
    kqs = [_dot_nt(jnp.concatenate([kbs[i], qbs[i]], axis=0), kbs[i]) for i in n]
    a_list = [jnp.where(strict, bcols[i] * kqs[i][0:c] * decs[i], 0.0) for i in n]
    ts = _unit_lower_inverses(a_list, c)
    rhs = [jnp.concatenate([bcols[i] * vs[i], (bcols[i] * ecols[i]) * ks[i]], axis=1).astype(BF16) for i in n]
    uws = [_dot(ts[i].astype(BF16), rhs[i]) for i in n]
    s_old = [s_ref[0, r, h] for r, h in probs]
    wq = [jnp.concatenate([uws[i][:, GD_DV:], qbs[i].astype(F32) * ecols[i]], axis=0).astype(BF16) for i in n]
    wqs = [_dot(wq[i], s_old[i].astype(BF16)) for i in n]
    ubs = [(uws[i][:, :GD_DV] - wqs[i][0:c]).astype(BF16) for i in n]
    for i, (r, h) in enumerate(probs):
        g_last = gcols[i][c - 1:c, :]
        k_dec = ks[i] * jnp.exp(g_last - gcols[i])
        s_ref[0, r, h] = jnp.exp(g_last) * s_old[i] + _dot_tn(k_dec.astype(BF16), ubs[i])
    for i, (r, h) in enumerate(probs):
        o = wqs[i][c:2 * c] + _dot((kqs[i][c:2 * c] * decs[i]).astype(BF16), ubs[i])
        z = p_ref[r, :, GD_QKV + h * GD_DV:GD_QKV + (h + 1) * GD_DV].astype(F32)
        o_ref[r, :, h * GD_DV:(h + 1) * GD_DV] = _gdn_out(o, z, gout_ref[...]).astype(o_ref.dtype)


def _gdn_chunked(p, gates, s0, layer_j, a_log, dt_bias, g_out, s_prev):
    bsz, length, _ = p.shape
    n_layers = s0.shape[0]
    bb = GD_CHUNK_ROWS
    idx = lambda i, j: i
    row_block = lambda width: pl.BlockSpec((bb, GD_CHUNK, width), lambda i, j: (i, j, 0))
    s_in = pl.BlockSpec((1, bb, GD_HEADS, GD_DK, GD_DV), lambda i, j: (layer_j, i, 0, 0, 0))
    s_shape, s_spec, extra_specs, extra_args, wide = _state_out(
        n_layers, layer_j, bsz, bb, (GD_HEADS, GD_DK, GD_DV), idx, s_prev)
    n_in = 6
    return pl.pallas_call(
        functools.partial(_gdn_chunk_kernel, c=GD_CHUNK, wide=wide, aliased=not wide),
        out_shape=(jax.ShapeDtypeStruct((bsz, length, D_MODEL), BF16), s_shape),
        grid=(bsz // bb, length // GD_CHUNK),
        in_specs=[row_block(GD_MAIN), row_block(GATE_COLS), _layer_block(a_log, layer_j),
                  _layer_block(dt_bias, layer_j), _layer_block(g_out, layer_j), s_in, *extra_specs],
        out_specs=(row_block(D_MODEL), s_spec),
        input_output_aliases={} if wide else {n_in: 1},
        compiler_params=_params(), name="gdn_chunk",
    )(p, gates, a_log, dt_bias, g_out, s0, *extra_args)


def _gdn_call(kernel_fn, name, p, gates, conv_prev, s0, layer_j, conv_w, a_log, dt_bias, g_out, s_prev,
              grid, row_block, idx, bb, scratch):
    bsz = p.shape[0]
    n_layers = s0.shape[0]
    cp_in = pl.BlockSpec((1, bb, GD_CONV - 1, GD_QKV), lambda *g: (layer_j, idx(*g), 0, 0))
    s_in = pl.BlockSpec((1, bb, GD_HEADS, GD_DK, GD_DV), lambda *g: (layer_j, idx(*g), 0, 0, 0))
    nc_out = pl.BlockSpec((bb, GD_CONV - 1, GD_QKV), lambda *g: (idx(*g), 0, 0))
    s_shape, s_spec, extra_specs, extra_args, wide = _state_out(
        n_layers, layer_j, bsz, bb, (GD_HEADS, GD_DK, GD_DV), idx, s_prev)
    n_in = 8
    return pl.pallas_call(
        functools.partial(kernel_fn, wide=wide, aliased=not wide),
        out_shape=(jax.ShapeDtypeStruct(p.shape[:2] + (D_MODEL,), BF16), s_shape,
                   jax.ShapeDtypeStruct((bsz, GD_CONV - 1, GD_QKV), F32)),
        grid=grid,
        in_specs=[row_block(GD_MAIN), row_block(GATE_COLS), cp_in, _layer_block(conv_w, layer_j),
                  _layer_block(a_log, layer_j), _layer_block(dt_bias, layer_j), _layer_block(g_out, layer_j), s_in,
                  *extra_specs],
        out_specs=(row_block(D_MODEL), s_spec, nc_out),
        input_output_aliases={} if wide else {n_in: 1},
        scratch_shapes=scratch, compiler_params=_params(), name=name,
    )(p, gates, conv_prev, conv_w, a_log, dt_bias, g_out, s0, *extra_args)


def _gdn_step_kernel(p_ref, gt_ref, cp_ref, cw_ref, alog_ref, dtb_ref, gout_ref, s0_ref, *rest, wide, aliased):
    if aliased:
        rest = rest[1:]
    o_ref, s_ref, nc_ref = rest
    hk = GD_HEADS * GD_DK
    rows = lax.broadcasted_iota(jnp.int32, (SUBLANES, GD_DK), 0)
    if wide:
        _zero_other_layers(s_ref)
    convs, betas, gs = [], [], []
    for i in range(STEP_BATCH):
        qkv = p_ref[i, :, 0:GD_QKV]
        prev = cp_ref[0, i]
        conv = qkv * cw_ref[GD_CONV - 1:GD_CONV, :]
        for j in range(GD_CONV - 1):
            conv = conv + prev[j:j + 1, :] * cw_ref[j:j + 1, :]
        convs.append(_silu(conv))
        nc_ref[i] = jnp.concatenate([prev[1:GD_CONV - 1], qkv], axis=0)
        beta, log_g = _gdn_gates(gt_ref, i, alog_ref, dtb_ref)
        betas.append(beta)
        gs.append(jnp.exp(log_g))
    probs = [(i, h) for i in range(STEP_BATCH) for h in range(GD_HEADS)]
    n = range(len(probs))
    stack = lambda xs: jnp.concatenate(xs, axis=0)
    q_all = _l2n(stack([convs[i][:, h * GD_DK:(h + 1) * GD_DK] for i, h in probs])) * GD_DK ** -0.5
    k_all = _l2n(stack([convs[i][:, hk + h * GD_DK:hk + (h + 1) * GD_DK] for i, h in probs]))
    v_all = stack([convs[i][:, 2 * hk + h * GD_DV:2 * hk + (h + 1) * GD_DV] for i, h in probs])
    g_all = stack([gs[i][:, h:h + 1] for i, h in probs])
    b_all = stack([betas[i][:, h:h + 1] for i, h in probs])
    s_old = [s0_ref[0, i, h] for i, h in probs]
    kq_ss = [_dot(jnp.where(rows == 0, k_all[j:j + 1], jnp.where(rows == 1, q_all[j:j + 1], 0.0)).astype(BF16),
                  s_old[j].astype(BF16)) for j in n]
    u_all = b_all * (v_all - g_all * stack([kq_s[0:1] for kq_s in kq_ss]))
    for j, (i, h) in enumerate(probs):
        u8 = jnp.broadcast_to(u_all[j:j + 1], (SUBLANES, GD_DV)).astype(BF16)
        s_ref[0, i, h] = g_all[j:j + 1] * s_old[j] + _dot_tn(_row0(k_all[j:j + 1]).astype(BF16), u8)
    o_all = g_all * stack([kq_s[1:2] for kq_s in kq_ss]) + jnp.sum(q_all * k_all, axis=-1, keepdims=True) * u_all
    z_all = stack([p_ref[i, :, GD_QKV + h * GD_DV:GD_QKV + (h + 1) * GD_DV] for i, h in probs])
    out = _gdn_out(o_all, z_all, gout_ref[...]).astype(o_ref.dtype)
    for j, (i, h) in enumerate(probs):
        o_ref[i, :, h * GD_DV:(h + 1) * GD_DV] = out[j:j + 1]


def _gdn_step(p, gates, conv_prev, s0, layer_j, conv_w, a_log, dt_bias, g_out, s_prev):
    bsz = p.shape[0]
    bb = STEP_BATCH
    row_block = lambda width: pl.BlockSpec((bb, 1, width), lambda i: (i, 0, 0))
    return _gdn_call(_gdn_step_kernel, "gdn_step", p, gates, conv_prev, s0, layer_j,
                     conv_w, a_log, dt_bias, g_out, s_prev, (bsz // bb,), row_block, lambda i: i, bb, [])


def _lane_pad(x):
    return jnp.pad(x, [(0, 0)] * (x.ndim - 1) + [(0, LANES - x.shape[-1])])


def _prep_params(P):
    W = {}
    for name in ('ffn1_w_gate', 'ffn1_w_up', 'ffn1_w_down', 'ffn2_w_gate', 'ffn2_w_up', 'ffn2_w_down',
                 'ml_w_out', 'gd_w_out', 'xa_w_q', 'xa_w_k', 'xa_w_v', 'xa_w_o'):
        W[name] = P[name].astype(BF16)
    for name in ('g_ffn1', 'g_mix', 'g_xattn', 'g_mem', 'g_ffn2', 'ml_g_head', 'gd_g_out'):
        W[name] = P[name][:, None, :]
    for name in ('ml_b_i', 'ml_b_f', 'gd_a_log', 'gd_dt_bias'):
        W[name] = _lane_pad(P[name])[:, None, :]
    W['gd_conv_w'] = P['gd_conv_w']
    ml, gd = P['ml_w_in'], P['gd_w_in']
    W['ml_w_main'] = ml[..., :ML_MAIN].astype(BF16)
    W['ml_w_gate'] = jnp.concatenate(
        [_lane_pad(ml[..., ML_MAIN:ML_MAIN + ML_HEADS]), _lane_pad(ml[..., ML_MAIN + ML_HEADS:])], axis=-1)
    W['gd_w_main'] = gd[..., :GD_MAIN].astype(BF16)
    W['gd_w_gate'] = jnp.concatenate(
        [_lane_pad(gd[..., GD_MAIN:GD_MAIN + GD_HEADS]), _lane_pad(gd[..., GD_MAIN + GD_HEADS:])], axis=-1)
    return W


def _trunk(x, mem_k, mem_v, ml_c, ml_n, ml_m, gd_s, gd_conv, W):
    bsz, length, _ = x.shape
    single = length == 1
    x = x.reshape(bsz * length, D_MODEL)
    ml_m = _lane_pad(ml_m)[:, :, None, :]
    c_all = s_all = None
    new_n, new_m, new_conv = [], [], []
    y = None
    for layer in range(DEPTH):
        j = layer // 2
        x = _ffn(x, W['g_ffn1'], W['ffn1_w_gate'], W['ffn1_w_up'], W['ffn1_w_down'], layer)
        if layer % 2 == 0:
            p, gates = _inproj(x, W['g_mix'], layer, W['ml_w_main'], W['ml_w_gate'], j, BF16)
            fn = _mlstm_step if single else _mlstm_chunked
            a, c_all, n, m = fn(p.reshape(bsz, length, ML_MAIN), gates.reshape(bsz, length, GATE_COLS), ml_c, ml_n, ml_m,
                                j, W['ml_b_i'], W['ml_b_f'], W['ml_g_head'], c_all)
            new_n.append(n)
            new_m.append(m[:, 0, :ML_HEADS])
            w_out = W['ml_w_out']
        else:
            if single:
                p, gates = _inproj(x, W['g_mix'], layer, W['gd_w_main'], W['gd_w_gate'], j, F32)
                a, s_all, cv = _gdn_step(p.reshape(bsz, 1, GD_MAIN), gates.reshape(bsz, 1, GATE_COLS), gd_conv, gd_s,
                                         j, W['gd_conv_w'], W['gd_a_log'], W['gd_dt_bias'], W['gd_g_out'], s_all)
            else:
                p, gates, cv = _inproj_gdn(x, W['g_mix'], layer, W['gd_w_main'], W['gd_w_gate'], j, gd_conv,
                                           W['gd_conv_w'], bsz)
                a, s_all = _gdn_chunked(p.reshape(bsz, length, GD_MAIN), gates.reshape(bsz, length, GATE_COLS), gd_s,
                                        j, W['gd_a_log'], W['gd_dt_bias'], W['gd_g_out'], s_all)
            new_conv.append(cv)
            w_out = W['gd_w_out']
        if single:
            a = a.reshape(bsz, D_MODEL)
            x = _proj(a, w_out, j, res=x)
            q = _proj(x, W['xa_w_q'], layer, g=W['g_xattn'], g_index=layer)
            x = _proj(_attn_step(q, mem_k, mem_v, layer), W['xa_w_o'], layer, res=x)
        else:
            x = _xattn_block(x.reshape(bsz, length, D_MODEL), a, mem_k, mem_v, layer, W['g_xattn'], w_out, j,
                             W['xa_w_q'], W['xa_w_o']).reshape(bsz * length, D_MODEL)
        if layer == DEPTH - 1:
            x, y = _ffn(x, W['g_ffn2'], W['ffn2_w_gate'], W['ffn2_w_up'], W['ffn2_w_down'], layer,
                        g_final=W['g_final'])
        else:
            x = _ffn(x, W['g_ffn2'], W['ffn2_w_gate'], W['ffn2_w_up'], W['ffn2_w_down'], layer)
    return (y.reshape(bsz, length, D_MODEL), c_all, jnp.stack(new_n), jnp.stack(new_m), s_all, jnp.stack(new_conv))


def kernel(x_prompt, x_sample, mem_prompt, cache_mem_k, cache_mem_v, state_mlstm_C, state_mlstm_n, state_mlstm_m, state_gdn_S, state_gdn_conv, g_ffn1, ffn1_w_gate, ffn1_w_up, ffn1_w_down, g_mix, ml_w_in, ml_b_i, ml_b_f, ml_g_head, ml_w_out, gd_w_in, gd_conv_w, gd_a_log, gd_dt_bias, gd_g_out, gd_w_out, g_xattn, g_mem, xa_w_q, xa_w_k, xa_w_v, xa_w_o, g_ffn2, ffn2_w_gate, ffn2_w_up, ffn2_w_down, g_final):
    P = dict(g_ffn1=g_ffn1, ffn1_w_gate=ffn1_w_gate, ffn1_w_up=ffn1_w_up, ffn1_w_down=ffn1_w_down, g_mix=g_mix,
             ml_w_in=ml_w_in, ml_b_i=ml_b_i, ml_b_f=ml_b_f, ml_g_head=ml_g_head, ml_w_out=ml_w_out,
             gd_w_in=gd_w_in, gd_conv_w=gd_conv_w, gd_a_log=gd_a_log, gd_dt_bias=gd_dt_bias, gd_g_out=gd_g_out,
             gd_w_out=gd_w_out, g_xattn=g_xattn, g_mem=g_mem, xa_w_q=xa_w_q, xa_w_k=xa_w_k, xa_w_v=xa_w_v,
             xa_w_o=xa_w_o, g_ffn2=g_ffn2, ffn2_w_gate=ffn2_w_gate, ffn2_w_up=ffn2_w_up, ffn2_w_down=ffn2_w_down)
    W = _prep_params(P)
    W['g_final'] = g_final
    batch, n_mem, _ = mem_prompt.shape
    n_ml, n_gd = state_mlstm_C.shape[0], state_gdn_S.shape[0]

    pk, pv, pkb, pvb = _memkv(mem_prompt.reshape(batch * n_mem, D_MODEL), W['g_mem'], W['xa_w_k'], W['xa_w_v'])
    z_c = jnp.zeros((n_ml, batch, ML_HEADS, ML_DQK, ML_DV), F32)
    z_n = jnp.zeros((n_ml, batch, ML_HEADS, ML_DQK), F32)
    z_m = jnp.zeros((n_ml, batch, ML_HEADS), F32)
    z_s = jnp.zeros((n_gd, batch, GD_HEADS, GD_DK, GD_DV), F32)
    z_conv = jnp.zeros((n_gd, batch, GD_CONV - 1, GD_QKV), F32)
    y_p, p_c, p_n, p_m, p_s, p_conv = _trunk(
        x_prompt, pkb.reshape(DEPTH, batch, n_mem, D_MODEL), pvb.reshape(DEPTH, batch, n_mem, D_MODEL),
        z_c, z_n, z_m, z_s, z_conv, W)

    y_s, s_c, s_n, s_m, s_s, s_conv = _trunk(
        x_sample, cache_mem_k, cache_mem_v,
        state_mlstm_C, state_mlstm_n, state_mlstm_m, state_gdn_S, state_gdn_conv, W)

    kv_shape = (DEPTH, batch, n_mem, XA_HEADS, XA_DH)
    return (y_p, y_s, pk.reshape(kv_shape), pv.reshape(kv_shape), p_c, p_n, p_m, p_s, p_conv,
            s_c, s_n, s_m, s_s, s_conv)
```

```python
import functools

import jax
import jax.numpy as jnp
from jax import lax
from jax.experimental import pallas as pl
from jax.experimental.pallas import tpu as pltpu

F32 = jnp.float32
BF16 = jnp.bfloat16

D_MODEL = 1024
DEPTH = 4
N_MEM = 256
D_FF = 2816
FFN_RES = 0.5
EPS = 1e-6
ML_HEADS = 4
ML_DV = 256
ML_DQK = 128
ML_MAIN = 2 * ML_HEADS * ML_DQK + 2 * D_MODEL
GD_HEADS = 8
GD_DK = 128
GD_DV = 128
GD_CONV = 4
GD_QKV = 3072
GD_MAIN = GD_QKV + D_MODEL
XA_HEADS = 4
XA_DH = 256
ML_CHUNK = 256
GD_CHUNK = 128
GD_CHUNK_ROWS = 2
ML_CHUNK_ROWS = 2
LANES = 128
SUBLANES = 8
GATE_COLS = 2 * LANES
VMEM_LIMIT_BYTES = 56 * 1024 * 1024
HI = lax.Precision.HIGHEST


def _params():
    return pltpu.CompilerParams(vmem_limit_bytes=VMEM_LIMIT_BYTES)


def _rms(x, g):
    return x * lax.rsqrt(jnp.mean(x * x, axis=-1, keepdims=True) + EPS) * g


def _silu(x):
    return x * jax.nn.sigmoid(x)


def _softplus(x):
    return jnp.maximum(x, 0.0) + jnp.log1p(jnp.exp(-jnp.abs(x)))


def _log_sigmoid(x):
    return jnp.minimum(x, 0.0) - jnp.log1p(jnp.exp(-jnp.abs(x)))


def _dot(a, b):
    return jnp.dot(a, b, preferred_element_type=F32)


def _dot_nt(a, b):
    return lax.dot_general(a, b, (((1,), (1,)), ((), ())), preferred_element_type=F32)


def _dot_tn(a, b):
    return lax.dot_general(a, b, (((0,), (0,)), ((), ())), preferred_element_type=F32)


def _resident(shape):
    nd = len(shape)
    return pl.BlockSpec(shape, lambda *_: (0,) * nd, pipeline_mode=pl.Buffered(1))


def _layer_block(arr, index):
    nd = arr.ndim - 1
    return pl.BlockSpec((None,) + arr.shape[1:], lambda *_: (index,) + (0,) * nd, pipeline_mode=pl.Buffered(1))


def _ffn_kernel(x_ref, g_ref, wg_ref, wu_ref, wd_ref, *rest, fchunk, final):
    if final:
        gf_ref, o_ref, y_ref, h_ref, a_ref = rest
    else:
        o_ref, h_ref, a_ref = rest
    h_ref[...] = _rms(x_ref[...], g_ref[...]).astype(BF16)
    for j in range(D_FF // fchunk):
        sl = slice(j * fchunk, (j + 1) * fchunk)
        h = h_ref[...]
        gate = _dot(h, wg_ref[:, sl])
        up = _dot(h, wu_ref[:, sl])
        a_ref[:, sl] = (_silu(gate) * up).astype(BF16)
    out = x_ref[...] + FFN_RES * _dot(a_ref[...], wd_ref[...])
    o_ref[...] = out
    if final:
        y_ref[...] = _rms(out, gf_ref[...])


def _ffn(x, g, wg, wu, wd, layer, g_final=None):
    m = x.shape[0]
    tm = min(m, 1024)
    final = g_final is not None
    row = pl.BlockSpec((tm, D_MODEL), lambda i: (i, 0))
    in_specs = [row, _layer_block(g, layer), _layer_block(wg, layer), _layer_block(wu, layer), _layer_block(wd, layer)]
    args = [x, g, wg, wu, wd]
    out_shape = jax.ShapeDtypeStruct((m, D_MODEL), F32)
    out_specs = row
    if final:
        in_specs.append(_resident((1, D_MODEL)))
        args.append(g_final.reshape(1, D_MODEL))
        out_shape = (out_shape, out_shape)
        out_specs = (row, row)
    return pl.pallas_call(
        functools.partial(_ffn_kernel, fchunk=256, final=final),
        out_shape=out_shape, grid=(m // tm,), in_specs=in_specs, out_specs=out_specs,
        scratch_shapes=[pltpu.VMEM((tm, D_MODEL), BF16), pltpu.VMEM((tm, D_FF), BF16)],
        compiler_params=_params(), name="ffn_final" if final else "ffn",
    )(*args)


def _matmul_chunks(h, w_ref, o_ref, nchunk, res_ref=None):
    n = w_ref.shape[1]
    for n0 in range(0, n, nchunk):
        n1 = min(n, n0 + nchunk)
        y = _dot(h, w_ref[:, n0:n1])
        if res_ref is not None:
            y = y + res_ref[:, n0:n1]
        o_ref[:, n0:n1] = y.astype(o_ref.dtype)


def _proj_kernel(*refs, norm, res, nchunk):
    refs = list(refs)
    x_ref = refs.pop(0)
    g_ref = refs.pop(0) if norm else None
    w_ref = refs.pop(0)
    r_ref = refs.pop(0) if res else None
    o_ref = refs.pop(0)
    if norm:
        h = _rms(x_ref[...], g_ref[...]).astype(BF16)
    else:
        h = x_ref[...].astype(BF16)
    _matmul_chunks(h, w_ref, o_ref, nchunk, r_ref)


def _proj(x, w, w_index, g=None, g_index=None, res=None, out_dtype=F32):
    m, k = x.shape
    n = w.shape[-1]
    tm = min(m, 512)
    in_specs = [pl.BlockSpec((tm, k), lambda i: (i, 0))]
    args = [x]
    if g is not None:
        in_specs.append(_layer_block(g, g_index))
        args.append(g)
    in_specs.append(_layer_block(w, w_index))
    args.append(w)
    if res is not None:
        in_specs.append(pl.BlockSpec((tm, n), lambda i: (i, 0)))
        args.append(res)
    return pl.pallas_call(
        functools.partial(_proj_kernel, norm=g is not None, res=res is not None, nchunk=512),
        out_shape=jax.ShapeDtypeStruct((m, n), out_dtype), grid=(m // tm,), in_specs=in_specs,
        out_specs=pl.BlockSpec((tm, n), lambda i: (i, 0)),
        compiler_params=_params(), name="proj",
    )(*args)


def _inproj_kernel(x_ref, g_ref, wm_ref, wg_ref, om_ref, og_ref, *, nchunk):
    h = _rms(x_ref[...], g_ref[...]).astype(BF16)
    _matmul_chunks(h, wm_ref, om_ref, nchunk)
    og_ref[...] = _dot(h, wg_ref[...].astype(BF16))


def _inproj(x, g, g_index, w_main, w_gate, w_index, main_dtype):
    m = x.shape[0]
    n = w_main.shape[-1]
    tm = min(m, 1024)
    rows = lambda width: pl.BlockSpec((tm, width), lambda i: (i, 0))
    return pl.pallas_call(
        functools.partial(_inproj_kernel, nchunk=512),
        out_shape=(jax.ShapeDtypeStruct((m, n), main_dtype), jax.ShapeDtypeStruct((m, GATE_COLS), F32)),
        grid=(m // tm,),
        in_specs=[rows(D_MODEL), _layer_block(g, g_index), _layer_block(w_main, w_index), _layer_block(w_gate, w_index)],
        out_specs=(rows(n), rows(GATE_COLS)),
        compiler_params=_params(), name="inproj",
    )(x, g, w_main, w_gate)


def _memkv_kernel(x_ref, g_ref, wk_ref, wv_ref, k_ref, v_ref, kb_ref, vb_ref):
    h = _rms(x_ref[...], g_ref[...]).astype(BF16)
    k = _dot(h, wk_ref[...])
    v = _dot(h, wv_ref[...])
    kb_ref[0] = k.astype(BF16)
    vb_ref[0] = v.astype(BF16)
    for i in range(XA_HEADS):
        sl = slice(i * XA_DH, (i + 1) * XA_DH)
        k_ref[0, :, i, :] = k[:, sl]
        v_ref[0, :, i, :] = v[:, sl]


def _memkv(mem, g_mem, wk, wv):
    m = mem.shape[0]
    tm = 512
    w_spec = pl.BlockSpec((None, D_MODEL, D_MODEL), lambda l, i: (l, 0, 0))
    o_spec = pl.BlockSpec((1, tm, XA_HEADS, XA_DH), lambda l, i: (l, i, 0, 0))
    b_spec = pl.BlockSpec((1, tm, D_MODEL), lambda l, i: (l, i, 0))
    o_shape = jax.ShapeDtypeStruct((DEPTH, m, XA_HEADS, XA_DH), F32)
    b_shape = jax.ShapeDtypeStruct((DEPTH, m, D_MODEL), BF16)
    return pl.pallas_call(
        _memkv_kernel, out_shape=(o_shape, o_shape, b_shape, b_shape), grid=(DEPTH, m // tm),
        in_specs=[pl.BlockSpec((tm, D_MODEL), lambda l, i: (i, 0)),
                  pl.BlockSpec((None, 1, D_MODEL), lambda l, i: (l, 0, 0)), w_spec, w_spec],
        out_specs=(o_spec, o_spec, b_spec, b_spec), compiler_params=_params(), name="memkv",
    )(mem, g_mem, wk, wv)


def _xattn_kernel(x_ref, a_ref, k_ref, v_ref, g_ref, wout_ref, wq_ref, wo_ref, o_ref, att_ref):
    x2 = x_ref[0] + _dot(a_ref[0], wout_ref[...])
    q = _dot(_rms(x2, g_ref[...]).astype(BF16), wq_ref[...]).astype(BF16)
    for h in range(XA_HEADS):
        sl = slice(h * XA_DH, (h + 1) * XA_DH)
        s = _dot_nt(q[:, sl], k_ref[0, :, sl]) * XA_DH ** -0.5
        e = jnp.exp(s - jnp.max(s, axis=-1, keepdims=True))
        p = e / jnp.sum(e, axis=-1, keepdims=True)
        att_ref[:, sl] = _dot(p.astype(BF16), v_ref[0, :, sl]).astype(BF16)
    o_ref[0] = x2 + _dot(att_ref[...], wo_ref[...])


def _xattn_block(x, a, mem_kb, mem_vb, layer, g, w_out, w_out_index, wq, wo):
    b, l, _ = x.shape
    tq = min(l, 1024)
    row = pl.BlockSpec((1, tq, D_MODEL), lambda i, j: (i, j, 0))
    kv_spec = pl.BlockSpec((None, 1, N_MEM, D_MODEL), lambda i, j: (layer, i, 0, 0))
    return pl.pallas_call(
        _xattn_kernel, out_shape=jax.ShapeDtypeStruct((b, l, D_MODEL), F32), grid=(b, l // tq),
        in_specs=[row, row, kv_spec, kv_spec, _layer_block(g, layer), _layer_block(w_out, w_out_index),
                  _layer_block(wq, layer), _layer_block(wo, layer)],
        out_specs=row, scratch_shapes=[pltpu.VMEM((tq, D_MODEL), BF16)],
        compiler_params=_params(), name="xattn",
    )(x, a, mem_kb, mem_vb, g, w_out, wq, wo)


ATTN_STEP_BATCH = 8


def _attn_step_kernel(q_ref, k_ref, v_ref, o_ref):
    nh = XA_HEADS
    pairs = (N_MEM // 2, 2 * nh, XA_DH)
    for i in range(ATTN_STEP_BATCH):
        q = q_ref[0, i] * XA_DH ** -0.5
        q2 = jnp.concatenate([q, q], axis=0)
        s = jnp.sum(k_ref[0, i].reshape(pairs) * q2[None], axis=-1, keepdims=True)
        mx = jnp.max(s, axis=0, keepdims=True)
        mx = jnp.maximum(mx[:, :nh], mx[:, nh:])
        e = jnp.exp(s - jnp.concatenate([mx, mx], axis=1))
        acc = jnp.sum(e * v_ref[0, i].reshape(pairs), axis=0)
        den = jnp.sum(e, axis=0)
        o_ref[0, i] = (acc[:nh] + acc[nh:]) / (den[:nh] + den[nh:])


def _attn_step(q, mem_k, mem_v, layer):
    b = q.shape[0]
    bb = ATTN_STEP_BATCH
    kv_spec = pl.BlockSpec((1, bb, N_MEM, XA_HEADS, XA_DH), lambda i: (layer, i, 0, 0, 0))
    q_spec = pl.BlockSpec((1, bb, XA_HEADS, XA_DH), lambda i: (i, 0, 0, 0))
    out = pl.pallas_call(
        _attn_step_kernel, out_shape=jax.ShapeDtypeStruct((b // bb, bb, XA_HEADS, XA_DH), F32), grid=(b // bb,),
        in_specs=[q_spec, kv_spec, kv_spec], out_specs=q_spec,
        compiler_params=_params(), name="attn_step",
    )(q.reshape(b // bb, bb, XA_HEADS, XA_DH), mem_k, mem_v)
    return out.reshape(b, D_MODEL)


def _zero_other_layers(ref):
    ref[1:] = jnp.zeros((ref.shape[0] - 1,) + ref.shape[1:], F32)


def _state_out(n_layers, layer_j, bsz, bb, tail, idx, prev):
    shape = jax.ShapeDtypeStruct((n_layers, bsz) + tail, F32)
    zeros = (0,) * len(tail)
    if prev is None:
        spec = pl.BlockSpec((n_layers, bb) + tail, lambda *g: (0, idx(*g)) + zeros)
        return shape, spec, [], [], True
    spec = pl.BlockSpec((1, bb) + tail, lambda *g: (layer_j, idx(*g)) + zeros)
    return shape, spec, [pl.BlockSpec(memory_space=pl.ANY)], [prev], False


def _lane_select(h, value, into):
    lane = lax.broadcasted_iota(jnp.int32, into.shape, 1)
    return jnp.where(lane == h, value, into)


def _mlstm_chunk_kernel(p_ref, gt_ref, c0_ref, n0_ref, m0_ref, bi_ref, bf_ref, gh_ref, *rest, c, wide, aliased):
    if aliased:
        rest = rest[1:]
    o_ref, c_ref, n_ref, m_ref = rest
    rows = range(ML_CHUNK_ROWS)

    @pl.when(pl.program_id(1) == 0)
    def _():
        for r in rows:
            c_ref[0, r] = c0_ref[0, r]
            n_ref[r] = n0_ref[0, r]
            m_ref[r] = m0_ref[0, r]
        if wide:
            _zero_other_layers(c_ref)

    hq = ML_HEADS * ML_DQK
    row = lax.broadcasted_iota(jnp.int32, (c, c), 0)
    col = lax.broadcasted_iota(jnp.int32, (c, c), 1)
    incl = row >= col
    scale = ML_DQK ** -0.5
    i_pres, bs, xts, m_prevs = [], [], [], []
    for r in rows:
        i_pre = gt_ref[r, :, 0:LANES] + bi_ref[...]
        log_f = _log_sigmoid(gt_ref[r, :, LANES:GATE_COLS] + bf_ref[...])
        b = jnp.dot(incl.astype(F32), log_f, precision=HI, preferred_element_type=F32)
        i_pres.append(i_pre)
        bs.append(b)
        xts.append((i_pre - b).T)
        m_prevs.append(m_ref[r])
    probs = [(r, h) for r in rows for h in range(ML_HEADS)]
    n = range(len(probs))
    bcols = [bs[r][:, h:h + 1] for r, h in probs]
    icols = [i_pres[r][:, h:h + 1] for r, h in probs]
    mprev = [m_prevs[r][:, h:h + 1] for r, h in probs]
    qbs = [p_ref[r, :, h * ML_DQK:(h + 1) * ML_DQK] for r, h in probs]
    kbs = [p_ref[r, :, hq + h * ML_DQK:hq + (h + 1) * ML_DQK] for r, h in probs]
    vbs = [p_ref[r, :, 2 * hq + h * ML_DV:2 * hq + (h + 1) * ML_DV] for r, h in probs]
    dms = [jnp.where(incl, bcols[i] + xts[r][h:h + 1, :], -jnp.inf) for i, (r, h) in enumerate(probs)]
    mts = [jnp.maximum(bcols[i] + mprev[i], jnp.max(dms[i], axis=-1, keepdims=True)) for i in n]
    w_inters = [jnp.exp(bcols[i] + mprev[i] - mts[i]) for i in n]
    ss = [_dot_nt(qbs[i], kbs[i]) * (jnp.exp(dms[i] - mts[i]) * scale) for i in n]
    c_old = [c_ref[0, r, h] for r, h in probs]
    n_old = [n_ref[r, h:h + 1, :] for r, h in probs]
    nums = [w_inters[i] * _dot(qbs[i], c_old[i].astype(BF16)) + _dot(ss[i].astype(BF16), vbs[i]) for i in n]
    m_rows = list(m_prevs)
    for i, (r, h) in enumerate(probs):
        m_new = mts[i][c - 1:c, :]
        b_last = bcols[i][c - 1:c, :]
        w_k = jnp.exp(b_last - bcols[i] + icols[i] - m_new) * scale
        decay = jnp.exp(b_last + mprev[i] - m_new)
        kw = kbs[i].astype(F32) * w_k
        c_ref[0, r, h] = decay * c_old[i] + _dot_tn(kw.astype(BF16), vbs[i])
        n_ref[r, h:h + 1, :] = decay * n_old[i] + jnp.sum(kw, axis=0, keepdims=True)
        m_rows[r] = _lane_select(h, m_new, m_rows[r])
    for r in rows:
        m_ref[r] = m_rows[r]
    for i, (r, h) in enumerate(probs):
        den = (w_inters[i] * jnp.sum(qbs[i].astype(F32) * n_old[i], axis=-1, keepdims=True)
               + jnp.sum(ss[i], axis=-1, keepdims=True))
        hh = nums[i] / jnp.maximum(jnp.abs(den), jnp.exp(-mts[i]))
        vs = slice(h * ML_DV, (h + 1) * ML_DV)
        hn = hh * lax.rsqrt(jnp.mean(hh * hh, axis=-1, keepdims=True) + EPS) * gh_ref[:, vs]
        og = p_ref[r, :, 2 * hq + D_MODEL + h * ML_DV:2 * hq + D_MODEL + (h + 1) * ML_DV].astype(F32)
        o_ref[r, :, vs] = (hn * jax.nn.sigmoid(og)).astype(o_ref.dtype)


def _mlstm_small_specs(bsz, layer_j, idx, bb):
    c_in = pl.BlockSpec((1, bb, ML_HEADS, ML_DQK, ML_DV), lambda *g: (layer_j, idx(*g), 0, 0, 0))
    n_in = pl.BlockSpec((1, bb, ML_HEADS, ML_DQK), lambda *g: (layer_j, idx(*g), 0, 0))
    m_in = pl.BlockSpec((1, bb, 1, LANES), lambda *g: (layer_j, idx(*g), 0, 0))
    n_out = pl.BlockSpec((bb, ML_HEADS, ML_DQK), lambda *g: (idx(*g), 0, 0))
    m_out = pl.BlockSpec((bb, 1, LANES), lambda *g: (idx(*g), 0, 0))
    shapes = (jax.ShapeDtypeStruct((bsz, ML_HEADS, ML_DQK), F32), jax.ShapeDtypeStruct((bsz, 1, LANES), F32))
    return (c_in, n_in, m_in), (n_out, m_out), shapes


def _mlstm_call(kernel_fn, name, p, gates, c0, n0, m0, layer_j, b_i, b_f, g_head, c_prev, grid, row_block, idx, bb):
    bsz = p.shape[0]
    n_layers = c0.shape[0]
    s_in, s_out, s_shapes = _mlstm_small_specs(bsz, layer_j, idx, bb)
    c_shape, c_spec, extra_specs, extra_args, wide = _state_out(
        n_layers, layer_j, bsz, bb, (ML_HEADS, ML_DQK, ML_DV), idx, c_prev)
    n_in = 8
    return pl.pallas_call(
        functools.partial(kernel_fn, wide=wide, aliased=not wide),
        out_shape=(jax.ShapeDtypeStruct(p.shape[:2] + (D_MODEL,), BF16), c_shape) + s_shapes,
        grid=grid,
        in_specs=[row_block(ML_MAIN), row_block(GATE_COLS), *s_in, _layer_block(b_i, layer_j),
                  _layer_block(b_f, layer_j), _layer_block(g_head, layer_j), *extra_specs],
        out_specs=(row_block(D_MODEL), c_spec) + s_out,
        input_output_aliases={} if wide else {n_in: 1},
        compiler_params=_params(), name=name,
    )(p, gates, c0, n0, m0, b_i, b_f, g_head, *extra_args)


def _mlstm_chunked(p, gates, c0, n0, m0, layer_j, b_i, b_f, g_head, c_prev):
    bsz, length, _ = p.shape
    bb = ML_CHUNK_ROWS
    row_block = lambda width: pl.BlockSpec((bb, ML_CHUNK, width), lambda i, j: (i, j, 0))
    return _mlstm_call(functools.partial(_mlstm_chunk_kernel, c=ML_CHUNK), "mlstm_chunk", p, gates, c0, n0, m0, layer_j,
                       b_i, b_f, g_head, c_prev, (bsz // bb, length // ML_CHUNK), row_block, lambda i, j: i, bb)


STEP_BATCH = 8


def _row0(x, rows=SUBLANES):
    r = lax.broadcasted_iota(jnp.int32, (rows, x.shape[1]), 0)
    return jnp.where(r == 0, x, 0.0)


def _mlstm_step_kernel(p_ref, gt_ref, c0_ref, n0_ref, m0_ref, bi_ref, bf_ref, gh_ref, *rest, wide, aliased):
    if aliased:
        rest = rest[1:]
    o_ref, c_ref, n_ref, m_ref = rest
    hq = ML_HEADS * ML_DQK
    if wide:
        _zero_other_layers(c_ref)
    w_inters, w_ks, e_invs = [], [], []
    for i in range(STEP_BATCH):
        i_pre = gt_ref[i, :, 0:LANES] + bi_ref[...]
        log_f = _log_sigmoid(gt_ref[i, :, LANES:GATE_COLS] + bf_ref[...])
        m_prev = m0_ref[0, i]
        mt = jnp.maximum(log_f + m_prev, i_pre)
        w_inters.append(jnp.exp(log_f + m_prev - mt))
        w_ks.append(jnp.exp(i_pre - mt))
        e_invs.append(jnp.exp(-mt))
        m_ref[i] = mt
    probs = [(i, h) for i in range(STEP_BATCH) for h in range(ML_HEADS)]
    n = range(len(probs))
    qs = [p_ref[i, :, h * ML_DQK:(h + 1) * ML_DQK].astype(F32) for i, h in probs]
    ks = [p_ref[i, :, hq + h * ML_DQK:hq + (h + 1) * ML_DQK].astype(F32) * ML_DQK ** -0.5 for i, h in probs]
    vs = [p_ref[i, :, 2 * hq + h * ML_DV:2 * hq + (h + 1) * ML_DV].astype(F32) for i, h in probs]
    c_old = [c0_ref[0, i, h] for i, h in probs]
    n_old = [n0_ref[0, i, h:h + 1, :] for i, h in probs]
    wis = [w_inters[i][:, h:h + 1] for i, h in probs]
    wks = [w_ks[i][:, h:h + 1] for i, h in probs]
    q_cs = [_dot(_row0(qs[j]).astype(BF16), c_old[j].astype(BF16))[0:1] for j in n]
    kws = [ks[j] * wks[j] for j in n]
    for j, (i, h) in enumerate(probs):
        v8 = jnp.broadcast_to(vs[j], (SUBLANES, ML_DV)).astype(BF16)
        c_ref[0, i, h] = wis[j] * c_old[j] + _dot_tn(_row0(kws[j]).astype(BF16), v8)
        n_ref[i, h:h + 1, :] = wis[j] * n_old[j] + kws[j]
    stack = lambda xs: jnp.concatenate(xs, axis=0)
    sls = [slice(h * ML_DV, (h + 1) * ML_DV) for _, h in probs]
    q_all, k_all, v_all, wi_all, wk_all = stack(qs), stack(ks), stack(vs), stack(wis), stack(wks)
    s = jnp.sum(q_all * k_all, axis=-1, keepdims=True) * wk_all
    num = wi_all * stack(q_cs) + s * v_all
    den = wi_all * jnp.sum(q_all * stack(n_old), axis=-1, keepdims=True) + s
    hh = num / jnp.maximum(jnp.abs(den), stack([e_invs[i][:, h:h + 1] for i, h in probs]))
    hn = hh * lax.rsqrt(jnp.mean(hh * hh, axis=-1, keepdims=True) + EPS) * stack([gh_ref[:, sl] for sl in sls])
    og = stack([p_ref[i, :, 2 * hq + D_MODEL + h * ML_DV:2 * hq + D_MODEL + (h + 1) * ML_DV] for i, h in probs])
    out = (hn * jax.nn.sigmoid(og.astype(F32))).astype(o_ref.dtype)
    for j, (i, h) in enumerate(probs):
        o_ref[i, :, sls[j]] = out[j:j + 1]


def _mlstm_step(p, gates, c0, n0, m0, layer_j, b_i, b_f, g_head, c_prev):
    bsz = p.shape[0]
    bb = STEP_BATCH
    row_block = lambda width: pl.BlockSpec((bb, 1, width), lambda i: (i, 0, 0))
    return _mlstm_call(_mlstm_step_kernel, "mlstm_step", p, gates, c0, n0, m0, layer_j,
                       b_i, b_f, g_head, c_prev, (bsz // bb,), row_block, lambda i: i, bb)


def _l2n(x):
    return x * lax.rsqrt(jnp.sum(x * x, axis=-1, keepdims=True) + EPS)


INV_BASE = 16


def _unit_lower_inverses(a_list, c):
    row = lax.broadcasted_iota(jnp.int32, (c, c), 0)
    col = lax.broadcasted_iota(jnp.int32, (c, c), 1)
    eye = jnp.where(row == col, 1.0, 0.0)
    same = lambda size: (row >> (size.bit_length() - 1)) == (col >> (size.bit_length() - 1))
    ns = [jnp.where(same(INV_BASE), -a, 0.0) for a in a_list]
    ts = [eye + n for n in ns]
    nbs = [n.astype(BF16) for n in ns]
    power = 2
    while power < INV_BASE:
        ns = [_dot(nb, nb) for nb in nbs]
        nbs = [n.astype(BF16) for n in ns]
        ts = [t + _dot(t.astype(BF16), nb) for t, nb in zip(ts, nbs)]
        power *= 2
    size = INV_BASE
    while size < c:
        off = same(2 * size) & jnp.logical_not(same(size))
        tbs = [t.astype(BF16) for t in ts]
        mids = [_dot(tb, jnp.where(off, a, 0.0).astype(BF16)).astype(BF16) for tb, a in zip(tbs, a_list)]
        ts = [t - _dot(mid, tb) for t, mid, tb in zip(ts, mids, tbs)]
        size *= 2
    return ts


def _gdn_gates(gt_ref, i, alog_ref, dtb_ref):
    beta = jax.nn.sigmoid(gt_ref[i, :, 0:LANES])
    log_g = -jnp.exp(alog_ref[...]) * _softplus(gt_ref[i, :, LANES:GATE_COLS] + dtb_ref[...])
    return beta, log_g


def _gdn_out(o, z, gout):
    return (o * lax.rsqrt(jnp.mean(o * o, axis=-1, keepdims=True) + EPS) * gout * _silu(z))


def _inproj_gdn_kernel(x_ref, g_ref, wm_ref, wg_ref, cp_ref, cw_ref, om_ref, og_ref, nc_ref, e_ref,
                       *, tiles_per_seq, nchunk):
    assert GD_CONV == 4, "the conv below pairs four taps"
    tm = x_ref.shape[0]
    pad = SUBLANES
    keep = GD_CONV - 1
    h = _rms(x_ref[...], g_ref[...]).astype(BF16)
    og_ref[...] = _dot(h, wg_ref[...].astype(BF16))

    @pl.when(pl.program_id(0) % tiles_per_seq == 0)
    def _():
        e_ref[...] = jnp.zeros((pad, GD_QKV), F32)
        e_ref[pl.ds(pad - keep, keep), :] = cp_ref[0, 0]

    hk = GD_HEADS * GD_DK
    starts = list(range(0, GD_QKV, nchunk))
    z_starts = list(range(GD_QKV, GD_MAIN, nchunk))
    every = len(starts) // len(z_starts)
    for idx, n0 in enumerate(starts):
        if idx % every == 0:
            z0 = z_starts[idx // every]
            om_ref[:, z0:z0 + nchunk] = _dot(h, wm_ref[:, z0:z0 + nchunk]).astype(om_ref.dtype)
        tile = _dot(h, wm_ref[:, n0:n0 + nchunk])
        ext = jnp.concatenate([e_ref[:, n0:n0 + nchunk], tile], axis=0)
        nc_ref[0, :, n0:n0 + nchunk] = tile[tm - keep:]
        e_ref[:, n0:n0 + nchunk] = tile[tm - pad:]
        for d0 in range(0, nchunk, GD_DK):
            sl = slice(n0 + d0, n0 + d0 + GD_DK)
            x = ext[:, d0:d0 + GD_DK]
            x1 = pltpu.roll(x, 1, 0)
            near = x * cw_ref[3:4, sl] + x1 * cw_ref[2:3, sl]
            far = x * cw_ref[1:2, sl] + x1 * cw_ref[0:1, sl]
            conv = _silu((near + pltpu.roll(far, 2, 0))[pad:])
            if n0 + d0 < hk:
                conv = _l2n(conv) * GD_DK ** -0.5
            elif n0 + d0 < 2 * hk:
                conv = _l2n(conv)
            om_ref[:, sl] = conv.astype(om_ref.dtype)


def _inproj_gdn(x, g, g_index, w_main, w_gate, w_index, conv_prev, conv_w, bsz):
    m = x.shape[0]
    tm = 1024
    tiles_per_seq = m // bsz // tm
    rows = lambda width: pl.BlockSpec((tm, width), lambda i: (i, 0))
    return pl.pallas_call(
        functools.partial(_inproj_gdn_kernel, tiles_per_seq=tiles_per_seq, nchunk=256),
        out_shape=(jax.ShapeDtypeStruct((m, GD_MAIN), BF16), jax.ShapeDtypeStruct((m, GATE_COLS), F32),
                   jax.ShapeDtypeStruct((bsz, GD_CONV - 1, GD_QKV), F32)),
        grid=(m // tm,),
        in_specs=[rows(D_MODEL), _layer_block(g, g_index), _layer_block(w_main, w_index), _layer_block(w_gate, w_index),
                  pl.BlockSpec((1, 1, GD_CONV - 1, GD_QKV), lambda i: (w_index, i // tiles_per_seq, 0, 0)),
                  _layer_block(conv_w, w_index)],
        out_specs=(rows(GD_MAIN), rows(GATE_COLS),
                   pl.BlockSpec((1, GD_CONV - 1, GD_QKV), lambda i: (i // tiles_per_seq, 0, 0))),
        scratch_shapes=[pltpu.VMEM((SUBLANES, GD_QKV), F32)],
        compiler_params=_params(), name="inproj_gdn",
    )(x, g, w_main, w_gate, conv_prev, conv_w)


def _gdn_chunk_kernel(p_ref, gt_ref, alog_ref, dtb_ref, gout_ref, s0_ref, *rest, c, wide, aliased):
    if aliased:
        rest = rest[1:]
    o_ref, s_ref = rest
    rows = range(GD_CHUNK_ROWS)

    @pl.when(pl.program_id(1) == 0)
    def _():
        for r in rows:
            s_ref[0, r] = s0_ref[0, r]
        if wide:
            _zero_other_layers(s_ref)

    row = lax.broadcasted_iota(jnp.int32, (c, c), 0)
    col = lax.broadcasted_iota(jnp.int32, (c, c), 1)
    incl = row >= col
    strict = row > col
    hk = GD_HEADS * GD_DK
    betas, gams, gam_ts, e_gams = [], [], [], []
    for r in rows:
        beta, log_g = _gdn_gates(gt_ref, r, alog_ref, dtb_ref)
        gam = jnp.dot(incl.astype(F32), log_g, precision=HI, preferred_element_type=F32)
        betas.append(beta)
        gams.append(gam)
        gam_ts.append(gam.T)
        e_gams.append(jnp.exp(gam))
    probs = [(r, h) for r in rows for h in range(GD_HEADS)]
    col_of = lambda xs, r, h: xs[r][:, h:h + 1]
    qbs = [p_ref[r, :, h * GD_DK:(h + 1) * GD_DK] for r, h in probs]
    kbs = [p_ref[r, :, hk + h * GD_DK:hk + (h + 1) * GD_DK] for r, h in probs]
    vs = [p_ref[r, :, 2 * hk + h * GD_DV:2 * hk + (h + 1) * GD_DV].astype(F32) for r, h in probs]
    ks = [kb.astype(F32) for kb in kbs]
    bcols = [col_of(betas, r, h) for r, h in probs]
    gcols = [col_of(gams, r, h) for r, h in probs]
    ecols = [col_of(e_gams, r, h) for r, h in probs]
    n = range(len(probs))
    decs = [jnp.exp(jnp.where(incl, gcols[i] - gam_ts[r][h:h + 1, :], -jnp.inf)) for i, (r, h) in enumerate(probs)]
    kqs = [_dot_nt(jnp.concatenate([kbs[i], qbs[i]], axis=0), kbs[i]) for i in n]
    a_list = [jnp.where(strict, bcols[i] * kqs[i][0:c] * decs[i], 0.0) for i in n]
    ts = _unit_lower_inverses(a_list, c)
    rhs = [jnp.concatenate([bcols[i] * vs[i], (bcols[i] * ecols[i]) * ks[i]], axis=1).astype(BF16) for i in n]
    uws = [_dot(ts[i].astype(BF16), rhs[i]) for i in n]
    s_old = [s_ref[0, r, h] for r, h in probs]
    wq = [jnp.concatenate([uws[i][:, GD_DV:], qbs[i].astype(F32) * ecols[i]], axis=0).astype(BF16) for i in n]
    wqs = [_dot(wq[i], s_old[i].astype(BF16)) for i in n]
    ubs = [(uws[i][:, :GD_DV] - wqs[i][0:c]).astype(BF16) for i in n]
    for i, (r, h) in enumerate(probs):
        g_last = gcols[i][c - 1:c, :]
        k_dec = ks[i] * jnp.exp(g_last - gcols[i])
        s_ref[0, r, h] = jnp.exp(g_last) * s_old[i] + _dot_tn(k_dec.astype(BF16), ubs[i])
    for i, (r, h) in enumerate(probs):
        o = wqs[i][c:2 * c] + _dot((kqs[i][c:2 * c] * decs[i]).astype(BF16), ubs[i])
        z = p_ref[r, :, GD_QKV + h * GD_DV:GD_QKV + (h + 1) * GD_DV].astype(F32)
        o_ref[r, :, h * GD_DV:(h + 1) * GD_DV] = _gdn_out(o, z, gout_ref[...]).astype(o_ref.dtype)


def _gdn_chunked(p, gates, s0, layer_j, a_log, dt_bias, g_out, s_prev):
    bsz, length, _ = p.shape
    n_layers = s0.shape[0]
    bb = GD_CHUNK_ROWS
    idx = lambda i, j: i
    row_block = lambda width: pl.BlockSpec((bb, GD_CHUNK, width), lambda i, j: (i, j, 0))
    s_in = pl.BlockSpec((1, bb, GD_HEADS, GD_DK, GD_DV), lambda i, j: (layer_j, i, 0, 0, 0))
    s_shape, s_spec, extra_specs, extra_args, wide = _state_out(
        n_layers, layer_j, bsz, bb, (GD_HEADS, GD_DK, GD_DV), idx, s_prev)
    n_in = 6
    return pl.pallas_call(
        functools.partial(_gdn_chunk_kernel, c=GD_CHUNK, wide=wide, aliased=not wide),
        out_shape=(jax.ShapeDtypeStruct((bsz, length, D_MODEL), BF16), s_shape),
        grid=(bsz // bb, length // GD_CHUNK),
        in_specs=[row_block(GD_MAIN), row_block(GATE_COLS), _layer_block(a_log, layer_j),
                  _layer_block(dt_bias, layer_j), _layer_block(g_out, layer_j), s_in, *extra_specs],
        out_specs=(row_block(D_MODEL), s_spec),
        input_output_aliases={} if wide else {n_in: 1},
        compiler_params=_params(), name="gdn_chunk",
    )(p, gates, a_log, dt_bias, g_out, s0, *extra_args)


def _gdn_call(kernel_fn, name, p, gates, conv_prev, s0, layer_j, conv_w, a_log, dt_bias, g_out, s_prev,
              grid, row_block, idx, bb, scratch):
    bsz = p.shape[0]
    n_layers = s0.shape[0]
    cp_in = pl.BlockSpec((1, bb, GD_CONV - 1, GD_QKV), lambda *g: (layer_j, idx(*g), 0, 0))
    s_in = pl.BlockSpec((1, bb, GD_HEADS, GD_DK, GD_DV), lambda *g: (layer_j, idx(*g), 0, 0, 0))
    nc_out = pl.BlockSpec((bb, GD_CONV - 1, GD_QKV), lambda *g: (idx(*g), 0, 0))
    s_shape, s_spec, extra_specs, extra_args, wide = _state_out(
        n_layers, layer_j, bsz, bb, (GD_HEADS, GD_DK, GD_DV), idx, s_prev)
    n_in = 8
    return pl.pallas_call(
        functools.partial(kernel_fn, wide=wide, aliased=not wide),
        out_shape=(jax.ShapeDtypeStruct(p.shape[:2] + (D_MODEL,), BF16), s_shape,
                   jax.ShapeDtypeStruct((bsz, GD_CONV - 1, GD_QKV), F32)),
        grid=grid,
        in_specs=[row_block(GD_MAIN), row_block(GATE_COLS), cp_in, _layer_block(conv_w, layer_j),
                  _layer_block(a_log, layer_j), _layer_block(dt_bias, layer_j), _layer_block(g_out, layer_j), s_in,
                  *extra_specs],
        out_specs=(row_block(D_MODEL), s_spec, nc_out),
        input_output_aliases={} if wide else {n_in: 1},
        scratch_shapes=scratch, compiler_params=_params(), name=name,
    )(p, gates, conv_prev, conv_w, a_log, dt_bias, g_out, s0, *extra_args)


def _gdn_step_kernel(p_ref, gt_ref, cp_ref, cw_ref, alog_ref, dtb_ref, gout_ref, s0_ref, *rest, wide, aliased):
    if aliased:
        rest = rest[1:]
    o_ref, s_ref, nc_ref = rest
    hk = GD_HEADS * GD_DK
    rows = lax.broadcasted_iota(jnp.int32, (SUBLANES, GD_DK), 0)
    if wide:
        _zero_other_layers(s_ref)
    convs, betas, gs = [], [], []
    for i in range(STEP_BATCH):
        qkv = p_ref[i, :, 0:GD_QKV]
        prev = cp_ref[0, i]
        conv = qkv * cw_ref[GD_CONV - 1:GD_CONV, :]
        for j in range(GD_CONV - 1):
            conv = conv + prev[j:j + 1, :] * cw_ref[j:j + 1, :]
        convs.append(_silu(conv))
        nc_ref[i] = jnp.concatenate([prev[1:GD_CONV - 1], qkv], axis=0)
        beta, log_g = _gdn_gates(gt_ref, i, alog_ref, dtb_ref)
        betas.append(beta)
        gs.append(jnp.exp(log_g))
    probs = [(i, h) for i in range(STEP_BATCH) for h in range(GD_HEADS)]
    n = range(len(probs))
    stack = lambda xs: jnp.concatenate(xs, axis=0)
    q_all = _l2n(stack([convs[i][:, h * GD_DK:(h + 1) * GD_DK] for i, h in probs])) * GD_DK ** -0.5
    k_all = _l2n(stack([convs[i][:, hk + h * GD_DK:hk + (h + 1) * GD_DK] for i, h in probs]))
    v_all = stack([convs[i][:, 2 * hk + h * GD_DV:2 * hk + (h + 1) * GD_DV] for i, h in probs])
    g_all = stack([gs[i][:, h:h + 1] for i, h in probs])
    b_all = stack([betas[i][:, h:h + 1] for i, h in probs])
    s_old = [s0_ref[0, i, h] for i, h in probs]
    kq_ss = [_dot(jnp.where(rows == 0, k_all[j:j + 1], jnp.where(rows == 1, q_all[j:j + 1], 0.0)).astype(BF16),
                  s_old[j].astype(BF16)) for j in n]
    u_all = b_all * (v_all - g_all * stack([kq_s[0:1] for kq_s in kq_ss]))
    for j, (i, h) in enumerate(probs):
        u8 = jnp.broadcast_to(u_all[j:j + 1], (SUBLANES, GD_DV)).astype(BF16)
        s_ref[0, i, h] = g_all[j:j + 1] * s_old[j] + _dot_tn(_row0(k_all[j:j + 1]).astype(BF16), u8)
    o_all = g_all * stack([kq_s[1:2] for kq_s in kq_ss]) + jnp.sum(q_all * k_all, axis=-1, keepdims=True) * u_all
    z_all = stack([p_ref[i, :, GD_QKV + h * GD_DV:GD_QKV + (h + 1) * GD_DV] for i, h in probs])
    out = _gdn_out(o_all, z_all, gout_ref[...]).astype(o_ref.dtype)
    for j, (i, h) in enumerate(probs):
        o_ref[i, :, h * GD_DV:(h + 1) * GD_DV] = out[j:j + 1]


def _gdn_step(p, gates, conv_prev, s0, layer_j, conv_w, a_log, dt_bias, g_out, s_prev):
    bsz = p.shape[0]
    bb = STEP_BATCH
    row_block = lambda width: pl.BlockSpec((bb, 1, width), lambda i: (i, 0, 0))
    return _gdn_call(_gdn_step_kernel, "gdn_step", p, gates, conv_prev, s0, layer_j,
                     conv_w, a_log, dt_bias, g_out, s_prev, (bsz // bb,), row_block, lambda i: i, bb, [])


def _lane_pad(x):
    return jnp.pad(x, [(0, 0)] * (x.ndim - 1) + [(0, LANES - x.shape[-1])])


def _prep_params(P):
    W = {}
    for name in ('ffn1_w_gate', 'ffn1_w_up', 'ffn1_w_down', 'ffn2_w_gate', 'ffn2_w_up', 'ffn2_w_down',
                 'ml_w_out', 'gd_w_out', 'xa_w_q', 'xa_w_k', 'xa_w_v', 'xa_w_o'):
        W[name] = P[name].astype(BF16)
    for name in ('g_ffn1', 'g_mix', 'g_xattn', 'g_mem', 'g_ffn2', 'ml_g_head', 'gd_g_out'):
        W[name] = P[name][:, None, :]
    for name in ('ml_b_i', 'ml_b_f', 'gd_a_log', 'gd_dt_bias'):
        W[name] = _lane_pad(P[name])[:, None, :]
    W['gd_conv_w'] = P['gd_conv_w']
    ml, gd = P['ml_w_in'], P['gd_w_in']
    W['ml_w_main'] = ml[..., :ML_MAIN].astype(BF16)
    W['ml_w_gate'] = jnp.concatenate(
        [_lane_pad(ml[..., ML_MAIN:ML_MAIN + ML_HEADS]), _lane_pad(ml[..., ML_MAIN + ML_HEADS:])], axis=-1)
    W['gd_w_main'] = gd[..., :GD_MAIN].astype(BF16)
    W['gd_w_gate'] = jnp.concatenate(
        [_lane_pad(gd[..., GD_MAIN:GD_MAIN + GD_HEADS]), _lane_pad(gd[..., GD_MAIN + GD_HEADS:])], axis=-1)
    return W


def _trunk(x, mem_k, mem_v, ml_c, ml_n, ml_m, gd_s, gd_conv, W):
    bsz, length, _ = x.shape
    single = length == 1
    x = x.reshape(bsz * length, D_MODEL)
    ml_m = _lane_pad(ml_m)[:, :, None, :]
    c_all = s_all = None
    new_n, new_m, new_conv = [], [], []
    y = None
    for layer in range(DEPTH):
        j = layer // 2
        x = _ffn(x, W['g_ffn1'], W['ffn1_w_gate'], W['ffn1_w_up'], W['ffn1_w_down'], layer)
        if layer % 2 == 0:
            p, gates = _inproj(x, W['g_mix'], layer, W['ml_w_main'], W['ml_w_gate'], j, BF16)
            fn = _mlstm_step if single else _mlstm_chunked
            a, c_all, n, m = fn(p.reshape(bsz, length, ML_MAIN), gates.reshape(bsz, length, GATE_COLS), ml_c, ml_n, ml_m,
                                j, W['ml_b_i'], W['ml_b_f'], W['ml_g_head'], c_all)
            new_n.append(n)
            new_m.append(m[:, 0, :ML_HEADS])
            w_out = W['ml_w_out']
        else:
            if single:
                p, gates = _inproj(x, W['g_mix'], layer, W['gd_w_main'], W['gd_w_gate'], j, F32)
                a, s_all, cv = _gdn_step(p.reshape(bsz, 1, GD_MAIN), gates.reshape(bsz, 1, GATE_COLS), gd_conv, gd_s,
                                         j, W['gd_conv_w'], W['gd_a_log'], W['gd_dt_bias'], W['gd_g_out'], s_all)
            else:
                p, gates, cv = _inproj_gdn(x, W['g_mix'], layer, W['gd_w_main'], W['gd_w_gate'], j, gd_conv,
                                           W['gd_conv_w'], bsz)
                a, s_all = _gdn_chunked(p.reshape(bsz, length, GD_MAIN), gates.reshape(bsz, length, GATE_COLS), gd_s,
                                        j, W['gd_a_log'], W['gd_dt_bias'], W['gd_g_out'], s_all)
            new_conv.append(cv)
            w_out = W['gd_w_out']
        if single:
            a = a.reshape(bsz, D_MODEL)
            x = _proj(a, w_out, j, res=x)
            q = _proj(x, W['xa_w_q'], layer, g=W['g_xattn'], g_index=layer)
            x = _proj(_attn_step(q, mem_k, mem_v, layer), W['xa_w_o'], layer, res=x)
        else:
            x = _xattn_block(x.reshape(bsz, length, D_MODEL), a, mem_k, mem_v, layer, W['g_xattn'], w_out, j,
                             W['xa_w_q'], W['xa_w_o']).reshape(bsz * length, D_MODEL)
        if layer == DEPTH - 1:
            x, y = _ffn(x, W['g_ffn2'], W['ffn2_w_gate'], W['ffn2_w_up'], W['ffn2_w_down'], layer,
                        g_final=W['g_final'])
        else:
            x = _ffn(x, W['g_ffn2'], W['ffn2_w_gate'], W['ffn2_w_up'], W['ffn2_w_down'], layer)
    return (y.reshape(bsz, length, D_MODEL), c_all, jnp.stack(new_n), jnp.stack(new_m), s_all, jnp.stack(new_conv))


def kernel(x_prompt, x_sample, mem_prompt, cache_mem_k, cache_mem_v, state_mlstm_C, state_mlstm_n, state_mlstm_m, state_gdn_S, state_gdn_conv, g_ffn1, ffn1_w_gate, ffn1_w_up, ffn1_w_down, g_mix, ml_w_in, ml_b_i, ml_b_f, ml_g_head, ml_w_out, gd_w_in, gd_conv_w, gd_a_log, gd_dt_bias, gd_g_out, gd_w_out, g_xattn, g_mem, xa_w_q, xa_w_k, xa_w_v, xa_w_o, g_ffn2, ffn2_w_gate, ffn2_w_up, ffn2_w_down, g_final):
    P = dict(g_ffn1=g_ffn1, ffn1_w_gate=ffn1_w_gate, ffn1_w_up=ffn1_w_up, ffn1_w_down=ffn1_w_down, g_mix=g_mix,
             ml_w_in=ml_w_in, ml_b_i=ml_b_i, ml_b_f=ml_b_f, ml_g_head=ml_g_head, ml_w_out=ml_w_out,
             gd_w_in=gd_w_in, gd_conv_w=gd_conv_w, gd_a_log=gd_a_log, gd_dt_bias=gd_dt_bias, gd_g_out=gd_g_out,
             gd_w_out=gd_w_out, g_xattn=g_xattn, g_mem=g_mem, xa_w_q=xa_w_q, xa_w_k=xa_w_k, xa_w_v=xa_w_v,
             xa_w_o=xa_w_o, g_ffn2=g_ffn2, ffn2_w_gate=ffn2_w_gate, ffn2_w_up=ffn2_w_up, ffn2_w_down=ffn2_w_down)
    W = _prep_params(P)
    W['g_final'] = g_final
    batch, n_mem, _ = mem_prompt.shape
    n_ml, n_gd = state_mlstm_C.shape[0], state_gdn_S.shape[0]

    pk, pv, pkb, pvb = _memkv(mem_prompt.reshape(batch * n_mem, D_MODEL), W['g_mem'], W['xa_w_k'], W['xa_w_v'])
    z_c = jnp.zeros((n_ml, batch, ML_HEADS, ML_DQK, ML_DV), F32)
    z_n = jnp.zeros((n_ml, batch, ML_HEADS, ML_DQK), F32)
    z_m = jnp.zeros((n_ml, batch, ML_HEADS), F32)
    z_s = jnp.zeros((n_gd, batch, GD_HEADS, GD_DK, GD_DV), F32)
    z_conv = jnp.zeros((n_gd, batch, GD_CONV - 1, GD_QKV), F32)
    y_p, p_c, p_n, p_m, p_s, p_conv = _trunk(
        x_prompt, pkb.reshape(DEPTH, batch, n_mem, D_MODEL), pvb.reshape(DEPTH, batch, n_mem, D_MODEL),
        z_c, z_n, z_m, z_s, z_conv, W)

    y_s, s_c, s_n, s_m, s_s, s_conv = _trunk(
        x_sample, cache_mem_k, cache_mem_v,
        state_mlstm_C, state_mlstm_n, state_mlstm_m, state_gdn_S, state_gdn_conv, W)

    kv_shape = (DEPTH, batch, n_mem, XA_HEADS, XA_DH)
    return (y_p, y_s, pk.reshape(kv_shape), pv.reshape(kv_shape), p_c, p_n, p_m, p_s, p_conv,
            s_c, s_n, s_m, s_s, s_conv)
```

```python
import functools

import jax
import jax.numpy as jnp
from jax import lax
from jax.experimental import pallas as pl
from jax.experimental.pallas import tpu as pltpu

F32 = jnp.float32
BF16 = jnp.bfloat16

D_MODEL = 1024
DEPTH = 4
N_MEM = 256
D_FF = 2816
FFN_RES = 0.5
EPS = 1e-6
ML_HEADS = 4
ML_DV = 256
ML_DQK = 128
ML_MAIN = 2 * ML_HEADS * ML_DQK + 2 * D_MODEL
GD_HEADS = 8
GD_DK = 128
GD_DV = 128
GD_CONV = 4
GD_QKV = 3072
GD_MAIN = GD_QKV + D_MODEL
XA_HEADS = 4
XA_DH = 256
ML_CHUNK = 256
GD_CHUNK = 128
GD_CHUNK_ROWS = 2
ML_CHUNK_ROWS = 4
LANES = 128
SUBLANES = 8
GATE_COLS = 2 * LANES
VMEM_LIMIT_BYTES = 56 * 1024 * 1024
HI = lax.Precision.HIGHEST


def _params():
    return pltpu.CompilerParams(vmem_limit_bytes=VMEM_LIMIT_BYTES)


def _rms(x, g):
    return x * lax.rsqrt(jnp.mean(x * x, axis=-1, keepdims=True) + EPS) * g


def _silu(x):
    return x * jax.nn.sigmoid(x)


def _softplus(x):
    return jnp.maximum(x, 0.0) + jnp.log1p(jnp.exp(-jnp.abs(x)))


def _log_sigmoid(x):
    return jnp.minimum(x, 0.0) - jnp.log1p(jnp.exp(-jnp.abs(x)))


def _dot(a, b):
    return jnp.dot(a, b, preferred_element_type=F32)


def _dot_nt(a, b):
    return lax.dot_general(a, b, (((1,), (1,)), ((), ())), preferred_element_type=F32)


def _dot_tn(a, b):
    return lax.dot_general(a, b, (((0,), (0,)), ((), ())), preferred_element_type=F32)


def _resident(shape):
    nd = len(shape)
    return pl.BlockSpec(shape, lambda *_: (0,) * nd, pipeline_mode=pl.Buffered(1))


def _layer_block(arr, index):
    nd = arr.ndim - 1
    return pl.BlockSpec((None,) + arr.shape[1:], lambda *_: (index,) + (0,) * nd, pipeline_mode=pl.Buffered(1))


def _ffn_kernel(x_ref, g_ref, wg_ref, wu_ref, wd_ref, *rest, fchunk, final):
    if final:
        gf_ref, o_ref, y_ref, h_ref, a_ref = rest
    else:
        o_ref, h_ref, a_ref = rest
    h_ref[...] = _rms(x_ref[...], g_ref[...]).astype(BF16)
    for j in range(D_FF // fchunk):
        sl = slice(j * fchunk, (j + 1) * fchunk)
        h = h_ref[...]
        gate = _dot(h, wg_ref[:, sl])
        up = _dot(h, wu_ref[:, sl])
        a_ref[:, sl] = (_silu(gate) * up).astype(BF16)
    out = x_ref[...] + FFN_RES * _dot(a_ref[...], wd_ref[...])
    o_ref[...] = out
    if final:
        y_ref[...] = _rms(out, gf_ref[...])


def _ffn(x, g, wg, wu, wd, layer, g_final=None):
    m = x.shape[0]
    tm = min(m, 1024)
    final = g_final is not None
    row = pl.BlockSpec((tm, D_MODEL), lambda i: (i, 0))
    in_specs = [row, _layer_block(g, layer), _layer_block(wg, layer), _layer_block(wu, layer), _layer_block(wd, layer)]
    args = [x, g, wg, wu, wd]
    out_shape = jax.ShapeDtypeStruct((m, D_MODEL), F32)
    out_specs = row
    if final:
        in_specs.append(_resident((1, D_MODEL)))
        args.append(g_final.reshape(1, D_MODEL))
        out_shape = (out_shape, out_shape)
        out_specs = (row, row)
    return pl.pallas_call(
        functools.partial(_ffn_kernel, fchunk=256, final=final),
        out_shape=out_shape, grid=(m // tm,), in_specs=in_specs, out_specs=out_specs,
        scratch_shapes=[pltpu.VMEM((tm, D_MODEL), BF16), pltpu.VMEM((tm, D_FF), BF16)],
        compiler_params=_params(), name="ffn_final" if final else "ffn",
    )(*args)


def _matmul_chunks(h, w_ref, o_ref, nchunk, res_ref=None):
    n = w_ref.shape[1]
    for n0 in range(0, n, nchunk):
        n1 = min(n, n0 + nchunk)
        y = _dot(h, w_ref[:, n0:n1])
        if res_ref is not None:
            y = y + res_ref[:, n0:n1]
        o_ref[:, n0:n1] = y.astype(o_ref.dtype)


def _proj_kernel(*refs, norm, res, nchunk):
    refs = list(refs)
    x_ref = refs.pop(0)
    g_ref = refs.pop(0) if norm else None
    w_ref = refs.pop(0)
    r_ref = refs.pop(0) if res else None
    o_ref = refs.pop(0)
    if norm:
        h = _rms(x_ref[...], g_ref[...]).astype(BF16)
    else:
        h = x_ref[...].astype(BF16)
    _matmul_chunks(h, w_ref, o_ref, nchunk, r_ref)


def _proj(x, w, w_index, g=None, g_index=None, res=None, out_dtype=F32):
    m, k = x.shape
    n = w.shape[-1]
    tm = min(m, 512)
    in_specs = [pl.BlockSpec((tm, k), lambda i: (i, 0))]
    args = [x]
    if g is not None:
        in_specs.append(_layer_block(g, g_index))
        args.append(g)
    in_specs.append(_layer_block(w, w_index))
    args.append(w)
    if res is not None:
        in_specs.append(pl.BlockSpec((tm, n), lambda i: (i, 0)))
        args.append(res)
    return pl.pallas_call(
        functools.partial(_proj_kernel, norm=g is not None, res=res is not None, nchunk=512),
        out_shape=jax.ShapeDtypeStruct((m, n), out_dtype), grid=(m // tm,), in_specs=in_specs,
        out_specs=pl.BlockSpec((tm, n), lambda i: (i, 0)),
        compiler_params=_params(), name="proj",
    )(*args)


def _inproj_kernel(x_ref, g_ref, wm_ref, wg_ref, om_ref, og_ref, *, nchunk):
    h = _rms(x_ref[...], g_ref[...]).astype(BF16)
    _matmul_chunks(h, wm_ref, om_ref, nchunk)
    og_ref[...] = _dot(h, wg_ref[...].astype(BF16))


def _inproj(x, g, g_index, w_main, w_gate, w_index, main_dtype):
    m = x.shape[0]
    n = w_main.shape[-1]
    tm = min(m, 1024)
    rows = lambda width: pl.BlockSpec((tm, width), lambda i: (i, 0))
    return pl.pallas_call(
        functools.partial(_inproj_kernel, nchunk=512),
        out_shape=(jax.ShapeDtypeStruct((m, n), main_dtype), jax.ShapeDtypeStruct((m, GATE_COLS), F32)),
        grid=(m // tm,),
        in_specs=[rows(D_MODEL), _layer_block(g, g_index), _layer_block(w_main, w_index), _layer_block(w_gate, w_index)],
        out_specs=(rows(n), rows(GATE_COLS)),
        compiler_params=_params(), name="inproj",
    )(x, g, w_main, w_gate)


def _memkv_kernel(x_ref, g_ref, wk_ref, wv_ref, k_ref, v_ref, kb_ref, vb_ref):
    h = _rms(x_ref[...], g_ref[...]).astype(BF16)
    k = _dot(h, wk_ref[...])
    v = _dot(h, wv_ref[...])
    kb_ref[0] = k.astype(BF16)
    vb_ref[0] = v.astype(BF16)
    for i in range(XA_HEADS):
        sl = slice(i * XA_DH, (i + 1) * XA_DH)
        k_ref[0, :, i, :] = k[:, sl]
        v_ref[0, :, i, :] = v[:, sl]


def _memkv(mem, g_mem, wk, wv):
    m = mem.shape[0]
    tm = 512
    w_spec = pl.BlockSpec((None, D_MODEL, D_MODEL), lambda l, i: (l, 0, 0))
    o_spec = pl.BlockSpec((1, tm, XA_HEADS, XA_DH), lambda l, i: (l, i, 0, 0))
    b_spec = pl.BlockSpec((1, tm, D_MODEL), lambda l, i: (l, i, 0))
    o_shape = jax.ShapeDtypeStruct((DEPTH, m, XA_HEADS, XA_DH), F32)
    b_shape = jax.ShapeDtypeStruct((DEPTH, m, D_MODEL), BF16)
    return pl.pallas_call(
        _memkv_kernel, out_shape=(o_shape, o_shape, b_shape, b_shape), grid=(DEPTH, m // tm),
        in_specs=[pl.BlockSpec((tm, D_MODEL), lambda l, i: (i, 0)),
                  pl.BlockSpec((None, 1, D_MODEL), lambda l, i: (l, 0, 0)), w_spec, w_spec],
        out_specs=(o_spec, o_spec, b_spec, b_spec), compiler_params=_params(), name="memkv",
    )(mem, g_mem, wk, wv)


def _xattn_kernel(x_ref, a_ref, k_ref, v_ref, g_ref, wout_ref, wq_ref, wo_ref, o_ref, att_ref):
    x2 = x_ref[0] + _dot(a_ref[0], wout_ref[...])
    q = _dot(_rms(x2, g_ref[...]).astype(BF16), wq_ref[...]).astype(BF16)
    for h in range(XA_HEADS):
        sl = slice(h * XA_DH, (h + 1) * XA_DH)
        s = _dot_nt(q[:, sl], k_ref[0, :, sl]) * XA_DH ** -0.5
        e = jnp.exp(s - jnp.max(s, axis=-1, keepdims=True))
        p = e / jnp.sum(e, axis=-1, keepdims=True)
        att_ref[:, sl] = _dot(p.astype(BF16), v_ref[0, :, sl]).astype(BF16)
    o_ref[0] = x2 + _dot(att_ref[...], wo_ref[...])


def _xattn_block(x, a, mem_kb, mem_vb, layer, g, w_out, w_out_index, wq, wo):
    b, l, _ = x.shape
    tq = min(l, 1024)
    row = pl.BlockSpec((1, tq, D_MODEL), lambda i, j: (i, j, 0))
    kv_spec = pl.BlockSpec((None, 1, N_MEM, D_MODEL), lambda i, j: (layer, i, 0, 0))
    return pl.pallas_call(
        _xattn_kernel, out_shape=jax.ShapeDtypeStruct((b, l, D_MODEL), F32), grid=(b, l // tq),
        in_specs=[row, row, kv_spec, kv_spec, _layer_block(g, layer), _layer_block(w_out, w_out_index),
                  _layer_block(wq, layer), _layer_block(wo, layer)],
        out_specs=row, scratch_shapes=[pltpu.VMEM((tq, D_MODEL), BF16)],
        compiler_params=_params(), name="xattn",
    )(x, a, mem_kb, mem_vb, g, w_out, wq, wo)


ATTN_STEP_BATCH = 8


def _attn_step_kernel(q_ref, k_ref, v_ref, o_ref):
    nh = XA_HEADS
    pairs = (N_MEM // 2, 2 * nh, XA_DH)
    for i in range(ATTN_STEP_BATCH):
        q = q_ref[0, i] * XA_DH ** -0.5
        q2 = jnp.concatenate([q, q], axis=0)
        s = jnp.sum(k_ref[0, i].reshape(pairs) * q2[None], axis=-1, keepdims=True)
        mx = jnp.max(s, axis=0, keepdims=True)
        mx = jnp.maximum(mx[:, :nh], mx[:, nh:])
        e = jnp.exp(s - jnp.concatenate([mx, mx], axis=1))
        acc = jnp.sum(e * v_ref[0, i].reshape(pairs), axis=0)
        den = jnp.sum(e, axis=0)
        o_ref[0, i] = (acc[:nh] + acc[nh:]) / (den[:nh] + den[nh:])


def _attn_step(q, mem_k, mem_v, layer):
    b = q.shape[0]
    bb = ATTN_STEP_BATCH
    kv_spec = pl.BlockSpec((1, bb, N_MEM, XA_HEADS, XA_DH), lambda i: (layer, i, 0, 0, 0))
    q_spec = pl.BlockSpec((1, bb, XA_HEADS, XA_DH), lambda i: (i, 0, 0, 0))
    out = pl.pallas_call(
        _attn_step_kernel, out_shape=jax.ShapeDtypeStruct((b // bb, bb, XA_HEADS, XA_DH), F32), grid=(b // bb,),
        in_specs=[q_spec, kv_spec, kv_spec], out_specs=q_spec,
        compiler_params=_params(), name="attn_step",
    )(q.reshape(b // bb, bb, XA_HEADS, XA_DH), mem_k, mem_v)
    return out.reshape(b, D_MODEL)


def _zero_other_layers(ref):
    ref[1:] = jnp.zeros((ref.shape[0] - 1,) + ref.shape[1:], F32)


def _state_out(n_layers, layer_j, bsz, bb, tail, idx, prev):
    shape = jax.ShapeDtypeStruct((n_layers, bsz) + tail, F32)
    zeros = (0,) * len(tail)
    if prev is None:
        spec = pl.BlockSpec((n_layers, bb) + tail, lambda *g: (0, idx(*g)) + zeros)
        return shape, spec, [], [], True
    spec = pl.BlockSpec((1, bb) + tail, lambda *g: (layer_j, idx(*g)) + zeros)
    return shape, spec, [pl.BlockSpec(memory_space=pl.ANY)], [prev], False


def _lane_select(h, value, into):
    lane = lax.broadcasted_iota(jnp.int32, into.shape, 1)
    return jnp.where(lane == h, value, into)


def _mlstm_chunk_kernel(p_ref, gt_ref, c0_ref, n0_ref, m0_ref, bi_ref, bf_ref, gh_ref, *rest, c, wide, aliased):
    if aliased:
        rest = rest[1:]
    o_ref, c_ref, n_ref, m_ref = rest
    rows = range(ML_CHUNK_ROWS)

    @pl.when(pl.program_id(1) == 0)
    def _():
        for r in rows:
            c_ref[0, r] = c0_ref[0, r]
            n_ref[r] = n0_ref[0, r]
            m_ref[r] = m0_ref[0, r]
        if wide:
            _zero_other_layers(c_ref)

    hq = ML_HEADS * ML_DQK
    row = lax.broadcasted_iota(jnp.int32, (c, c), 0)
    col = lax.broadcasted_iota(jnp.int32, (c, c), 1)
    incl = row >= col
    scale = ML_DQK ** -0.5
    i_pres, bs, xts, m_prevs = [], [], [], []
    for r in rows:
        i_pre = gt_ref[r, :, 0:LANES] + bi_ref[...]
        log_f = _log_sigmoid(gt_ref[r, :, LANES:GATE_COLS] + bf_ref[...])
        b = jnp.dot(incl.astype(F32), log_f, precision=HI, preferred_element_type=F32)
        i_pres.append(i_pre)
        bs.append(b)
        xts.append((i_pre - b).T)
        m_prevs.append(m_ref[r])
    probs = [(r, h) for r in rows for h in range(ML_HEADS)]
    n = range(len(probs))
    bcols = [bs[r][:, h:h + 1] for r, h in probs]
    icols = [i_pres[r][:, h:h + 1] for r, h in probs]
    mprev = [m_prevs[r][:, h:h + 1] for r, h in probs]
    qbs = [p_ref[r, :, h * ML_DQK:(h + 1) * ML_DQK] for r, h in probs]
    kbs = [p_ref[r, :, hq + h * ML_DQK:hq + (h + 1) * ML_DQK] for r, h in probs]
    vbs = [p_ref[r, :, 2 * hq + h * ML_DV:2 * hq + (h + 1) * ML_DV] for r, h in probs]
    dms = [jnp.where(incl, bcols[i] + xts[r][h:h + 1, :], -jnp.inf) for i, (r, h) in enumerate(probs)]
    mts = [jnp.maximum(bcols[i] + mprev[i], jnp.max(dms[i], axis=-1, keepdims=True)) for i in n]
    w_inters = [jnp.exp(bcols[i] + mprev[i] - mts[i]) for i in n]
    ss = [_dot_nt(qbs[i], kbs[i]) * (jnp.exp(dms[i] - mts[i]) * scale) for i in n]
    c_old = [c_ref[0, r, h] for r, h in probs]
    n_old = [n_ref[r, h:h + 1, :] for r, h in probs]
    nums = [w_inters[i] * _dot(qbs[i], c_old[i].astype(BF16)) + _dot(ss[i].astype(BF16), vbs[i]) for i in n]
    m_rows = list(m_prevs)
    for i, (r, h) in enumerate(probs):
        m_new = mts[i][c - 1:c, :]
        b_last = bcols[i][c - 1:c, :]
        w_k = jnp.exp(b_last - bcols[i] + icols[i] - m_new) * scale
        decay = jnp.exp(b_last + mprev[i] - m_new)
        kw = kbs[i].astype(F32) * w_k
        c_ref[0, r, h] = decay * c_old[i] + _dot_tn(kw.astype(BF16), vbs[i])
        n_ref[r, h:h + 1, :] = decay * n_old[i] + jnp.sum(kw, axis=0, keepdims=True)
        m_rows[r] = _lane_select(h, m_new, m_rows[r])
    for r in rows:
        m_ref[r] = m_rows[r]
    for i, (r, h) in enumerate(probs):
        den = (w_inters[i] * jnp.sum(qbs[i].astype(F32) * n_old[i], axis=-1, keepdims=True)
               + jnp.sum(ss[i], axis=-1, keepdims=True))
        hh = nums[i] / jnp.maximum(jnp.abs(den), jnp.exp(-mts[i]))
        vs = slice(h * ML_DV, (h + 1) * ML_DV)
        hn = hh * lax.rsqrt(jnp.mean(hh * hh, axis=-1, keepdims=True) + EPS) * gh_ref[:, vs]
        og = p_ref[r, :, 2 * hq + D_MODEL + h * ML_DV:2 * hq + D_MODEL + (h + 1) * ML_DV].astype(F32)
        o_ref[r, :, vs] = (hn * jax.nn.sigmoid(og)).astype(o_ref.dtype)


def _mlstm_small_specs(bsz, layer_j, idx, bb):
    c_in = pl.BlockSpec((1, bb, ML_HEADS, ML_DQK, ML_DV), lambda *g: (layer_j, idx(*g), 0, 0, 0))
    n_in = pl.BlockSpec((1, bb, ML_HEADS, ML_DQK), lambda *g: (layer_j, idx(*g), 0, 0))
    m_in = pl.BlockSpec((1, bb, 1, LANES), lambda *g: (layer_j, idx(*g), 0, 0))
    n_out = pl.BlockSpec((bb, ML_HEADS, ML_DQK), lambda *g: (idx(*g), 0, 0))
    m_out = pl.BlockSpec((bb, 1, LANES), lambda *g: (idx(*g), 0, 0))
    shapes = (jax.ShapeDtypeStruct((bsz, ML_HEADS, ML_DQK), F32), jax.ShapeDtypeStruct((bsz, 1, LANES), F32))
    return (c_in, n_in, m_in), (n_out, m_out), shapes


def _mlstm_call(kernel_fn, name, p, gates, c0, n0, m0, layer_j, b_i, b_f, g_head, c_prev, grid, row_block, idx, bb):
    bsz = p.shape[0]
    n_layers = c0.shape[0]
    s_in, s_out, s_shapes = _mlstm_small_specs(bsz, layer_j, idx, bb)
    c_shape, c_spec, extra_specs, extra_args, wide = _state_out(
        n_layers, layer_j, bsz, bb, (ML_HEADS, ML_DQK, ML_DV), idx, c_prev)
    n_in = 8
    return pl.pallas_call(
        functools.partial(kernel_fn, wide=wide, aliased=not wide),
        out_shape=(jax.ShapeDtypeStruct(p.shape[:2] + (D_MODEL,), BF16), c_shape) + s_shapes,
        grid=grid,
        in_specs=[row_block(ML_MAIN), row_block(GATE_COLS), *s_in, _layer_block(b_i, layer_j),
                  _layer_block(b_f, layer_j), _layer_block(g_head, layer_j), *extra_specs],
        out_specs=(row_block(D_MODEL), c_spec) + s_out,
        input_output_aliases={} if wide else {n_in: 1},
        compiler_params=_params(), name=name,
    )(p, gates, c0, n0, m0, b_i, b_f, g_head, *extra_args)


def _mlstm_chunked(p, gates, c0, n0, m0, layer_j, b_i, b_f, g_head, c_prev):
    bsz, length, _ = p.shape
    bb = ML_CHUNK_ROWS
    row_block = lambda width: pl.BlockSpec((bb, ML_CHUNK, width), lambda i, j: (i, j, 0))
    return _mlstm_call(functools.partial(_mlstm_chunk_kernel, c=ML_CHUNK), "mlstm_chunk", p, gates, c0, n0, m0, layer_j,
                       b_i, b_f, g_head, c_prev, (bsz // bb, length // ML_CHUNK), row_block, lambda i, j: i, bb)


STEP_BATCH = 8


def _row0(x, rows=SUBLANES):
    r = lax.broadcasted_iota(jnp.int32, (rows, x.shape[1]), 0)
    return jnp.where(r == 0, x, 0.0)


def _mlstm_step_kernel(p_ref, gt_ref, c0_ref, n0_ref, m0_ref, bi_ref, bf_ref, gh_ref, *rest, wide, aliased):
    if aliased:
        rest = rest[1:]
    o_ref, c_ref, n_ref, m_ref = rest
    hq = ML_HEADS * ML_DQK
    if wide:
        _zero_other_layers(c_ref)
    w_inters, w_ks, e_invs = [], [], []
    for i in range(STEP_BATCH):
        i_pre = gt_ref[i, :, 0:LANES] + bi_ref[...]
        log_f = _log_sigmoid(gt_ref[i, :, LANES:GATE_COLS] + bf_ref[...])
        m_prev = m0_ref[0, i]
        mt = jnp.maximum(log_f + m_prev, i_pre)
        w_inters.append(jnp.exp(log_f + m_prev - mt))
        w_ks.append(jnp.exp(i_pre - mt))
        e_invs.append(jnp.exp(-mt))
        m_ref[i] = mt
    probs = [(i, h) for i in range(STEP_BATCH) for h in range(ML_HEADS)]
    n = range(len(probs))
    qs = [p_ref[i, :, h * ML_DQK:(h + 1) * ML_DQK].astype(F32) for i, h in probs]
    ks = [p_ref[i, :, hq + h * ML_DQK:hq + (h + 1) * ML_DQK].astype(F32) * ML_DQK ** -0.5 for i, h in probs]
    vs = [p_ref[i, :, 2 * hq + h * ML_DV:2 * hq + (h + 1) * ML_DV].astype(F32) for i, h in probs]
    c_old = [c0_ref[0, i, h] for i, h in probs]
    n_old = [n0_ref[0, i, h:h + 1, :] for i, h in probs]
    wis = [w_inters[i][:, h:h + 1] for i, h in probs]
    wks = [w_ks[i][:, h:h + 1] for i, h in probs]
    q_cs = [_dot(_row0(qs[j]).astype(BF16), c_old[j].astype(BF16))[0:1] for j in n]
    kws = [ks[j] * wks[j] for j in n]
    for j, (i, h) in enumerate(probs):
        v8 = jnp.broadcast_to(vs[j], (SUBLANES, ML_DV)).astype(BF16)
        c_ref[0, i, h] = wis[j] * c_old[j] + _dot_tn(_row0(kws[j]).astype(BF16), v8)
        n_ref[i, h:h + 1, :] = wis[j] * n_old[j] + kws[j]
    stack = lambda xs: jnp.concatenate(xs, axis=0)
    sls = [slice(h * ML_DV, (h + 1) * ML_DV) for _, h in probs]
    q_all, k_all, v_all, wi_all, wk_all = stack(qs), stack(ks), stack(vs), stack(wis), stack(wks)
    s = jnp.sum(q_all * k_all, axis=-1, keepdims=True) * wk_all
    num = wi_all * stack(q_cs) + s * v_all
    den = wi_all * jnp.sum(q_all * stack(n_old), axis=-1, keepdims=True) + s
    hh = num / jnp.maximum(jnp.abs(den), stack([e_invs[i][:, h:h + 1] for i, h in probs]))
    hn = hh * lax.rsqrt(jnp.mean(hh * hh, axis=-1, keepdims=True) + EPS) * stack([gh_ref[:, sl] for sl in sls])
    og = stack([p_ref[i, :, 2 * hq + D_MODEL + h * ML_DV:2 * hq + D_MODEL + (h + 1) * ML_DV] for i, h in probs])
    out = (hn * jax.nn.sigmoid(og.astype(F32))).astype(o_ref.dtype)
    for j, (i, h) in enumerate(probs):
        o_ref[i, :, sls[j]] = out[j:j + 1]


def _mlstm_step(p, gates, c0, n0, m0, layer_j, b_i, b_f, g_head, c_prev):
    bsz = p.shape[0]
    bb = STEP_BATCH
    row_block = lambda width: pl.BlockSpec((bb, 1, width), lambda i: (i, 0, 0))
    return _mlstm_call(_mlstm_step_kernel, "mlstm_step", p, gates, c0, n0, m0, layer_j,
                       b_i, b_f, g_head, c_prev, (bsz // bb,), row_block, lambda i: i, bb)


def _l2n(x):
    return x * lax.rsqrt(jnp.sum(x * x, axis=-1, keepdims=True) + EPS)


INV_BASE = 16


def _unit_lower_inverses(a_list, c):
    row = lax.broadcasted_iota(jnp.int32, (c, c), 0)
    col = lax.broadcasted_iota(jnp.int32, (c, c), 1)
    eye = jnp.where(row == col, 1.0, 0.0)
    same = lambda size: (row >> (size.bit_length() - 1)) == (col >> (size.bit_length() - 1))
    ns = [jnp.where(same(INV_BASE), -a, 0.0) for a in a_list]
    ts = [eye + n for n in ns]
    nbs = [n.astype(BF16) for n in ns]
    power = 2
    while power < INV_BASE:
        ns = [_dot(nb, nb) for nb in nbs]
        nbs = [n.astype(BF16) for n in ns]
        ts = [t + _dot(t.astype(BF16), nb) for t, nb in zip(ts, nbs)]
        power *= 2
    size = INV_BASE
    while size < c:
        off = same(2 * size) & jnp.logical_not(same(size))
        tbs = [t.astype(BF16) for t in ts]
        mids = [_dot(tb, jnp.where(off, a, 0.0).astype(BF16)).astype(BF16) for tb, a in zip(tbs, a_list)]
        ts = [t - _dot(mid, tb) for t, mid, tb in zip(ts, mids, tbs)]
        size *= 2
    return ts


def _gdn_gates(gt_ref, i, alog_ref, dtb_ref):
    beta = jax.nn.sigmoid(gt_ref[i, :, 0:LANES])
    log_g = -jnp.exp(alog_ref[...]) * _softplus(gt_ref[i, :, LANES:GATE_COLS] + dtb_ref[...])
    return beta, log_g


def _gdn_out(o, z, gout):
    return (o * lax.rsqrt(jnp.mean(o * o, axis=-1, keepdims=True) + EPS) * gout * _silu(z))


def _inproj_gdn_kernel(x_ref, g_ref, wm_ref, wg_ref, cp_ref, cw_ref, om_ref, og_ref, nc_ref, e_ref,
                       *, tiles_per_seq, nchunk):
    assert GD_CONV == 4, "the conv below pairs four taps"
    tm = x_ref.shape[0]
    pad = SUBLANES
    keep = GD_CONV - 1
    h = _rms(x_ref[...], g_ref[...]).astype(BF16)
    og_ref[...] = _dot(h, wg_ref[...].astype(BF16))

    @pl.when(pl.program_id(0) % tiles_per_seq == 0)
    def _():
        e_ref[...] = jnp.zeros((pad, GD_QKV), F32)
        e_ref[pl.ds(pad - keep, keep), :] = cp_ref[0, 0]

    hk = GD_HEADS * GD_DK
    starts = list(range(0, GD_QKV, nchunk))
    z_starts = list(range(GD_QKV, GD_MAIN, nchunk))
    every = len(starts) // len(z_starts)
    for idx, n0 in enumerate(starts):
        if idx % every == 0:
            z0 = z_starts[idx // every]
            om_ref[:, z0:z0 + nchunk] = _dot(h, wm_ref[:, z0:z0 + nchunk]).astype(om_ref.dtype)
        tile = _dot(h, wm_ref[:, n0:n0 + nchunk])
        ext = jnp.concatenate([e_ref[:, n0:n0 + nchunk], tile], axis=0)
        nc_ref[0, :, n0:n0 + nchunk] = tile[tm - keep:]
        e_ref[:, n0:n0 + nchunk] = tile[tm - pad:]
        for d0 in range(0, nchunk, GD_DK):
            sl = slice(n0 + d0, n0 + d0 + GD_DK)
            x = ext[:, d0:d0 + GD_DK]
            x1 = pltpu.roll(x, 1, 0)
            near = x * cw_ref[3:4, sl] + x1 * cw_ref[2:3, sl]
            far = x * cw_ref[1:2, sl] + x1 * cw_ref[0:1, sl]
            conv = _silu((near + pltpu.roll(far, 2, 0))[pad:])
            if n0 + d0 < hk:
                conv = _l2n(conv) * GD_DK ** -0.5
            elif n0 + d0 < 2 * hk:
                conv = _l2n(conv)
            om_ref[:, sl] = conv.astype(om_ref.dtype)


def _inproj_gdn(x, g, g_index, w_main, w_gate, w_index, conv_prev, conv_w, bsz):
    m = x.shape[0]
    tm = 1024
    tiles_per_seq = m // bsz // tm
    rows = lambda width: pl.BlockSpec((tm, width), lambda i: (i, 0))
    return pl.pallas_call(
        functools.partial(_inproj_gdn_kernel, tiles_per_seq=tiles_per_seq, nchunk=256),
        out_shape=(jax.ShapeDtypeStruct((m, GD_MAIN), BF16), jax.ShapeDtypeStruct((m, GATE_COLS), F32),
                   jax.ShapeDtypeStruct((bsz, GD_CONV - 1, GD_QKV), F32)),
        grid=(m // tm,),
        in_specs=[rows(D_MODEL), _layer_block(g, g_index), _layer_block(w_main, w_index), _layer_block(w_gate, w_index),
                  pl.BlockSpec((1, 1, GD_CONV - 1, GD_QKV), lambda i: (w_index, i // tiles_per_seq, 0, 0)),
                  _layer_block(conv_w, w_index)],
        out_specs=(rows(GD_MAIN), rows(GATE_COLS),
                   pl.BlockSpec((1, GD_CONV - 1, GD_QKV), lambda i: (i // tiles_per_seq, 0, 0))),
        scratch_shapes=[pltpu.VMEM((SUBLANES, GD_QKV), F32)],
        compiler_params=_params(), name="inproj_gdn",
    )(x, g, w_main, w_gate, conv_prev, conv_w)


def _gdn_chunk_kernel(p_ref, gt_ref, alog_ref, dtb_ref, gout_ref, s0_ref, *rest, c, wide, aliased):
    if aliased:
        rest = rest[1:]
    o_ref, s_ref = rest
    rows = range(GD_CHUNK_ROWS)

    @pl.when(pl.program_id(1) == 0)
    def _():
        for r in rows:
            s_ref[0, r] = s0_ref[0, r]
        if wide:
            _zero_other_layers(s_ref)

    row = lax.broadcasted_iota(jnp.int32, (c, c), 0)
    col = lax.broadcasted_iota(jnp.int32, (c, c), 1)
    incl = row >= col
    strict = row > col
    hk = GD_HEADS * GD_DK
    betas, gams, gam_ts, e_gams = [], [], [], []
    for r in rows:
        beta, log_g = _gdn_gates(gt_ref, r, alog_ref, dtb_ref)
        gam = jnp.dot(incl.astype(F32), log_g, precision=HI, preferred_element_type=F32)
        betas.append(beta)
        gams.append(gam)
        gam_ts.append(gam.T)
        e_gams.append(jnp.exp(gam))
    probs = [(r, h) for r in rows for h in range(GD_HEADS)]
    col_of = lambda xs, r, h: xs[r][:, h:h + 1]
    qbs = [p_ref[r, :, h * GD_DK:(h + 1) * GD_DK] for r, h in probs]
    kbs = [p_ref[r, :, hk + h * GD_DK:hk + (h + 1) * GD_DK] for r, h in probs]
    vs = [p_ref[r, :, 2 * hk + h * GD_DV:2 * hk + (h + 1) * GD_DV].astype(F32) for r, h in probs]
    ks = [kb.astype(F32) for kb in kbs]
    bcols = [col_of(betas, r, h) for r, h in probs]
    gcols = [col_of(gams, r, h) for r, h in probs]
    ecols = [col_of(e_gams, r, h) for r, h in probs]
    n = range(len(probs))
    decs = [jnp.exp(jnp.where(incl, gcols[i] - gam_ts[r][h:h + 1, :], -jnp.inf)) for i, (r, h) in enumerate(probs)]
    kqs = [_dot_nt(jnp.concatenate([kbs[i], qbs[i]], axis=0), kbs[i]) for i in n]
    a_list = [jnp.where(strict, bcols[i] * kqs[i][0:c] * decs[i], 0.0) for i in n]
    ts = _unit_lower_inverses(a_list, c)
    rhs = [jnp.concatenate([bcols[i] * vs[i], (bcols[i] * ecols[i]) * ks[i]], axis=1).astype(BF16) for i in n]
    uws = [_dot(ts[i].astype(BF16), rhs[i]) for i in n]
    s_old = [s_ref[0, r, h] for r, h in probs]
    wq = [jnp.concatenate([uws[i][:, GD_DV:], qbs[i].astype(F32) * ecols[i]], axis=0).astype(BF16) for i in n]
    wqs = [_dot(wq[i], s_old[i].astype(BF16)) for i in n]
    ubs = [(uws[i][:, :GD_DV] - wqs[i][0:c]).astype(BF16) for i in n]
    for i, (r, h) in enumerate(probs):
        g_last = gcols[i][c - 1:c, :]
        k_dec = ks[i] * jnp.exp(g_last - gcols[i])
        s_ref[0, r, h] = jnp.exp(g_last) * s_old[i] + _dot_tn(k_dec.astype(BF16), ubs[i])
    for i, (r, h) in enumerate(probs):
        o = wqs[i][c:2 * c] + _dot((kqs[i][c:2 * c] * decs[i]).astype(BF16), ubs[i])
        z = p_ref[r, :, GD_QKV + h * GD_DV:GD_QKV + (h + 1) * GD_DV].astype(F32)
        o_ref[r, :, h * GD_DV:(h + 1) * GD_DV] = _gdn_out(o, z, gout_ref[...]).astype(o_ref.dtype)


def _gdn_chunked(p, gates, s0, layer_j, a_log, dt_bias, g_out, s_prev):
    bsz, length, _ = p.shape
    n_layers = s0.shape[0]
    bb = GD_CHUNK_ROWS
    idx = lambda i, j: i
    row_block = lambda width: pl.BlockSpec((bb, GD_CHUNK, width), lambda i, j: (i, j, 0))
    s_in = pl.BlockSpec((1, bb, GD_HEADS, GD_DK, GD_DV), lambda i, j: (layer_j, i, 0, 0, 0))
    s_shape, s_spec, extra_specs, extra_args, wide = _state_out(
        n_layers, layer_j, bsz, bb, (GD_HEADS, GD_DK, GD_DV), idx, s_prev)
    n_in = 6
    return pl.pallas_call(
        functools.partial(_gdn_chunk_kernel, c=GD_CHUNK, wide=wide, aliased=not wide),
        out_shape=(jax.ShapeDtypeStruct((bsz, length, D_MODEL), BF16), s_shape),
        grid=(bsz // bb, length // GD_CHUNK),
        in_specs=[row_block(GD_MAIN), row_block(GATE_COLS), _layer_block(a_log, layer_j),
                  _layer_block(dt_bias, layer_j), _layer_block(g_out, layer_j), s_in, *extra_specs],
        out_specs=(row_block(D_MODEL), s_spec),
        input_output_aliases={} if wide else {n_in: 1},
        compiler_params=_params(), name="gdn_chunk",
    )(p, gates, a_log, dt_bias, g_out, s0, *extra_args)


def _gdn_call(kernel_fn, name, p, gates, conv_prev, s0, layer_j, conv_w, a_log, dt_bias, g_out, s_prev,
              grid, row_block, idx, bb, scratch):
    bsz = p.shape[0]
    n_layers = s0.shape[0]
    cp_in = pl.BlockSpec((1, bb, GD_CONV - 1, GD_QKV), lambda *g: (layer_j, idx(*g), 0, 0))
    s_in = pl.BlockSpec((1, bb, GD_HEADS, GD_DK, GD_DV), lambda *g: (layer_j, idx(*g), 0, 0, 0))
    nc_out = pl.BlockSpec((bb, GD_CONV - 1, GD_QKV), lambda *g: (idx(*g), 0, 0))
    s_shape, s_spec, extra_specs, extra_args, wide = _state_out(
        n_layers, layer_j, bsz, bb, (GD_HEADS, GD_DK, GD_DV), idx, s_prev)
    n_in = 8
    return pl.pallas_call(
        functools.partial(kernel_fn, wide=wide, aliased=not wide),
        out_shape=(jax.ShapeDtypeStruct(p.shape[:2] + (D_MODEL,), BF16), s_shape,
                   jax.ShapeDtypeStruct((bsz, GD_CONV - 1, GD_QKV), F32)),
        grid=grid,
        in_specs=[row_block(GD_MAIN), row_block(GATE_COLS), cp_in, _layer_block(conv_w, layer_j),
                  _layer_block(a_log, layer_j), _layer_block(dt_bias, layer_j), _layer_block(g_out, layer_j), s_in,
                  *extra_specs],
        out_specs=(row_block(D_MODEL), s_spec, nc_out),
        input_output_aliases={} if wide else {n_in: 1},
        scratch_shapes=scratch, compiler_params=_params(), name=name,
    )(p, gates, conv_prev, conv_w, a_log, dt_bias, g_out, s0, *extra_args)


def _gdn_step_kernel(p_ref, gt_ref, cp_ref, cw_ref, alog_ref, dtb_ref, gout_ref, s0_ref, *rest, wide, aliased):
    if aliased:
        rest = rest[1:]
    o_ref, s_ref, nc_ref = rest
    hk = GD_HEADS * GD_DK
    rows = lax.broadcasted_iota(jnp.int32, (SUBLANES, GD_DK), 0)
    if wide:
        _zero_other_layers(s_ref)
    convs, betas, gs = [], [], []
    for i in range(STEP_BATCH):
        qkv = p_ref[i, :, 0:GD_QKV]
        prev = cp_ref[0, i]
        conv = qkv * cw_ref[GD_CONV - 1:GD_CONV, :]
        for j in range(GD_CONV - 1):
            conv = conv + prev[j:j + 1, :] * cw_ref[j:j + 1, :]
        convs.append(_silu(conv))
        nc_ref[i] = jnp.concatenate([prev[1:GD_CONV - 1], qkv], axis=0)
        beta, log_g = _gdn_gates(gt_ref, i, alog_ref, dtb_ref)
        betas.append(beta)
        gs.append(jnp.exp(log_g))
    probs = [(i, h) for i in range(STEP_BATCH) for h in range(GD_HEADS)]
    n = range(len(probs))
    stack = lambda xs: jnp.concatenate(xs, axis=0)
    q_all = _l2n(stack([convs[i][:, h * GD_DK:(h + 1) * GD_DK] for i, h in probs])) * GD_DK ** -0.5
    k_all = _l2n(stack([convs[i][:, hk + h * GD_DK:hk + (h + 1) * GD_DK] for i, h in probs]))
    v_all = stack([convs[i][:, 2 * hk + h * GD_DV:2 * hk + (h + 1) * GD_DV] for i, h in probs])
    g_all = stack([gs[i][:, h:h + 1] for i, h in probs])
    b_all = stack([betas[i][:, h:h + 1] for i, h in probs])
    s_old = [s0_ref[0, i, h] for i, h in probs]
    kq_ss = [_dot(jnp.where(rows == 0, k_all[j:j + 1], jnp.where(rows == 1, q_all[j:j + 1], 0.0)).astype(BF16),
                  s_old[j].astype(BF16)) for j in n]
    u_all = b_all * (v_all - g_all * stack([kq_s[0:1] for kq_s in kq_ss]))
    for j, (i, h) in enumerate(probs):
        u8 = jnp.broadcast_to(u_all[j:j + 1], (SUBLANES, GD_DV)).astype(BF16)
        s_ref[0, i, h] = g_all[j:j + 1] * s_old[j] + _dot_tn(_row0(k_all[j:j + 1]).astype(BF16), u8)
    o_all = g_all * stack([kq_s[1:2] for kq_s in kq_ss]) + jnp.sum(q_all * k_all, axis=-1, keepdims=True) * u_all
    z_all = stack([p_ref[i, :, GD_QKV + h * GD_DV:GD_QKV + (h + 1) * GD_DV] for i, h in probs])
    out = _gdn_out(o_all, z_all, gout_ref[...]).astype(o_ref.dtype)
    for j, (i, h) in enumerate(probs):
        o_ref[i, :, h * GD_DV:(h + 1) * GD_DV] = out[j:j + 1]


def _gdn_step(p, gates, conv_prev, s0, layer_j, conv_w, a_log, dt_bias, g_out, s_prev):
    bsz = p.shape[0]
    bb = STEP_BATCH
    row_block = lambda width: pl.BlockSpec((bb, 1, width), lambda i: (i, 0, 0))
    return _gdn_call(_gdn_step_kernel, "gdn_step", p, gates, conv_prev, s0, layer_j,
                     conv_w, a_log, dt_bias, g_out, s_prev, (bsz // bb,), row_block, lambda i: i, bb, [])


def _lane_pad(x):
    return jnp.pad(x, [(0, 0)] * (x.ndim - 1) + [(0, LANES - x.shape[-1])])


def _prep_params(P):
    W = {}
    for name in ('ffn1_w_gate', 'ffn1_w_up', 'ffn1_w_down', 'ffn2_w_gate', 'ffn2_w_up', 'ffn2_w_down',
                 'ml_w_out', 'gd_w_out', 'xa_w_q', 'xa_w_k', 'xa_w_v', 'xa_w_o'):
        W[name] = P[name].astype(BF16)
    for name in ('g_ffn1', 'g_mix', 'g_xattn', 'g_mem', 'g_ffn2', 'ml_g_head', 'gd_g_out'):
        W[name] = P[name][:, None, :]
    for name in ('ml_b_i', 'ml_b_f', 'gd_a_log', 'gd_dt_bias'):
        W[name] = _lane_pad(P[name])[:, None, :]
    W['gd_conv_w'] = P['gd_conv_w']
    ml, gd = P['ml_w_in'], P['gd_w_in']
    W['ml_w_main'] = ml[..., :ML_MAIN].astype(BF16)
    W['ml_w_gate'] = jnp.concatenate(
        [_lane_pad(ml[..., ML_MAIN:ML_MAIN + ML_HEADS]), _lane_pad(ml[..., ML_MAIN + ML_HEADS:])], axis=-1)
    W['gd_w_main'] = gd[..., :GD_MAIN].astype(BF16)
    W['gd_w_gate'] = jnp.concatenate(
        [_lane_pad(gd[..., GD_MAIN:GD_MAIN + GD_HEADS]), _lane_pad(gd[..., GD_MAIN + GD_HEADS:])], axis=-1)
    return W


def _trunk(x, mem_k, mem_v, ml_c, ml_n, ml_m, gd_s, gd_conv, W):
    bsz, length, _ = x.shape
    single = length == 1
    x = x.reshape(bsz * length, D_MODEL)
    ml_m = _lane_pad(ml_m)[:, :, None, :]
    c_all = s_all = None
    new_n, new_m, new_conv = [], [], []
    y = None
    for layer in range(DEPTH):
        j = layer // 2
        x = _ffn(x, W['g_ffn1'], W['ffn1_w_gate'], W['ffn1_w_up'], W['ffn1_w_down'], layer)
        if layer % 2 == 0:
            p, gates = _inproj(x, W['g_mix'], layer, W['ml_w_main'], W['ml_w_gate'], j, BF16)
            fn = _mlstm_step if single else _mlstm_chunked
            a, c_all, n, m = fn(p.reshape(bsz, length, ML_MAIN), gates.reshape(bsz, length, GATE_COLS), ml_c, ml_n, ml_m,
                                j, W['ml_b_i'], W['ml_b_f'], W['ml_g_head'], c_all)
            new_n.append(n)
            new_m.append(m[:, 0, :ML_HEADS])
            w_out = W['ml_w_out']
        else:
            if single:
                p, gates = _inproj(x, W['g_mix'], layer, W['gd_w_main'], W['gd_w_gate'], j, F32)
                a, s_all, cv = _gdn_step(p.reshape(bsz, 1, GD_MAIN), gates.reshape(bsz, 1, GATE_COLS), gd_conv, gd_s,
                                         j, W['gd_conv_w'], W['gd_a_log'], W['gd_dt_bias'], W['gd_g_out'], s_all)
            else:
                p, gates, cv = _inproj_gdn(x, W['g_mix'], layer, W['gd_w_main'], W['gd_w_gate'], j, gd_conv,
                                           W['gd_conv_w'], bsz)
                a, s_all = _gdn_chunked(p.reshape(bsz, length, GD_MAIN), gates.reshape(bsz, length, GATE_COLS), gd_s,
                                        j, W['gd_a_log'], W['gd_dt_bias'], W['gd_g_out'], s_all)
            new_conv.append(cv)
            w_out = W['gd_w_out']
        if single:
            a = a.reshape(bsz, D_MODEL)
            x = _proj(a, w_out, j, res=x)
            q = _proj(x, W['xa_w_q'], layer, g=W['g_xattn'], g_index=layer)
            x = _proj(_attn_step(q, mem_k, mem_v, layer), W['xa_w_o'], layer, res=x)
        else:
            x = _xattn_block(x.reshape(bsz, length, D_MODEL), a, mem_k, mem_v, layer, W['g_xattn'], w_out, j,
                             W['xa_w_q'], W['xa_w_o']).reshape(bsz * length, D_MODEL)
        if layer == DEPTH - 1:
            x, y = _ffn(x, W['g_ffn2'], W['ffn2_w_gate'], W['ffn2_w_up'], W['ffn2_w_down'], layer,
                        g_final=W['g_final'])
        else:
            x = _ffn(x, W['g_ffn2'], W['ffn2_w_gate'], W['ffn2_w_up'], W['ffn2_w_down'], layer)
    return (y.reshape(bsz, length, D_MODEL), c_all, jnp.stack(new_n), jnp.stack(new_m), s_all, jnp.stack(new_conv))


def kernel(x_prompt, x_sample, mem_prompt, cache_mem_k, cache_mem_v, state_mlstm_C, state_mlstm_n, state_mlstm_m, state_gdn_S, state_gdn_conv, g_ffn1, ffn1_w_gate, ffn1_w_up, ffn1_w_down, g_mix, ml_w_in, ml_b_i, ml_b_f, ml_g_head, ml_w_out, gd_w_in, gd_conv_w, gd_a_log, gd_dt_bias, gd_g_out, gd_w_out, g_xattn, g_mem, xa_w_q, xa_w_k, xa_w_v, xa_w_o, g_ffn2, ffn2_w_gate, ffn2_w_up, ffn2_w_down, g_final):
    P = dict(g_ffn1=g_ffn1, ffn1_w_gate=ffn1_w_gate, ffn1_w_up=ffn1_w_up, ffn1_w_down=ffn1_w_down, g_mix=g_mix,
             ml_w_in=ml_w_in, ml_b_i=ml_b_i, ml_b_f=ml_b_f, ml_g_head=ml_g_head, ml_w_out=ml_w_out,
             gd_w_in=gd_w_in, gd_conv_w=gd_conv_w, gd_a_log=gd_a_log, gd_dt_bias=gd_dt_bias, gd_g_out=gd_g_out,
             gd_w_out=gd_w_out, g_xattn=g_xattn, g_mem=g_mem, xa_w_q=xa_w_q, xa_w_k=xa_w_k, xa_w_v=xa_w_v,
             xa_w_o=xa_w_o, g_ffn2=g_ffn2, ffn2_w_gate=ffn2_w_gate, ffn2_w_up=ffn2_w_up, ffn2_w_down=ffn2_w_down)
    W = _prep_params(P)
    W['g_final'] = g_final
    batch, n_mem, _ = mem_prompt.shape
    n_ml, n_gd = state_mlstm_C.shape[0], state_gdn_S.shape[0]

    pk, pv, pkb, pvb = _memkv(mem_prompt.reshape(batch * n_mem, D_MODEL), W['g_mem'], W['xa_w_k'], W['xa_w_v'])
    z_c = jnp.zeros((n_ml, batch, ML_HEADS, ML_DQK, ML_DV), F32)
    z_n = jnp.zeros((n_ml, batch, ML_HEADS, ML_DQK), F32)
    z_m = jnp.zeros((n_ml, batch, ML_HEADS), F32)
    z_s = jnp.zeros((n_gd, batch, GD_HEADS, GD_DK, GD_DV), F32)
    z_conv = jnp.zeros((n_gd, batch, GD_CONV - 1, GD_QKV), F32)
    y_p, p_c, p_n, p_m, p_s, p_conv = _trunk(
        x_prompt, pkb.reshape(DEPTH, batch, n_mem, D_MODEL), pvb.reshape(DEPTH, batch, n_mem, D_MODEL),
        z_c, z_n, z_m, z_s, z_conv, W)

    y_s, s_c, s_n, s_m, s_s, s_conv = _trunk(
        x_sample, cache_mem_k, cache_mem_v,
        state_mlstm_C, state_mlstm_n, state_mlstm_m, state_gdn_S, state_gdn_conv, W)

    kv_shape = (DEPTH, batch, n_mem, XA_HEADS, XA_DH)
    return (y_p, y_s, pk.reshape(kv_shape), pv.reshape(kv_shape), p_c, p_n, p_m, p_s, p_conv,
            s_c, s_n, s_m, s_s, s_conv)
```

```python
import functools

import jax
import jax.numpy as jnp
from jax import lax
from jax.experimental import pallas as pl
from jax.experimental.pallas import tpu as pltpu

F32 = jnp.float32
BF16 = jnp.bfloat16

D_MODEL = 1024
DEPTH = 4
N_MEM = 256
D_FF = 2816
FFN_RES = 0.5
EPS = 1e-6
ML_HEADS = 4
ML_DV = 256
ML_DQK = 128
ML_MAIN = 2 * ML_HEADS * ML_DQK + 2 * D_MODEL
GD_HEADS = 8
GD_DK = 128
GD_DV = 128
GD_CONV = 4
GD_QKV = 3072
GD_MAIN = GD_QKV + D_MODEL
XA_HEADS = 4
XA_DH = 256
ML_CHUNK = 256
GD_CHUNK = 128
GD_CHUNK_ROWS = 2
ML_CHUNK_ROWS = 4
LANES = 128
SUBLANES = 8
GATE_COLS = 2 * LANES
VMEM_LIMIT_BYTES = 56 * 1024 * 1024
HI = lax.Precision.HIGHEST


def _params():
    return pltpu.CompilerParams(vmem_limit_bytes=VMEM_LIMIT_BYTES)


def _rms(x, g):
    return x * lax.rsqrt(jnp.mean(x * x, axis=-1, keepdims=True) + EPS) * g


def _silu(x):
    return x * jax.nn.sigmoid(x)


def _softplus(x):
    return jnp.maximum(x, 0.0) + jnp.log1p(jnp.exp(-jnp.abs(x)))


def _log_sigmoid(x):
    return jnp.minimum(x, 0.0) - jnp.log1p(jnp.exp(-jnp.abs(x)))


def _dot(a, b):
    return jnp.dot(a, b, preferred_element_type=F32)


def _dot_nt(a, b):
    return lax.dot_general(a, b, (((1,), (1,)), ((), ())), preferred_element_type=F32)


def _dot_tn(a, b):
    return lax.dot_general(a, b, (((0,), (0,)), ((), ())), preferred_element_type=F32)


def _resident(shape):
    nd = len(shape)
    return pl.BlockSpec(shape, lambda *_: (0,) * nd, pipeline_mode=pl.Buffered(1))


def _layer_block(arr, index):
    nd = arr.ndim - 1
    return pl.BlockSpec((None,) + arr.shape[1:], lambda *_: (index,) + (0,) * nd, pipeline_mode=pl.Buffered(1))


def _ffn_kernel(x_ref, g_ref, wg_ref, wu_ref, wd_ref, *rest, fchunk, final):
    if final:
        gf_ref, o_ref, y_ref, h_ref, a_ref = rest
    else:
        o_ref, h_ref, a_ref = rest
    h_ref[...] = _rms(x_ref[...], g_ref[...]).astype(BF16)
    for j in range(D_FF // fchunk):
        sl = slice(j * fchunk, (j + 1) * fchunk)
        h = h_ref[...]
        gate = _dot(h, wg_ref[:, sl])
        up = _dot(h, wu_ref[:, sl])
        a_ref[:, sl] = (_silu(gate) * up).astype(BF16)
    out = x_ref[...] + FFN_RES * _dot(a_ref[...], wd_ref[...])
    o_ref[...] = out
    if final:
        y_ref[...] = _rms(out, gf_ref[...])


def _ffn(x, g, wg, wu, wd, layer, g_final=None):
    m = x.shape[0]
    tm = min(m, 1024)
    final = g_final is not None
    row = pl.BlockSpec((tm, D_MODEL), lambda i: (i, 0))
    in_specs = [row, _layer_block(g, layer), _layer_block(wg, layer), _layer_block(wu, layer), _layer_block(wd, layer)]
    args = [x, g, wg, wu, wd]
    out_shape = jax.ShapeDtypeStruct((m, D_MODEL), F32)
    out_specs = row
    if final:
        in_specs.append(_resident((1, D_MODEL)))
        args.append(g_final.reshape(1, D_MODEL))
        out_shape = (out_shape, out_shape)
        out_specs = (row, row)
    return pl.pallas_call(
        functools.partial(_ffn_kernel, fchunk=256, final=final),
        out_shape=out_shape, grid=(m // tm,), in_specs=in_specs, out_specs=out_specs,
        scratch_shapes=[pltpu.VMEM((tm, D_MODEL), BF16), pltpu.VMEM((tm, D_FF), BF16)],
        compiler_params=_params(), name="ffn_final" if final else "ffn",
    )(*args)


def _matmul_chunks(h, w_ref, o_ref, nchunk, res_ref=None):
    n = w_ref.shape[1]
    for n0 in range(0, n, nchunk):
        n1 = min(n, n0 + nchunk)
        y = _dot(h, w_ref[:, n0:n1])
        if res_ref is not None:
            y = y + res_ref[:, n0:n1]
        o_ref[:, n0:n1] = y.astype(o_ref.dtype)


def _proj_kernel(*refs, norm, res, nchunk):
    refs = list(refs)
    x_ref = refs.pop(0)
    g_ref = refs.pop(0) if norm else None
    w_ref = refs.pop(0)
    r_ref = refs.pop(0) if res else None
    o_ref = refs.pop(0)
    if norm:
        h = _rms(x_ref[...], g_ref[...]).astype(BF16)
    else:
        h = x_ref[...].astype(BF16)
    _matmul_chunks(h, w_ref, o_ref, nchunk, r_ref)


def _proj(x, w, w_index, g=None, g_index=None, res=None, out_dtype=F32):
    m, k = x.shape
    n = w.shape[-1]
    tm = min(m, 512)
    in_specs = [pl.BlockSpec((tm, k), lambda i: (i, 0))]
    args = [x]
    if g is not None:
        in_specs.append(_layer_block(g, g_index))
        args.append(g)
    in_specs.append(_layer_block(w, w_index))
    args.append(w)
    if res is not None:
        in_specs.append(pl.BlockSpec((tm, n), lambda i: (i, 0)))
        args.append(res)
    return pl.pallas_call(
        functools.partial(_proj_kernel, norm=g is not None, res=res is not None, nchunk=512),
        out_shape=jax.ShapeDtypeStruct((m, n), out_dtype), grid=(m // tm,), in_specs=in_specs,
        out_specs=pl.BlockSpec((tm, n), lambda i: (i, 0)),
        compiler_params=_params(), name="proj",
    )(*args)


def _inproj_kernel(x_ref, g_ref, wm_ref, wg_ref, om_ref, og_ref, *, nchunk):
    h = _rms(x_ref[...], g_ref[...]).astype(BF16)
    _matmul_chunks(h, wm_ref, om_ref, nchunk)
    og_ref[...] = _dot(h, wg_ref[...].astype(BF16))


def _inproj(x, g, g_index, w_main, w_gate, w_index, main_dtype):
    m = x.shape[0]
    n = w_main.shape[-1]
    tm = min(m, 1024)
    rows = lambda width: pl.BlockSpec((tm, width), lambda i: (i, 0))
    return pl.pallas_call(
        functools.partial(_inproj_kernel, nchunk=512),
        out_shape=(jax.ShapeDtypeStruct((m, n), main_dtype), jax.ShapeDtypeStruct((m, GATE_COLS), F32)),
        grid=(m // tm,),
        in_specs=[rows(D_MODEL), _layer_block(g, g_index), _layer_block(w_main, w_index), _layer_block(w_gate, w_index)],
        out_specs=(rows(n), rows(GATE_COLS)),
        compiler_params=_params(), name="inproj",
    )(x, g, w_main, w_gate)


def _memkv_kernel(x_ref, g_ref, wk_ref, wv_ref, k_ref, v_ref, kb_ref, vb_ref):
    h = _rms(x_ref[...], g_ref[...]).astype(BF16)
    k = _dot(h, wk_ref[...])
    v = _dot(h, wv_ref[...])
    kb_ref[0] = k.astype(BF16)
    vb_ref[0] = v.astype(BF16)
    for i in range(XA_HEADS):
        sl = slice(i * XA_DH, (i + 1) * XA_DH)
        k_ref[0, :, i, :] = k[:, sl]
        v_ref[0, :, i, :] = v[:, sl]


def _memkv(mem, g_mem, wk, wv):
    m = mem.shape[0]
    tm = 512
    w_spec = pl.BlockSpec((None, D_MODEL, D_MODEL), lambda l, i: (l, 0, 0))
    o_spec = pl.BlockSpec((1, tm, XA_HEADS, XA_DH), lambda l, i: (l, i, 0, 0))
    b_spec = pl.BlockSpec((1, tm, D_MODEL), lambda l, i: (l, i, 0))
    o_shape = jax.ShapeDtypeStruct((DEPTH, m, XA_HEADS, XA_DH), F32)
    b_shape = jax.ShapeDtypeStruct((DEPTH, m, D_MODEL), BF16)
    return pl.pallas_call(
        _memkv_kernel, out_shape=(o_shape, o_shape, b_shape, b_shape), grid=(DEPTH, m // tm),
        in_specs=[pl.BlockSpec((tm, D_MODEL), lambda l, i: (i, 0)),
                  pl.BlockSpec((None, 1, D_MODEL), lambda l, i: (l, 0, 0)), w_spec, w_spec],
        out_specs=(o_spec, o_spec, b_spec, b_spec), compiler_params=_params(), name="memkv",
    )(mem, g_mem, wk, wv)


def _xattn_kernel(x_ref, a_ref, k_ref, v_ref, g_ref, wout_ref, wq_ref, wo_ref, o_ref, att_ref):
    x2 = x_ref[0] + _dot(a_ref[0], wout_ref[...])
    q = _dot(_rms(x2, g_ref[...]).astype(BF16), wq_ref[...]).astype(BF16)
    for h in range(XA_HEADS):
        sl = slice(h * XA_DH, (h + 1) * XA_DH)
        s = _dot_nt(q[:, sl], k_ref[0, :, sl]) * XA_DH ** -0.5
        e = jnp.exp(s - jnp.max(s, axis=-1, keepdims=True))
        p = e / jnp.sum(e, axis=-1, keepdims=True)
        att_ref[:, sl] = _dot(p.astype(BF16), v_ref[0, :, sl]).astype(BF16)
    o_ref[0] = x2 + _dot(att_ref[...], wo_ref[...])


def _xattn_block(x, a, mem_kb, mem_vb, layer, g, w_out, w_out_index, wq, wo):
    b, l, _ = x.shape
    tq = min(l, 1024)
    row = pl.BlockSpec((1, tq, D_MODEL), lambda i, j: (i, j, 0))
    kv_spec = pl.BlockSpec((None, 1, N_MEM, D_MODEL), lambda i, j: (layer, i, 0, 0))
    return pl.pallas_call(
        _xattn_kernel, out_shape=jax.ShapeDtypeStruct((b, l, D_MODEL), F32), grid=(b, l // tq),
        in_specs=[row, row, kv_spec, kv_spec, _layer_block(g, layer), _layer_block(w_out, w_out_index),
                  _layer_block(wq, layer), _layer_block(wo, layer)],
        out_specs=row, scratch_shapes=[pltpu.VMEM((tq, D_MODEL), BF16)],
        compiler_params=_params(), name="xattn",
    )(x, a, mem_kb, mem_vb, g, w_out, wq, wo)


ATTN_STEP_BATCH = 8


def _attn_step_kernel(q_ref, k_ref, v_ref, o_ref):
    nh = XA_HEADS
    pairs = (N_MEM // 2, 2 * nh, XA_DH)
    for i in range(ATTN_STEP_BATCH):
        q = q_ref[0, i] * XA_DH ** -0.5
        q2 = jnp.concatenate([q, q], axis=0)
        s = jnp.sum(k_ref[0, i].reshape(pairs) * q2[None], axis=-1, keepdims=True)
        mx = jnp.max(s, axis=0, keepdims=True)
        mx = jnp.maximum(mx[:, :nh], mx[:, nh:])
        e = jnp.exp(s - jnp.concatenate([mx, mx], axis=1))
        acc = jnp.sum(e * v_ref[0, i].reshape(pairs), axis=0)
        den = jnp.sum(e, axis=0)
        o_ref[0, i] = (acc[:nh] + acc[nh:]) / (den[:nh] + den[nh:])


def _attn_step(q, mem_k, mem_v, layer):
    b = q.shape[0]
    bb = ATTN_STEP_BATCH
    kv_spec = pl.BlockSpec((1, bb, N_MEM, XA_HEADS, XA_DH), lambda i: (layer, i, 0, 0, 0))
    q_spec = pl.BlockSpec((1, bb, XA_HEADS, XA_DH), lambda i: (i, 0, 0, 0))
    out = pl.pallas_call(
        _attn_step_kernel, out_shape=jax.ShapeDtypeStruct((b // bb, bb, XA_HEADS, XA_DH), F32), grid=(b // bb,),
        in_specs=[q_spec, kv_spec, kv_spec], out_specs=q_spec,
        compiler_params=_params(), name="attn_step",
    )(q.reshape(b // bb, bb, XA_HEADS, XA_DH), mem_k, mem_v)
    return out.reshape(b, D_MODEL)


def _zero_other_layers(ref):
    ref[1:] = jnp.zeros((ref.shape[0] - 1,) + ref.shape[1:], F32)


def _state_out(n_layers, layer_j, bsz, bb, tail, idx, prev):
    shape = jax.ShapeDtypeStruct((n_layers, bsz) + tail, F32)
    zeros = (0,) * len(tail)
    if prev is None:
        spec = pl.BlockSpec((n_layers, bb) + tail, lambda *g: (0, idx(*g)) + zeros)
        return shape, spec, [], [], True
    spec = pl.BlockSpec((1, bb) + tail, lambda *g: (layer_j, idx(*g)) + zeros)
    return shape, spec, [pl.BlockSpec(memory_space=pl.ANY)], [prev], False


def _lane_select(h, value, into):
    lane = lax.broadcasted_iota(jnp.int32, into.shape, 1)
    return jnp.where(lane == h, value, into)


def _mlstm_chunk_kernel(p_ref, gt_ref, c0_ref, n0_ref, m0_ref, bi_ref, bf_ref, gh_ref, *rest, c, wide, aliased):
    if aliased:
        rest = rest[1:]
    o_ref, c_ref, n_ref, m_ref = rest
    rows = range(ML_CHUNK_ROWS)

    @pl.when(pl.program_id(1) == 0)
    def _():
        for r in rows:
            c_ref[0, r] = c0_ref[0, r]
            n_ref[r] = n0_ref[0, r]
            m_ref[r] = m0_ref[0, r]
        if wide:
            _zero_other_layers(c_ref)

    hq = ML_HEADS * ML_DQK
    row = lax.broadcasted_iota(jnp.int32, (c, c), 0)
    col = lax.broadcasted_iota(jnp.int32, (c, c), 1)
    incl = row >= col
    scale = ML_DQK ** -0.5
    i_pres, bs, xts, m_prevs, mt_alls = [], [], [], [], []
    t_idx = lax.broadcasted_iota(jnp.int32, (c, LANES), 0)
    for r in rows:
        i_pre = gt_ref[r, :, 0:LANES] + bi_ref[...]
        log_f = _log_sigmoid(gt_ref[r, :, LANES:GATE_COLS] + bf_ref[...])
        b = jnp.dot(incl.astype(F32), log_f, precision=HI, preferred_element_type=F32)
        x = i_pre - b
        cm = x
        shift = 1
        while shift < c:
            cm = jnp.maximum(cm, jnp.where(t_idx >= shift, pltpu.roll(cm, shift, 0), -jnp.inf))
            shift *= 2
        i_pres.append(i_pre)
        bs.append(b)
        xts.append(x.T)
        m_prevs.append(m_ref[r])
        mt_alls.append(b + jnp.maximum(m_ref[r], cm))
    probs = [(r, h) for r in rows for h in range(ML_HEADS)]
    n = range(len(probs))
    bcols = [bs[r][:, h:h + 1] for r, h in probs]
    icols = [i_pres[r][:, h:h + 1] for r, h in probs]
    mprev = [m_prevs[r][:, h:h + 1] for r, h in probs]
    qbs = [p_ref[r, :, h * ML_DQK:(h + 1) * ML_DQK] for r, h in probs]
    kbs = [p_ref[r, :, hq + h * ML_DQK:hq + (h + 1) * ML_DQK] for r, h in probs]
    vbs = [p_ref[r, :, 2 * hq + h * ML_DV:2 * hq + (h + 1) * ML_DV] for r, h in probs]
    dms = [jnp.where(incl, bcols[i] + xts[r][h:h + 1, :], -jnp.inf) for i, (r, h) in enumerate(probs)]
    mts = [mt_alls[r][:, h:h + 1] for r, h in probs]
    w_inters = [jnp.exp(bcols[i] + mprev[i] - mts[i]) for i in n]
    ss = [_dot_nt(qbs[i], kbs[i]) * (jnp.exp(dms[i] - mts[i]) * scale) for i in n]
    c_old = [c_ref[0, r, h] for r, h in probs]
    n_old = [n_ref[r, h:h + 1, :] for r, h in probs]
    nums = [w_inters[i] * _dot(qbs[i], c_old[i].astype(BF16)) + _dot(ss[i].astype(BF16), vbs[i]) for i in n]
    m_rows = list(m_prevs)
    for i, (r, h) in enumerate(probs):
        m_new = mts[i][c - 1:c, :]
        b_last = bcols[i][c - 1:c, :]
        w_k = jnp.exp(b_last - bcols[i] + icols[i] - m_new) * scale
        decay = jnp.exp(b_last + mprev[i] - m_new)
        kw = kbs[i].astype(F32) * w_k
        c_ref[0, r, h] = decay * c_old[i] + _dot_tn(kw.astype(BF16), vbs[i])
        n_ref[r, h:h + 1, :] = decay * n_old[i] + jnp.sum(kw, axis=0, keepdims=True)
        m_rows[r] = _lane_select(h, m_new, m_rows[r])
    for r in rows:
        m_ref[r] = m_rows[r]
    for i, (r, h) in enumerate(probs):
        den = (w_inters[i] * jnp.sum(qbs[i].astype(F32) * n_old[i], axis=-1, keepdims=True)
               + jnp.sum(ss[i], axis=-1, keepdims=True))
        hh = nums[i] / jnp.maximum(jnp.abs(den), jnp.exp(-mts[i]))
        vs = slice(h * ML_DV, (h + 1) * ML_DV)
        hn = hh * lax.rsqrt(jnp.mean(hh * hh, axis=-1, keepdims=True) + EPS) * gh_ref[:, vs]
        og = p_ref[r, :, 2 * hq + D_MODEL + h * ML_DV:2 * hq + D_MODEL + (h + 1) * ML_DV].astype(F32)
        o_ref[r, :, vs] = (hn * jax.nn.sigmoid(og)).astype(o_ref.dtype)


def _mlstm_small_specs(bsz, layer_j, idx, bb):
    c_in = pl.BlockSpec((1, bb, ML_HEADS, ML_DQK, ML_DV), lambda *g: (layer_j, idx(*g), 0, 0, 0))
    n_in = pl.BlockSpec((1, bb, ML_HEADS, ML_DQK), lambda *g: (layer_j, idx(*g), 0, 0))
    m_in = pl.BlockSpec((1, bb, 1, LANES), lambda *g: (layer_j, idx(*g), 0, 0))
    n_out = pl.BlockSpec((bb, ML_HEADS, ML_DQK), lambda *g: (idx(*g), 0, 0))
    m_out = pl.BlockSpec((bb, 1, LANES), lambda *g: (idx(*g), 0, 0))
    shapes = (jax.ShapeDtypeStruct((bsz, ML_HEADS, ML_DQK), F32), jax.ShapeDtypeStruct((bsz, 1, LANES), F32))
    return (c_in, n_in, m_in), (n_out, m_out), shapes


def _mlstm_call(kernel_fn, name, p, gates, c0, n0, m0, layer_j, b_i, b_f, g_head, c_prev, grid, row_block, idx, bb):
    bsz = p.shape[0]
    n_layers = c0.shape[0]
    s_in, s_out, s_shapes = _mlstm_small_specs(bsz, layer_j, idx, bb)
    c_shape, c_spec, extra_specs, extra_args, wide = _state_out(
        n_layers, layer_j, bsz, bb, (ML_HEADS, ML_DQK, ML_DV), idx, c_prev)
    n_in = 8
    return pl.pallas_call(
        functools.partial(kernel_fn, wide=wide, aliased=not wide),
        out_shape=(jax.ShapeDtypeStruct(p.shape[:2] + (D_MODEL,), BF16), c_shape) + s_shapes,
        grid=grid,
        in_specs=[row_block(ML_MAIN), row_block(GATE_COLS), *s_in, _layer_block(b_i, layer_j),
                  _layer_block(b_f, layer_j), _layer_block(g_head, layer_j), *extra_specs],
        out_specs=(row_block(D_MODEL), c_spec) + s_out,
        input_output_aliases={} if wide else {n_in: 1},
        compiler_params=_params(), name=name,
    )(p, gates, c0, n0, m0, b_i, b_f, g_head, *extra_args)


def _mlstm_chunked(p, gates, c0, n0, m0, layer_j, b_i, b_f, g_head, c_prev):
    bsz, length, _ = p.shape
    bb = ML_CHUNK_ROWS
    row_block = lambda width: pl.BlockSpec((bb, ML_CHUNK, width), lambda i, j: (i, j, 0))
    return _mlstm_call(functools.partial(_mlstm_chunk_kernel, c=ML_CHUNK), "mlstm_chunk", p, gates, c0, n0, m0, layer_j,
                       b_i, b_f, g_head, c_prev, (bsz // bb, length // ML_CHUNK), row_block, lambda i, j: i, bb)


STEP_BATCH = 8


def _row0(x, rows=SUBLANES):
    r = lax.broadcasted_iota(jnp.int32, (rows, x.shape[1]), 0)
    return jnp.where(r == 0, x, 0.0)


def _mlstm_step_kernel(p_ref, gt_ref, c0_ref, n0_ref, m0_ref, bi_ref, bf_ref, gh_ref, *rest, wide, aliased):
    if aliased:
        rest = rest[1:]
    o_ref, c_ref, n_ref, m_ref = rest
    hq = ML_HEADS * ML_DQK
    if wide:
        _zero_other_layers(c_ref)
    w_inters, w_ks, e_invs = [], [], []
    for i in range(STEP_BATCH):
        i_pre = gt_ref[i, :, 0:LANES] + bi_ref[...]
        log_f = _log_sigmoid(gt_ref[i, :, LANES:GATE_COLS] + bf_ref[...])
        m_prev = m0_ref[0, i]
        mt = jnp.maximum(log_f + m_prev, i_pre)
        w_inters.append(jnp.exp(log_f + m_prev - mt))
        w_ks.append(jnp.exp(i_pre - mt))
        e_invs.append(jnp.exp(-mt))
        m_ref[i] = mt
    probs = [(i, h) for i in range(STEP_BATCH) for h in range(ML_HEADS)]
    n = range(len(probs))
    qs = [p_ref[i, :, h * ML_DQK:(h + 1) * ML_DQK].astype(F32) for i, h in probs]
    ks = [p_ref[i, :, hq + h * ML_DQK:hq + (h + 1) * ML_DQK].astype(F32) * ML_DQK ** -0.5 for i, h in probs]
    vs = [p_ref[i, :, 2 * hq + h * ML_DV:2 * hq + (h + 1) * ML_DV].astype(F32) for i, h in probs]
    c_old = [c0_ref[0, i, h] for i, h in probs]
    n_old = [n0_ref[0, i, h:h + 1, :] for i, h in probs]
    wis = [w_inters[i][:, h:h + 1] for i, h in probs]
    wks = [w_ks[i][:, h:h + 1] for i, h in probs]
    q_cs = [_dot(_row0(qs[j]).astype(BF16), c_old[j].astype(BF16))[0:1] for j in n]
    kws = [ks[j] * wks[j] for j in n]
    for j, (i, h) in enumerate(probs):
        v8 = jnp.broadcast_to(vs[j], (SUBLANES, ML_DV)).astype(BF16)
        c_ref[0, i, h] = wis[j] * c_old[j] + _dot_tn(_row0(kws[j]).astype(BF16), v8)
        n_ref[i, h:h + 1, :] = wis[j] * n_old[j] + kws[j]
    stack = lambda xs: jnp.concatenate(xs, axis=0)
    sls = [slice(h * ML_DV, (h + 1) * ML_DV) for _, h in probs]
    q_all, k_all, v_all, wi_all, wk_all = stack(qs), stack(ks), stack(vs), stack(wis), stack(wks)
    s = jnp.sum(q_all * k_all, axis=-1, keepdims=True) * wk_all
    num = wi_all * stack(q_cs) + s * v_all
    den = wi_all * jnp.sum(q_all * stack(n_old), axis=-1, keepdims=True) + s
    hh = num / jnp.maximum(jnp.abs(den), stack([e_invs[i][:, h:h + 1] for i, h in probs]))
    hn = hh * lax.rsqrt(jnp.mean(hh * hh, axis=-1, keepdims=True) + EPS) * stack([gh_ref[:, sl] for sl in sls])
    og = stack([p_ref[i, :, 2 * hq + D_MODEL + h * ML_DV:2 * hq + D_MODEL + (h + 1) * ML_DV] for i, h in probs])
    out = (hn * jax.nn.sigmoid(og.astype(F32))).astype(o_ref.dtype)
    for j, (i, h) in enumerate(probs):
        o_ref[i, :, sls[j]] = out[j:j + 1]


def _mlstm_step(p, gates, c0, n0, m0, layer_j, b_i, b_f, g_head, c_prev):
    bsz = p.shape[0]
    bb = STEP_BATCH
    row_block = lambda width: pl.BlockSpec((bb, 1, width), lambda i: (i, 0, 0))
    return _mlstm_call(_mlstm_step_kernel, "mlstm_step", p, gates, c0, n0, m0, layer_j,
                       b_i, b_f, g_head, c_prev, (bsz // bb,), row_block, lambda i: i, bb)


def _l2n(x):
    return x * lax.rsqrt(jnp.sum(x * x, axis=-1, keepdims=True) + EPS)


INV_BASE = 16


def _unit_lower_inverses(a_list, c):
    row = lax.broadcasted_iota(jnp.int32, (c, c), 0)
    col = lax.broadcasted_iota(jnp.int32, (c, c), 1)
    eye = jnp.where(row == col, 1.0, 0.0)
    same = lambda size: (row >> (size.bit_length() - 1)) == (col >> (size.bit_length() - 1))
    ns = [jnp.where(same(INV_BASE), -a, 0.0) for a in a_list]
    ts = [eye + n for n in ns]
    nbs = [n.astype(BF16) for n in ns]
    power = 2
    while power < INV_BASE:
        ns = [_dot(nb, nb) for nb in nbs]
        nbs = [n.astype(BF16) for n in ns]
        ts = [t + _dot(t.astype(BF16), nb) for t, nb in zip(ts, nbs)]
        power *= 2
    size = INV_BASE
    while size < c:
        off = same(2 * size) & jnp.logical_not(same(size))
        tbs = [t.astype(BF16) for t in ts]
        mids = [_dot(tb, jnp.where(off, a, 0.0).astype(BF16)).astype(BF16) for tb, a in zip(tbs, a_list)]
        ts = [t - _dot(mid, tb) for t, mid, tb in zip(ts, mids, tbs)]
        size *= 2
    return ts


def _gdn_gates(gt_ref, i, alog_ref, dtb_ref):
    beta = jax.nn.sigmoid(gt_ref[i, :, 0:LANES])
    log_g = -jnp.exp(alog_ref[...]) * _softplus(gt_ref[i, :, LANES:GATE_COLS] + dtb_ref[...])
    return beta, log_g


def _gdn_out(o, z, gout):
    return (o * lax.rsqrt(jnp.mean(o * o, axis=-1, keepdims=True) + EPS) * gout * _silu(z))


def _inproj_gdn_kernel(x_ref, g_ref, wm_ref, wg_ref, cp_ref, cw_ref, om_ref, og_ref, nc_ref, e_ref,
                       *, tiles_per_seq, nchunk):
    assert GD_CONV == 4, "the conv below pairs four taps"
    tm = x_ref.shape[0]
    pad = SUBLANES
    keep = GD_CONV - 1
    h = _rms(x_ref[...], g_ref[...]).astype(BF16)
    og_ref[...] = _dot(h, wg_ref[...].astype(BF16))

    @pl.when(pl.program_id(0) % tiles_per_seq == 0)
    def _():
        e_ref[...] = jnp.zeros((pad, GD_QKV), F32)
        e_ref[pl.ds(pad - keep, keep), :] = cp_ref[0, 0]

    hk = GD_HEADS * GD_DK
    starts = list(range(0, GD_QKV, nchunk))
    z_starts = list(range(GD_QKV, GD_MAIN, nchunk))
    every = len(starts) // len(z_starts)
    for idx, n0 in enumerate(starts):
        if idx % every == 0:
            z0 = z_starts[idx // every]
            om_ref[:, z0:z0 + nchunk] = _dot(h, wm_ref[:, z0:z0 + nchunk]).astype(om_ref.dtype)
        tile = _dot(h, wm_ref[:, n0:n0 + nchunk])
        ext = jnp.concatenate([e_ref[:, n0:n0 + nchunk], tile], axis=0)
        nc_ref[0, :, n0:n0 + nchunk] = tile[tm - keep:]
        e_ref[:, n0:n0 + nchunk] = tile[tm - pad:]
        for d0 in range(0, nchunk, GD_DK):
            sl = slice(n0 + d0, n0 + d0 + GD_DK)
            x = ext[:, d0:d0 + GD_DK]
            x1 = pltpu.roll(x, 1, 0)
            near = x * cw_ref[3:4, sl] + x1 * cw_ref[2:3, sl]
            far = x * cw_ref[1:2, sl] + x1 * cw_ref[0:1, sl]
            conv = _silu((near + pltpu.roll(far, 2, 0))[pad:])
            if n0 + d0 < hk:
                conv = _l2n(conv) * GD_DK ** -0.5
            elif n0 + d0 < 2 * hk:
                conv = _l2n(conv)
            om_ref[:, sl] = conv.astype(om_ref.dtype)


def _inproj_gdn(x, g, g_index, w_main, w_gate, w_index, conv_prev, conv_w, bsz):
    m = x.shape[0]
    tm = 1024
    tiles_per_seq = m // bsz // tm
    rows = lambda width: pl.BlockSpec((tm, width), lambda i: (i, 0))
    return pl.pallas_call(
        functools.partial(_inproj_gdn_kernel, tiles_per_seq=tiles_per_seq, nchunk=256),
        out_shape=(jax.ShapeDtypeStruct((m, GD_MAIN), BF16), jax.ShapeDtypeStruct((m, GATE_COLS), F32),
                   jax.ShapeDtypeStruct((bsz, GD_CONV - 1, GD_QKV), F32)),
        grid=(m // tm,),
        in_specs=[rows(D_MODEL), _layer_block(g, g_index), _layer_block(w_main, w_index), _layer_block(w_gate, w_index),
                  pl.BlockSpec((1, 1, GD_CONV - 1, GD_QKV), lambda i: (w_index, i // tiles_per_seq, 0, 0)),
                  _layer_block(conv_w, w_index)],
        out_specs=(rows(GD_MAIN), rows(GATE_COLS),
                   pl.BlockSpec((1, GD_CONV - 1, GD_QKV), lambda i: (i // tiles_per_seq, 0, 0))),
        scratch_shapes=[pltpu.VMEM((SUBLANES, GD_QKV), F32)],
        compiler_params=_params(), name="inproj_gdn",
    )(x, g, w_main, w_gate, conv_prev, conv_w)


def _gdn_chunk_kernel(p_ref, gt_ref, alog_ref, dtb_ref, gout_ref, s0_ref, *rest, c, wide, aliased):
    if aliased:
        rest = rest[1:]
    o_ref, s_ref = rest
    rows = range(GD_CHUNK_ROWS)

    @pl.when(pl.program_id(1) == 0)
    def _():
        for r in rows:
            s_ref[0, r] = s0_ref[0, r]
        if wide:
            _zero_other_layers(s_ref)

    row = lax.broadcasted_iota(jnp.int32, (c, c), 0)
    col = lax.broadcasted_iota(jnp.int32, (c, c), 1)
    incl = row >= col
    strict = row > col
    hk = GD_HEADS * GD_DK
    betas, gams, gam_ts, e_gams = [], [], [], []
    for r in rows:
        beta, log_g = _gdn_gates(gt_ref, r, alog_ref, dtb_ref)
        gam = jnp.dot(incl.astype(F32), log_g, precision=HI, preferred_element_type=F32)
        betas.append(beta)
        gams.append(gam)
        gam_ts.append(gam.T)
        e_gams.append(jnp.exp(gam))
    probs = [(r, h) for r in rows for h in range(GD_HEADS)]
    col_of = lambda xs, r, h: xs[r][:, h:h + 1]
    qbs = [p_ref[r, :, h * GD_DK:(h + 1) * GD_DK] for r, h in probs]
    kbs = [p_ref[r, :, hk + h * GD_DK:hk + (h + 1) * GD_DK] for r, h in probs]
    vs = [p_ref[r, :, 2 * hk + h * GD_DV:2 * hk + (h + 1) * GD_DV].astype(F32) for r, h in probs]
    ks = [kb.astype(F32) for kb in kbs]
    bcols = [col_of(betas, r, h) for r, h in probs]
    gcols = [col_of(gams, r, h) for r, h in probs]
    ecols = [col_of(e_gams, r, h) for r, h in probs]
    n = range(len(probs))
    decs = [jnp.exp(jnp.where(incl, gcols[i] - gam_ts[r][h:h + 1, :], -jnp.inf)) for i, (r, h) in enumerate(probs)]
    kqs = [_dot_nt(jnp.concatenate([kbs[i], qbs[i]], axis=0), kbs[i]) for i in n]
    a_list = [jnp.where(strict, bcols[i] * kqs[i][0:c] * decs[i], 0.0) for i in n]
    ts = _unit_lower_inverses(a_list, c)
    rhs = [jnp.concatenate([bcols[i] * vs[i], (bcols[i] * ecols[i]) * ks[i]], axis=1).astype(BF16) for i in n]
    uws = [_dot(ts[i].astype(BF16), rhs[i]) for i in n]
    s_old = [s_ref[0, r, h] for r, h in probs]
    wq = [jnp.concatenate([uws[i][:, GD_DV:], qbs[i].astype(F32) * ecols[i]], axis=0).astype(BF16) for i in n]
    wqs = [_dot(wq[i], s_old[i].astype(BF16)) for i in n]
    ubs = [(uws[i][:, :GD_DV] - wqs[i][0:c]).astype(BF16) for i in n]
    for i, (r, h) in enumerate(probs):
        g_last = gcols[i][c - 1:c, :]
        k_dec = ks[i] * jnp.exp(g_last - gcols[i])
        s_ref[0, r, h] = jnp.exp(g_last) * s_old[i] + _dot_tn(k_dec.astype(BF16), ubs[i])
    for i, (r, h) in enumerate(probs):
        o = wqs[i][c:2 * c] + _dot((kqs[i][c:2 * c] * decs[i]).astype(BF16), ubs[i])
        z = p_ref[r, :, GD_QKV + h * GD_DV:GD_QKV + (h + 1) * GD_DV].astype(F32)
        o_ref[r, :, h * GD_DV:(h + 1) * GD_DV] = _gdn_out(o, z, gout_ref[...]).astype(o_ref.dtype)


def _gdn_chunked(p, gates, s0, layer_j, a_log, dt_bias, g_out, s_prev):
    bsz, length, _ = p.shape
    n_layers = s0.shape[0]
    bb = GD_CHUNK_ROWS
    idx = lambda i, j: i
    row_block = lambda width: pl.BlockSpec((bb, GD_CHUNK, width), lambda i, j: (i, j, 0))
    s_in = pl.BlockSpec((1, bb, GD_HEADS, GD_DK, GD_DV), lambda i, j: (layer_j, i, 0, 0, 0))
    s_shape, s_spec, extra_specs, extra_args, wide = _state_out(
        n_layers, layer_j, bsz, bb, (GD_HEADS, GD_DK, GD_DV), idx, s_prev)
    n_in = 6
    return pl.pallas_call(
        functools.partial(_gdn_chunk_kernel, c=GD_CHUNK, wide=wide, aliased=not wide),
        out_shape=(jax.ShapeDtypeStruct((bsz, length, D_MODEL), BF16), s_shape),
        grid=(bsz // bb, length // GD_CHUNK),
        in_specs=[row_block(GD_MAIN), row_block(GATE_COLS), _layer_block(a_log, layer_j),
                  _layer_block(dt_bias, layer_j), _layer_block(g_out, layer_j), s_in, *extra_specs],
        out_specs=(row_block(D_MODEL), s_spec),
        input_output_aliases={} if wide else {n_in: 1},
        compiler_params=_params(), name="gdn_chunk",
    )(p, gates, a_log, dt_bias, g_out, s0, *extra_args)


def _gdn_call(kernel_fn, name, p, gates, conv_prev, s0, layer_j, conv_w, a_log, dt_bias, g_out, s_prev,
              grid, row_block, idx, bb, scratch):
    bsz = p.shape[0]
    n_layers = s0.shape[0]
    cp_in = pl.BlockSpec((1, bb, GD_CONV - 1, GD_QKV), lambda *g: (layer_j, idx(*g), 0, 0))
    s_in = pl.BlockSpec((1, bb, GD_HEADS, GD_DK, GD_DV), lambda *g: (layer_j, idx(*g), 0, 0, 0))
    nc_out = pl.BlockSpec((bb, GD_CONV - 1, GD_QKV), lambda *g: (idx(*g), 0, 0))
    s_shape, s_spec, extra_specs, extra_args, wide = _state_out(
        n_layers, layer_j, bsz, bb, (GD_HEADS, GD_DK, GD_DV), idx, s_prev)
    n_in = 8
    return pl.pallas_call(
        functools.partial(kernel_fn, wide=wide, aliased=not wide),
        out_shape=(jax.ShapeDtypeStruct(p.shape[:2] + (D_MODEL,), BF16), s_shape,
                   jax.ShapeDtypeStruct((bsz, GD_CONV - 1, GD_QKV), F32)),
        grid=grid,
        in_specs=[row_block(GD_MAIN), row_block(GATE_COLS), cp_in, _layer_block(conv_w, layer_j),
                  _layer_block(a_log, layer_j), _layer_block(dt_bias, layer_j), _layer_block(g_out, layer_j), s_in,
                  *extra_specs],
        out_specs=(row_block(D_MODEL), s_spec, nc_out),
        input_output_aliases={} if wide else {n_in: 1},
        scratch_shapes=scratch, compiler_params=_params(), name=name,
    )(p, gates, conv_prev, conv_w, a_log, dt_bias, g_out, s0, *extra_args)


def _gdn_step_kernel(p_ref, gt_ref, cp_ref, cw_ref, alog_ref, dtb_ref, gout_ref, s0_ref, *rest, wide, aliased):
    if aliased:
        rest = rest[1:]
    o_ref, s_ref, nc_ref = rest
    hk = GD_HEADS * GD_DK
    rows = lax.broadcasted_iota(jnp.int32, (SUBLANES, GD_DK), 0)
    if wide:
        _zero_other_layers(s_ref)
    convs, betas, gs = [], [], []
    for i in range(STEP_BATCH):
        qkv = p_ref[i, :, 0:GD_QKV]
        prev = cp_ref[0, i]
        conv = qkv * cw_ref[GD_CONV - 1:GD_CONV, :]
        for j in range(GD_CONV - 1):
            conv = conv + prev[j:j + 1, :] * cw_ref[j:j + 1, :]
        convs.append(_silu(conv))
        nc_ref[i] = jnp.concatenate([prev[1:GD_CONV - 1], qkv], axis=0)
        beta, log_g = _gdn_gates(gt_ref, i, alog_ref, dtb_ref)
        betas.append(beta)
        gs.append(jnp.exp(log_g))
    probs = [(i, h) for i in range(STEP_BATCH) for h in range(GD_HEADS)]
    n = range(len(probs))
    stack = lambda xs: jnp.concatenate(xs, axis=0)
    q_all = _l2n(stack([convs[i][:, h * GD_DK:(h + 1) * GD_DK] for i, h in probs])) * GD_DK ** -0.5
    k_all = _l2n(stack([convs[i][:, hk + h * GD_DK:hk + (h + 1) * GD_DK] for i, h in probs]))
    v_all = stack([convs[i][:, 2 * hk + h * GD_DV:2 * hk + (h + 1) * GD_DV] for i, h in probs])
    g_all = stack([gs[i][:, h:h + 1] for i, h in probs])
    b_all = stack([betas[i][:, h:h + 1] for i, h in probs])
    s_old = [s0_ref[0, i, h] for i, h in probs]
    kq_ss = [_dot(jnp.where(rows == 0, k_all[j:j + 1], jnp.where(rows == 1, q_all[j:j + 1], 0.0)).astype(BF16),
                  s_old[j].astype(BF16)) for j in n]
    u_all = b_all * (v_all - g_all * stack([kq_s[0:1] for kq_s in kq_ss]))
    for j, (i, h) in enumerate(probs):
        u8 = jnp.broadcast_to(u_all[j:j + 1], (SUBLANES, GD_DV)).astype(BF16)
        s_ref[0, i, h] = g_all[j:j + 1] * s_old[j] + _dot_tn(_row0(k_all[j:j + 1]).astype(BF16), u8)
    o_all = g_all * stack([kq_s[1:2] for kq_s in kq_ss]) + jnp.sum(q_all * k_all, axis=-1, keepdims=True) * u_all
    z_all = stack([p_ref[i, :, GD_QKV + h * GD_DV:GD_QKV + (h + 1) * GD_DV] for i, h in probs])
    out = _gdn_out(o_all, z_all, gout_ref[...]).astype(o_ref.dtype)
    for j, (i, h) in enumerate(probs):
        o_ref[i, :, h * GD_DV:(h + 1) * GD_DV] = out[j:j + 1]


def _gdn_step(p, gates, conv_prev, s0, layer_j, conv_w, a_log, dt_bias, g_out, s_prev):
    bsz = p.shape[0]
    bb = STEP_BATCH
    row_block = lambda width: pl.BlockSpec((bb, 1, width), lambda i: (i, 0, 0))
    return _gdn_call(_gdn_step_kernel, "gdn_step", p, gates, conv_prev, s0, layer_j,
                     conv_w, a_log, dt_bias, g_out, s_prev, (bsz // bb,), row_block, lambda i: i, bb, [])


def _lane_pad(x):
    return jnp.pad(x, [(0, 0)] * (x.ndim - 1) + [(0, LANES - x.shape[-1])])


def _prep_params(P):
    W = {}
    for name in ('ffn1_w_gate', 'ffn1_w_up', 'ffn1_w_down', 'ffn2_w_gate', 'ffn2_w_up', 'ffn2_w_down',
                 'ml_w_out', 'gd_w_out', 'xa_w_q', 'xa_w_k', 'xa_w_v', 'xa_w_o'):
        W[name] = P[name].astype(BF16)
    for name in ('g_ffn1', 'g_mix', 'g_xattn', 'g_mem', 'g_ffn2', 'ml_g_head', 'gd_g_out'):
        W[name] = P[name][:, None, :]
    for name in ('ml_b_i', 'ml_b_f', 'gd_a_log', 'gd_dt_bias'):
        W[name] = _lane_pad(P[name])[:, None, :]
    W['gd_conv_w'] = P['gd_conv_w']
    ml, gd = P['ml_w_in'], P['gd_w_in']
    W['ml_w_main'] = ml[..., :ML_MAIN].astype(BF16)
    W['ml_w_gate'] = jnp.concatenate(
        [_lane_pad(ml[..., ML_MAIN:ML_MAIN + ML_HEADS]), _lane_pad(ml[..., ML_MAIN + ML_HEADS:])], axis=-1)
    W['gd_w_main'] = gd[..., :GD_MAIN].astype(BF16)
    W['gd_w_gate'] = jnp.concatenate(
        [_lane_pad(gd[..., GD_MAIN:GD_MAIN + GD_HEADS]), _lane_pad(gd[..., GD_MAIN + GD_HEADS:])], axis=-1)
    return W


def _trunk(x, mem_k, mem_v, ml_c, ml_n, ml_m, gd_s, gd_conv, W):
    bsz, length, _ = x.shape
    single = length == 1
    x = x.reshape(bsz * length, D_MODEL)
    ml_m = _lane_pad(ml_m)[:, :, None, :]
    c_all = s_all = None
    new_n, new_m, new_conv = [], [], []
    y = None
    for layer in range(DEPTH):
        j = layer // 2
        x = _ffn(x, W['g_ffn1'], W['ffn1_w_gate'], W['ffn1_w_up'], W['ffn1_w_down'], layer)
        if layer % 2 == 0:
            p, gates = _inproj(x, W['g_mix'], layer, W['ml_w_main'], W['ml_w_gate'], j, BF16)
            fn = _mlstm_step if single else _mlstm_chunked
            a, c_all, n, m = fn(p.reshape(bsz, length, ML_MAIN), gates.reshape(bsz, length, GATE_COLS), ml_c, ml_n, ml_m,
                                j, W['ml_b_i'], W['ml_b_f'], W['ml_g_head'], c_all)
            new_n.append(n)
            new_m.append(m[:, 0, :ML_HEADS])
            w_out = W['ml_w_out']
        else:
            if single:
                p, gates = _inproj(x, W['g_mix'], layer, W['gd_w_main'], W['gd_w_gate'], j, F32)
                a, s_all, cv = _gdn_step(p.reshape(bsz, 1, GD_MAIN), gates.reshape(bsz, 1, GATE_COLS), gd_conv, gd_s,
                                         j, W['gd_conv_w'], W['gd_a_log'], W['gd_dt_bias'], W['gd_g_out'], s_all)
            else:
                p, gates, cv = _inproj_gdn(x, W['g_mix'], layer, W['gd_w_main'], W['gd_w_gate'], j, gd_conv,
                                           W['gd_conv_w'], bsz)
                a, s_all = _gdn_chunked(p.reshape(bsz, length, GD_MAIN), gates.reshape(bsz, length, GATE_COLS), gd_s,
                                        j, W['gd_a_log'], W['gd_dt_bias'], W['gd_g_out'], s_all)
            new_conv.append(cv)
            w_out = W['gd_w_out']
        if single:
            a = a.reshape(bsz, D_MODEL)
            x = _proj(a, w_out, j, res=x)
            q = _proj(x, W['xa_w_q'], layer, g=W['g_xattn'], g_index=layer)
            x = _proj(_attn_step(q, mem_k, mem_v, layer), W['xa_w_o'], layer, res=x)
        else:
            x = _xattn_block(x.reshape(bsz, length, D_MODEL), a, mem_k, mem_v, layer, W['g_xattn'], w_out, j,
                             W['xa_w_q'], W['xa_w_o']).reshape(bsz * length, D_MODEL)
        if layer == DEPTH - 1:
            x, y = _ffn(x, W['g_ffn2'], W['ffn2_w_gate'], W['ffn2_w_up'], W['ffn2_w_down'], layer,
                        g_final=W['g_final'])
        else:
            x = _ffn(x, W['g_ffn2'], W['ffn2_w_gate'], W['ffn2_w_up'], W['ffn2_w_down'], layer)
    return (y.reshape(bsz, length, D_MODEL), c_all, jnp.stack(new_n), jnp.stack(new_m), s_all, jnp.stack(new_conv))


def kernel(x_prompt, x_sample, mem_prompt, cache_mem_k, cache_mem_v, state_mlstm_C, state_mlstm_n, state_mlstm_m, state_gdn_S, state_gdn_conv, g_ffn1, ffn1_w_gate, ffn1_w_up, ffn1_w_down, g_mix, ml_w_in, ml_b_i, ml_b_f, ml_g_head, ml_w_out, gd_w_in, gd_conv_w, gd_a_log, gd_dt_bias, gd_g_out, gd_w_out, g_xattn, g_mem, xa_w_q, xa_w_k, xa_w_v, xa_w_o, g_ffn2, ffn2_w_gate, ffn2_w_up, ffn2_w_down, g_final):
    P = dict(g_ffn1=g_ffn1, ffn1_w_gate=ffn1_w_gate, ffn1_w_up=ffn1_w_up, ffn1_w_down=ffn1_w_down, g_mix=g_mix,
             ml_w_in=ml_w_in, ml_b_i=ml_b_i, ml_b_f=ml_b_f, ml_g_head=ml_g_head, ml_w_out=ml_w_out,
             gd_w_in=gd_w_in, gd_conv_w=gd_conv_w, gd_a_log=gd_a_log, gd_dt_bias=gd_dt_bias, gd_g_out=gd_g_out,
             gd_w_out=gd_w_out, g_xattn=g_xattn, g_mem=g_mem, xa_w_q=xa_w_q, xa_w_k=xa_w_k, xa_w_v=xa_w_v,
             xa_w_o=xa_w_o, g_ffn2=g_ffn2, ffn2_w_gate=ffn2_w_gate, ffn2_w_up=ffn2_w_up, ffn2_w_down=ffn2_w_down)
    W = _prep_params(P)
    W['g_final'] = g_final
    batch, n_mem, _ = mem_prompt.shape
    n_ml, n_gd = state_mlstm_C.shape[0], state_gdn_S.shape[0]

    pk, pv, pkb, pvb = _memkv(mem_prompt.reshape(batch * n_mem, D_MODEL), W['g_mem'], W['xa_w_k'], W['xa_w_v'])
    z_c = jnp.zeros((n_ml, batch, ML_HEADS, ML_DQK, ML_DV), F32)
    z_n = jnp.zeros((n_ml, batch, ML_HEADS, ML_DQK), F32)
    z_m = jnp.zeros((n_ml, batch, ML_HEADS), F32)
    z_s = jnp.zeros((n_gd, batch, GD_HEADS, GD_DK, GD_DV), F32)
    z_conv = jnp.zeros((n_gd, batch, GD_CONV - 1, GD_QKV), F32)
    y_p, p_c, p_n, p_m, p_s, p_conv = _trunk(
        x_prompt, pkb.reshape(DEPTH, batch, n_mem, D_MODEL), pvb.reshape(DEPTH, batch, n_mem, D_MODEL),
        z_c, z_n, z_m, z_s, z_conv, W)

    y_s, s_c, s_n, s_m, s_s, s_conv = _trunk(
        x_sample, cache_mem_k, cache_mem_v,
        state_mlstm_C, state_mlstm_n, state_mlstm_m, state_gdn_S, state_gdn_conv, W)

    kv_shape = (DEPTH, batch, n_mem, XA_HEADS, XA_DH)
    return (y_p, y_s, pk.reshape(kv_shape), pv.reshape(kv_shape), p_c, p_n, p_m, p_s, p_conv,
            s_c, s_n, s_m, s_s, s_conv)
```
